```python
import math
import jax, jax.numpy as jnp
from jax import lax
import numpy as np

D_MODEL = 4096
BATCH = 4
SEQ = 2048
DEPTH = 2
DEC_BATCH = 128
DEC_SEQ = 8
PAST_LEN = 16384
PAGE_SIZE = 128

EPS = 1e-6
MIX_WIDTH = D_MODEL
CHUNK = 128
W_A = D_MODEL // 4
G_A = 8
DH_A = W_A // G_A
W_B = D_MODEL // 2
H_B = 8
DV_B = W_B // H_B
DK_B = DV_B // 2
KW_B = H_B * DK_B
GATE_RANK = 16
GATE_TAU = 16.0
GLA_CHUNK = 64
W_C = D_MODEL // 4
G_C = 8
CONV_W = 3
D_FF = -(-8 * D_MODEL // (3 * 256)) * 256
SPLIT_SIZES = (W_A, W_A, KW_B, KW_B, W_B, W_B, GATE_RANK, W_C, W_C, W_C)
P_IN = sum(SPLIT_SIZES)
N_MOD = 6

kernel_name = "hybrid_chunkmlp_gla_shortconv_decoder_step"


def rms_norm(x, g):
    xf = x.astype(jnp.float32)
    y = xf * lax.rsqrt(jnp.mean(xf * xf, axis=-1, keepdims=True) + EPS)
    return (y * g.astype(jnp.float32)).astype(x.dtype)


def standardize(v):
    vf = v.astype(jnp.float32)
    mu = jnp.mean(vf, axis=-1, keepdims=True)
    var = jnp.mean(jnp.square(vf - mu), axis=-1, keepdims=True)
    return ((vf - mu) * lax.rsqrt(var + EPS)).astype(v.dtype)


def chunk_mlp(u, v, w_s, b_s):
    Bt, T, _ = u.shape
    L = min(T, CHUNK)
    nc = T // L
    u = jax.nn.gelu(u)
    vn = standardize(jax.nn.gelu(v).reshape(Bt, T, G_A, DH_A))
    mask = jnp.tril(jnp.ones((L, L), dtype=bool))
    ws = jnp.where(mask[None], w_s[:, :L, :L], jnp.zeros((), w_s.dtype))
    vc = vn.reshape(Bt, nc, L, G_A, DH_A)
    mixed = jnp.einsum('gij,bcjgd->bcigd', ws, vc) + b_s[:, :L].T[None, None, :, :, None]
    y = u * mixed.reshape(Bt, T, W_A)
    return y, vn.reshape(Bt, T, W_A)


def gla(q, k, v, log_a, s0):
    Bt, T = q.shape[:2]
    out_dtype = v.dtype
    L = math.gcd(T, GLA_CHUNK)
    nc = T // L
    f32 = jnp.float32
    q = q.astype(f32) * (DK_B ** -0.5)
    k, v, log_a = k.astype(f32), v.astype(f32), log_a.astype(f32)

    def to_chunks(a):
        return a.reshape(Bt, nc, L, *a.shape[2:]).swapaxes(0, 1)

    mask = jnp.tril(jnp.ones((L, L), dtype=bool))[None, :, :, None, None]

    def step(S, inp):
        qc, kc, vc, lac = inp
        G = jnp.cumsum(lac, axis=1)
        diff = G[:, :, None] - G[:, None, :]
        decay = jnp.exp(jnp.where(mask, diff, -jnp.inf))
        A = jnp.einsum('bijhk,bjhk->bhij', qc[:, :, None] * decay, kc)
        o_intra = jnp.einsum('bhij,bjhv->bihv', A, vc)
        o_inter = jnp.einsum('bihk,bhkv->bihv', qc * jnp.exp(G), S)
        G_last = G[:, -1]
        k_dec = kc * jnp.exp(G_last[:, None] - G)
        S_new = jnp.exp(G_last)[..., None] * S + jnp.einsum('bjhk,bjhv->bhkv', k_dec, vc)
        return S_new, o_intra + o_inter

    S_fin, o = lax.scan(step, s0.astype(f32), (to_chunks(q), to_chunks(k), to_chunks(v), to_chunks(log_a)))
    o = o.swapaxes(0, 1).reshape(Bt, T, H_B, DV_B)
    return o.astype(out_dtype), S_fin.astype(s0.dtype)


def short_conv(b_g, c_g, h, w_conv, buf):
    z = c_g * h
    zp = jnp.concatenate([buf.astype(z.dtype), z], axis=1)
    T = z.shape[1]
    y = zp[:, 0:T] * w_conv[0]
    for i in range(1, CONV_W):
        y = y + zp[:, i:i + T] * w_conv[i]
    return b_g * y, zp[:, -(CONV_W - 1):]


def layer(x, c, s_gla, buf_conv, g_mix, g_ffn, w_mod, b_mod, w_in, w_s, b_s, w_a2, b_a2,
          g_gla, w_conv, w_out, w_gate, w_up, w_down):
    Bt, T, _ = x.shape
    mod = jax.nn.silu(c) @ w_mod + b_mod
    sh1, sc1, gt1, sh2, sc2, gt2 = jnp.split(mod[:, None, :], N_MOD, axis=-1)
    h = rms_norm(x, g_mix) * (1 + sc1) + sh1
    z = h @ w_in
    idx = np.cumsum(SPLIT_SIZES)[:-1].tolist()
    u_a, v_a, q_b, k_b, v_b, r_b, a_b, b_c, c_c, h_c = jnp.split(z, idx, axis=-1)
    y_a, v_rows = chunk_mlp(u_a, v_a, w_s, b_s)
    log_a = jax.nn.log_sigmoid((a_b @ w_a2 + b_a2).astype(jnp.float32)) / GATE_TAU
    o_b, s_new = gla(q_b.reshape(Bt, T, H_B, DK_B), k_b.reshape(Bt, T, H_B, DK_B),
                     v_b.reshape(Bt, T, H_B, DV_B), log_a.reshape(Bt, T, H_B, DK_B), s_gla)
    y_b = rms_norm(o_b, g_gla).reshape(Bt, T, W_B) * jax.nn.silu(r_b)
    y_c, buf_new = short_conv(b_c, c_c, h_c, w_conv, buf_conv)
    x = x + gt1 * (jnp.concatenate([y_a, y_b, y_c], axis=-1) @ w_out)
    h2 = rms_norm(x, g_ffn) * (1 + sc2) + sh2
    x = x + gt2 * ((jax.nn.silu(h2 @ w_gate) * (h2 @ w_up)) @ w_down)
    return x, s_new, buf_new, v_rows


def trunk(x, c, s_gla, buf_conv, layer_weights, g_final):
    gla_out, conv_out, v_out = [], [], []
    for l in range(DEPTH):
        x, s_new, b_new, v_rows = layer(x, c, s_gla[l], buf_conv[l], *[w[l] for w in layer_weights])
        gla_out.append(s_new)
        conv_out.append(b_new)
        v_out.append(v_rows)
    y = rms_norm(x, g_final)
    return y, jnp.stack(gla_out), jnp.stack(conv_out), jnp.stack(v_out)


def setup_inputs(seed: int = 0) -> dict:
    key = jax.random.key(seed)
    ks = jax.random.split(key, 26)
    f32 = jnp.float32
    nrm = lambda k, shape, s: jax.random.normal(k, shape, f32) * s
    return {
        "x_prompt": nrm(ks[0], (BATCH, SEQ, D_MODEL), 1.0),
        "x_sample": nrm(ks[1], (DEC_BATCH, DEC_SEQ, D_MODEL), 1.0),
        "state_gla": nrm(ks[2], (DEPTH, DEC_BATCH, H_B, DK_B, DV_B), 1.0),
        "state_conv": nrm(ks[3], (DEPTH, DEC_BATCH, CONV_W - 1, W_C), 1.0),
        "c_prompt": nrm(ks[4], (BATCH, D_MODEL), 1.0),
        "c_sample": nrm(ks[5], (DEC_BATCH, D_MODEL), 1.0),
        "g_mix": 1.0 + nrm(ks[6], (DEPTH, D_MODEL), 0.02),
        "g_ffn": 1.0 + nrm(ks[7], (DEPTH, D_MODEL), 0.02),
        "w_mod": nrm(ks[8], (DEPTH, D_MODEL, N_MOD * D_MODEL), 0.5 * D_MODEL ** -0.5),
        "b_mod": nrm(ks[9], (DEPTH, N_MOD * D_MODEL), 0.02),
        "w_in": nrm(ks[10], (DEPTH, D_MODEL, P_IN), D_MODEL ** -0.5),
        "w_s": nrm(ks[11], (DEPTH, G_A, CHUNK, CHUNK), CHUNK ** -0.5),
        "b_s": 1.0 + nrm(ks[12], (DEPTH, G_A, CHUNK), 0.05),
        "w_a2": nrm(ks[13], (DEPTH, GATE_RANK, KW_B), GATE_RANK ** -0.5),
        "b_a2": nrm(ks[14], (DEPTH, KW_B), 0.1),
        "g_gla": 1.0 + nrm(ks[15], (DEPTH, DV_B), 0.02),
        "w_conv": nrm(ks[16], (DEPTH, CONV_W, W_C), CONV_W ** -0.5),
        "w_out": nrm(ks[17], (DEPTH, MIX_WIDTH, D_MODEL), MIX_WIDTH ** -0.5),
        "w_gate": nrm(ks[18], (DEPTH, D_MODEL, D_FF), D_MODEL ** -0.5),
        "w_up": nrm(ks[19], (DEPTH, D_MODEL, D_FF), D_MODEL ** -0.5),
        "w_down": nrm(ks[20], (DEPTH, D_FF, D_MODEL), D_FF ** -0.5),
        "g_final": 1.0 + nrm(ks[21], (D_MODEL,), 0.02),
    }


def reference(x_prompt, x_sample, state_gla, state_conv, c_prompt, c_sample, g_mix, g_ffn,
              w_mod, b_mod, w_in, w_s, b_s, w_a2, b_a2, g_gla, w_conv, w_out, w_gate, w_up,
              w_down, g_final):
    layer_weights = (g_mix, g_ffn, w_mod, b_mod, w_in, w_s, b_s, w_a2, b_a2, g_gla, w_conv,
                     w_out, w_gate, w_up, w_down)
    bp = x_prompt.shape[0]
    gla0 = jnp.zeros((DEPTH, bp, H_B, DK_B, DV_B), x_prompt.dtype)
    conv0 = jnp.zeros((DEPTH, bp, CONV_W - 1, W_C), x_prompt.dtype)
    y_prompt, gla_p, conv_p, _ = trunk(x_prompt, c_prompt, gla0, conv0, layer_weights, g_final)
    y_sample, gla_s, conv_s, chunk_v_s = trunk(x_sample, c_sample, state_gla, state_conv,
                                               layer_weights, g_final)
    return (y_prompt, y_sample, gla_p, conv_p, gla_s, conv_s, chunk_v_s)
```

```python
import functools

import numpy as np
import jax
import jax.numpy as jnp
from jax import lax
from jax.experimental import pallas as pl
from jax.experimental.pallas import tpu as pltpu

F32 = jnp.float32
BF16 = jnp.bfloat16

D_MODEL = 4096
DEPTH = 2
EPS = 1e-6
CHUNK = 128
W_A = D_MODEL // 4
G_A = 8
DH_A = W_A // G_A
W_B = D_MODEL // 2
H_B = 8
DV_B = W_B // H_B
DK_B = DV_B // 2
KW_B = H_B * DK_B
GATE_RANK = 16
GATE_TAU = 16.0
W_C = D_MODEL // 4
CONV_W = 3
D_FF = -(-8 * D_MODEL // (3 * 256)) * 256
N_MOD = 6
A_OFF = 2 * W_A + 2 * KW_B + 2 * W_B
P_MAIN = A_OFF + 3 * W_C

LANES = 128
ROW_TILE = 128
TM = 1024
VMEM_LIMIT = 56 * 1024 * 1024

ZC_U, ZC_V, ZC_B, ZC_C, ZC_H = 0, 1, 8, 9, 10
ZQ0 = (2 * W_A) // DK_B
ZK0 = (2 * W_A + KW_B) // DK_B
ZV0 = (2 * W_A + 2 * KW_B) // DV_B
ZR0 = (2 * W_A + 2 * KW_B + W_B) // DV_B


def _cparams(sem):
    return pltpu.CompilerParams(dimension_semantics=sem, vmem_limit_bytes=VMEM_LIMIT)


def _silu(x):
    return x / (1.0 + jnp.exp(-x))


def _gelu_tanh(x):
    c = np.float32(np.sqrt(2.0 / np.pi))
    return 0.5 * x * (1.0 + jnp.tanh(c * (x + 0.044715 * (x * x * x))))


def _mod_kernel(c_ref, w_ref, b_ref, o_ref):
    s = _silu(c_ref[...]).astype(BF16)
    o_ref[0] = jnp.dot(s, w_ref[0].astype(BF16), preferred_element_type=F32) + b_ref[0]


def _modulation(c_all, w_mod, b_mod):
    mc = c_all.shape[0]
    n = N_MOD * D_MODEL
    tn = 512
    return pl.pallas_call(
        _mod_kernel,
        out_shape=jax.ShapeDtypeStruct((DEPTH, mc, n), F32),
        grid=(DEPTH, n // tn),
        in_specs=[
            pl.BlockSpec((mc, D_MODEL), lambda l, j: (0, 0)),
            pl.BlockSpec((1, D_MODEL, tn), lambda l, j: (l, 0, j)),
            pl.BlockSpec((1, 1, tn), lambda l, j: (l, 0, j)),
        ],
        out_specs=pl.BlockSpec((1, mc, tn), lambda l, j: (l, 0, j)),
        compiler_params=_cparams(("parallel", "parallel")),
        name="modulation",
    )(c_all, w_mod, b_mod.reshape(DEPTH, 1, n))


def _norm_mod(x, g, sc, sh):
    y = x * lax.rsqrt(jnp.mean(x * x, axis=-1, keepdims=True) + EPS) * g
    return y * (1.0 + sc) + sh


def _norm_gate_kernel(x_ref, g_ref, sc_ref, sh_ref, wa_ref, wa2_ref, ba2_ref, h_ref, la_ref):
    bs, tt, d = x_ref.shape
    h = _norm_mod(x_ref[...], g_ref[...], sc_ref[...], sh_ref[...]).reshape(bs * tt, d).astype(BF16)
    h_ref[...] = h
    a = jnp.dot(h, wa_ref[...], preferred_element_type=F32)
    pre = jnp.dot(a.astype(BF16), wa2_ref[...], preferred_element_type=F32) + ba2_ref[...]
    log_sig = jnp.minimum(pre, 0.0) - jnp.log(1.0 + jnp.exp(-jnp.abs(pre)))
    la_ref[...] = (log_sig / GATE_TAU).reshape(bs, tt, KW_B)


def _norm_kernel(x_ref, g_ref, sc_ref, sh_ref, h_ref):
    bs, tt, d = x_ref.shape
    h_ref[...] = _norm_mod(x_ref[...], g_ref[...], sc_ref[...], sh_ref[...]).reshape(bs * tt, d).astype(BF16)


def _row_tiling(n_seq, t, rows):
    if t >= rows:
        return 1, rows
    return rows // t, t


def _norm_call(x, g, mod3, sc_idx, sh_idx, gate_w=None):
    n_seq, t, d = x.shape
    bs, tt = _row_tiling(n_seq, t, 256)
    rows = bs * tt
    nt = t // tt
    grid = (n_seq // bs, nt)
    x_spec = pl.BlockSpec((bs, tt, d), lambda i, j: (i, j, 0))
    g_spec = pl.BlockSpec((1, d), lambda i, j: (0, 0))
    mod_spec = lambda c: pl.BlockSpec((bs, 1, d), lambda i, j: (i, 0, c))
    h_spec = pl.BlockSpec((rows, d), lambda i, j: (i * nt + j, 0))
    h_shape = jax.ShapeDtypeStruct((n_seq * t, d), BF16)
    if gate_w is None:
        return pl.pallas_call(
            _norm_kernel, out_shape=h_shape, grid=grid,
            in_specs=[x_spec, g_spec, mod_spec(sc_idx), mod_spec(sh_idx)],
            out_specs=h_spec, compiler_params=_cparams(("parallel", "parallel")),
            name="norm_mod")(x, g, mod3, mod3)
    wa, wa2, ba2 = gate_w
    return pl.pallas_call(
        _norm_gate_kernel,
        out_shape=(h_shape, jax.ShapeDtypeStruct((n_seq, t, KW_B), F32)),
        grid=grid,
        in_specs=[x_spec, g_spec, mod_spec(sc_idx), mod_spec(sh_idx),
                  pl.BlockSpec((d, LANES), lambda i, j: (0, 0)),
                  pl.BlockSpec((LANES, KW_B), lambda i, j: (0, 0)),
                  pl.BlockSpec((1, KW_B), lambda i, j: (0, 0))],
        out_specs=(h_spec, pl.BlockSpec((bs, tt, KW_B), lambda i, j: (i, j, 0))),
        compiler_params=_cparams(("parallel", "parallel")),
        name="norm_mod_gate")(x, g, mod3, mod3, wa, wa2, ba2)


def _final_norm_kernel(x_ref, g_ref, o_ref):
    x = x_ref[...]
    o_ref[...] = x * lax.rsqrt(jnp.mean(x * x, axis=-1, keepdims=True) + EPS) * g_ref[...]


def _final_norm(x, g):
    n_seq, t, d = x.shape
    bs, tt = _row_tiling(n_seq, t, 256)
    return pl.pallas_call(
        _final_norm_kernel, out_shape=jax.ShapeDtypeStruct(x.shape, F32),
        grid=(n_seq // bs, t // tt),
        in_specs=[pl.BlockSpec((bs, tt, d), lambda i, j: (i, j, 0)),
                  pl.BlockSpec((1, d), lambda i, j: (0, 0))],
        out_specs=pl.BlockSpec((bs, tt, d), lambda i, j: (i, j, 0)),
        compiler_params=_cparams(("parallel", "parallel")),
        name="final_norm")(x, g)


def _in_proj_kernel(h_ref, w_ref, z_ref):
    acc = jnp.dot(h_ref[...], w_ref[...], preferred_element_type=F32)
    z_ref[...] = acc.reshape(z_ref.shape)


def _in_proj(h, w, n_seq, t):
    rows, k = h.shape
    n = w.shape[1]
    bs, tt = _row_tiling(n_seq, t, TM)
    nt = t // tt
    tn = 1024
    return pl.pallas_call(
        _in_proj_kernel, out_shape=jax.ShapeDtypeStruct((n_seq, t, n), F32),
        grid=(rows // TM, n // tn),
        in_specs=[pl.BlockSpec((TM, k), lambda i, j: (i, 0)),
                  pl.BlockSpec((k, tn), lambda i, j: (0, j))],
        out_specs=pl.BlockSpec((bs, tt, tn), lambda i, j: (i // nt, i % nt, j)),
        compiler_params=_cparams(("parallel", "parallel")),
        name="in_proj")(h, w)


def _out_proj_kernel(ya_ref, yb_ref, yc_ref, w_ref, x_ref, gt_ref, o_ref):
    acc = jnp.dot(ya_ref[...], w_ref[0:W_A, :], preferred_element_type=F32)
    acc += jnp.dot(yb_ref[...], w_ref[W_A:W_A + W_B, :], preferred_element_type=F32)
    acc += jnp.dot(yc_ref[...], w_ref[W_A + W_B:, :], preferred_element_type=F32)
    o_ref[...] = x_ref[...] + gt_ref[...] * acc.reshape(o_ref.shape)


def _out_proj(ya, yb, yc, w, x, mod3, gt_idx):
    n_seq, t, d = x.shape
    rows = n_seq * t
    bs, tt = _row_tiling(n_seq, t, TM)
    nt = t // tt
    tn = 512
    xo_spec = pl.BlockSpec((bs, tt, tn), lambda i, j: (i // nt, i % nt, j))
    return pl.pallas_call(
        _out_proj_kernel, out_shape=jax.ShapeDtypeStruct(x.shape, F32),
        grid=(rows // TM, d // tn),
        in_specs=[pl.BlockSpec((TM, W_A), lambda i, j: (i, 0)),
                  pl.BlockSpec((TM, W_B), lambda i, j: (i, 0)),
                  pl.BlockSpec((TM, W_C), lambda i, j: (i, 0)),
                  pl.BlockSpec((d, tn), lambda i, j: (0, j)),
                  xo_spec,
                  pl.BlockSpec((bs, 1, tn), lambda i, j: (i // nt, 0, gt_idx * (d // tn) + j))],
        out_specs=xo_spec,
        compiler_params=_cparams(("parallel", "parallel")),
        name="out_proj")(ya, yb, yc, w, x, mod3)


def _swiglu_kernel(h_ref, wg_ref, wu_ref, a_ref):
    h = h_ref[...]
    g = jnp.dot(h, wg_ref[...], preferred_element_type=F32)
    u = jnp.dot(h, wu_ref[...], preferred_element_type=F32)
    a_ref[...] = (_silu(g) * u).astype(BF16)


def _swiglu(h, wg, wu):
    rows, k = h.shape
    n = wg.shape[1]
    tn = 256
    w_spec = pl.BlockSpec((k, tn), lambda i, j: (0, j))
    return pl.pallas_call(
        _swiglu_kernel, out_shape=jax.ShapeDtypeStruct((rows, n), BF16),
        grid=(rows // TM, n // tn),
        in_specs=[pl.BlockSpec((TM, k), lambda i, j: (i, 0)), w_spec, w_spec],
        out_specs=pl.BlockSpec((TM, tn), lambda i, j: (i, j)),
        compiler_params=_cparams(("parallel", "parallel")),
        name="swiglu")(h, wg, wu)


def _down_proj_kernel(a_ref, w_ref, x_ref, gt_ref, o_ref, acc_ref):
    kk = pl.program_id(2)
    part = jnp.dot(a_ref[...], w_ref[...], preferred_element_type=F32)

    @pl.when(kk == 0)
    def _():
        acc_ref[...] = part

    @pl.when(kk == pl.num_programs(2) - 1)
    def _():
        o_ref[...] = x_ref[...] + gt_ref[...] * (acc_ref[...] + part).reshape(o_ref.shape)


def _down_proj(a, w, x, mod3, gt_idx):
    n_seq, t, d = x.shape
    rows, k = a.shape
    bs, tt = _row_tiling(n_seq, t, TM)
    nt = t // tt
    tn = 512
    k_steps = 2
    tk = k // k_steps
    xo_spec = pl.BlockSpec((bs, tt, tn), lambda i, j, kk: (i // nt, i % nt, j))
    return pl.pallas_call(
        _down_proj_kernel, out_shape=jax.ShapeDtypeStruct(x.shape, F32),
        grid=(rows // TM, d // tn, k_steps),
        in_specs=[pl.BlockSpec((TM, tk), lambda i, j, kk: (i, kk)),
                  pl.BlockSpec((tk, tn), lambda i, j, kk: (kk, j)),
                  xo_spec,
                  pl.BlockSpec((bs, 1, tn), lambda i, j, kk: (i // nt, 0, gt_idx * (d // tn) + j))],
        out_specs=xo_spec,
        scratch_shapes=[pltpu.VMEM((TM, tn), F32)],
        compiler_params=_cparams(("parallel", "parallel", "arbitrary")),
        name="down_proj")(a, w, x, mod3)


def _mix_ac_kernel(u_ref, v_ref, b_ref, c_ref, hc_ref, wt_ref, bias_ref, wc_ref, buf_ref,
                   ya_ref, yc_ref, vn_ref, bufnew_ref, zbuf, *, seq_len):
    bs, tt, w = u_ref.shape
    rows = bs * tt
    t = pl.program_id(1)
    pad = 8

    @pl.when(t == 0)
    def _():
        zbuf[:, pad - 2:pad, :] = buf_ref[...]

    zc = c_ref[...] * hc_ref[...]
    zbuf[:, pad:pad + tt, :] = zc
    y = zbuf[:, pad - 2:pad - 2 + tt, :] * wc_ref[0:1, :]
    y = y + zbuf[:, pad - 1:pad - 1 + tt, :] * wc_ref[1:2, :]
    y = y + zc * wc_ref[2:3, :]
    yc_ref[...] = (b_ref[...] * y).reshape(rows, w).astype(BF16)
    last2 = zbuf[:, pad + tt - 2:pad + tt, :]
    zbuf[:, pad - 2:pad, :] = last2

    @pl.when(t == pl.num_programs(1) - 1)
    def _():
        bufnew_ref[...] = last2

    gu = _gelu_tanh(u_ref[...].reshape(rows, w))
    gv = _gelu_tanh(v_ref[...].reshape(rows, w))
    ri = lax.broadcasted_iota(jnp.int32, (rows, rows), 0)
    ci = lax.broadcasted_iota(jnp.int32, (rows, rows), 1)
    keep = ci <= ri
    if seq_len < rows:
        keep = keep & ((ri // seq_len) == (ci // seq_len))
    for g in range(G_A):
        sl = slice(g * DH_A, (g + 1) * DH_A)
        vg = gv[:, sl]
        dv = vg - jnp.mean(vg, axis=-1, keepdims=True)
        vn = dv * lax.rsqrt(jnp.mean(dv * dv, axis=-1, keepdims=True) + EPS)
        wm = jnp.where(keep, wt_ref[g], 0.0).astype(BF16)
        mixed = jnp.dot(wm, vn.astype(BF16), preferred_element_type=F32) + bias_ref[:, sl]
        ya_ref[:, sl] = (gu[:, sl] * mixed).astype(BF16)
        vn_ref[:, :, sl] = vn.reshape(bs, tt, DH_A)


def _mix_ac(z, wt, bias_full, w_conv, buf):
    n_seq, t, _ = z.shape
    bs, tt = _row_tiling(n_seq, t, ROW_TILE)
    nt = t // tt
    rows = n_seq * t
    zspec = lambda c: pl.BlockSpec((bs, tt, W_A), lambda i, j: (i, j, c))
    y_spec = pl.BlockSpec((ROW_TILE, W_A), lambda i, j: (i * nt + j, 0))
    buf_spec = pl.BlockSpec((bs, CONV_W - 1, W_C), lambda i, j: (i, 0, 0))
    return pl.pallas_call(
        functools.partial(_mix_ac_kernel, seq_len=min(t, CHUNK)),
        out_shape=(jax.ShapeDtypeStruct((rows, W_A), BF16),
                   jax.ShapeDtypeStruct((rows, W_C), BF16),
                   jax.ShapeDtypeStruct((n_seq, t, W_A), F32),
                   jax.ShapeDtypeStruct((n_seq, CONV_W - 1, W_C), F32)),
        grid=(n_seq // bs, nt),
        in_specs=[zspec(ZC_U), zspec(ZC_V), zspec(ZC_B), zspec(ZC_C), zspec(ZC_H),
                  pl.BlockSpec((G_A, ROW_TILE, ROW_TILE), lambda i, j: (0, 0, 0)),
                  pl.BlockSpec((ROW_TILE, W_A), lambda i, j: (0, 0)),
                  pl.BlockSpec((CONV_W, W_C), lambda i, j: (0, 0)),
                  buf_spec],
        out_specs=(y_spec, y_spec, pl.BlockSpec((bs, tt, W_A), lambda i, j: (i, j, 0)), buf_spec),
        scratch_shapes=[pltpu.VMEM((bs, tt + 8, W_C), F32)],
        compiler_params=_cparams(("parallel", "arbitrary")),
        name="mix_ac")(z, z, z, z, z, wt, bias_full, w_conv, buf)


def _gla_tables(rows, seq_len):
    i = np.arange(rows)[:, None]
    t = np.arange(rows)[None, :]
    same = (i // seq_len) == (t // seq_len)
    slabs = [same & (t <= i), same & (t > i)]
    masks = [i == t]
    s = seq_len // 2
    while s >= 1:
        second = (i // s) % 2 == 1
        start2 = (i // s) * s
        end1 = start2 + s - 1
        slabs.append(np.where(second, (t >= start2) & (t <= i), (t > i) & (t <= end1)))
        masks.append(((i // (2 * s)) == (t // (2 * s))) & second & ((t // s) % 2 == 0))
        s //= 2
    return (np.concatenate(slabs, 0).astype(np.float32), np.stack(masks).astype(np.float32))


def _gla_kernel(q_ref, k_ref, v_ref, r_ref, la_ref, mall_ref, mask_ref, g_ref, s0_ref,
                y_ref, snew_ref, s_scr, qg_scr, kd_scr, eg_scr, o_scr):
    bs, tt, _ = q_ref.shape
    rows = bs * tt
    n_lvl = mask_ref.shape[0] - 1
    t = pl.program_id(2)

    @pl.when(t == 0)
    def _():
        s_scr[...] = s0_ref[:, 0]

    q = q_ref[...].reshape(rows, DK_B) * (DK_B ** -0.5)
    k = k_ref[...].reshape(rows, DK_B)
    vb = v_ref[...].reshape(rows, DV_B).astype(BF16)
    la = la_ref[...].reshape(rows, DK_B)
    la_hi = la.astype(BF16)
    la_lo = (la - la_hi.astype(F32)).astype(BF16)
    x = jnp.dot(mall_ref[...], jnp.concatenate([la_hi, la_lo], axis=1), preferred_element_type=F32)
    e = jnp.exp(x[:, :DK_B] + x[:, DK_B:])

    nt_dims = (((1,), (1,)), ((), ()))
    a = mask_ref[0] * lax.dot_general(q.astype(BF16), k.astype(BF16), nt_dims, preferred_element_type=F32)
    for lvl in range(n_lvl):
        el = e[(2 + lvl) * rows:(3 + lvl) * rows]
        p = lax.dot_general((q * el).astype(BF16), (k * el).astype(BF16), nt_dims,
                            preferred_element_type=F32)
        a = a + mask_ref[lvl + 1] * p
    o_scr[...] = jnp.dot(a.astype(BF16), vb, preferred_element_type=F32).reshape(bs, tt, DV_B)
    e_g = e[0:rows]
    eg_scr[...] = e_g.reshape(bs, tt, DK_B)
    qg_scr[...] = (q * e_g).reshape(bs, tt, DK_B)
    kd_scr[...] = (k * e[rows:2 * rows]).reshape(bs, tt, DK_B)

    ri = lax.broadcasted_iota(jnp.int32, (DK_B, DK_B), 0)
    ci = lax.broadcasted_iota(jnp.int32, (DK_B, DK_B), 1)
    eye = ri == ci

    def per_seq(b, carry):
        s = s_scr[b]
        o_scr[b] += jnp.dot(qg_scr[b].astype(BF16), s.astype(BF16), preferred_element_type=F32)
        e_last = eg_scr[b, tt - 1:tt, :]
        e_col = jnp.sum(jnp.where(eye, jnp.broadcast_to(e_last, (DK_B, DK_B)), 0.0),
                        axis=1, keepdims=True)
        kv = lax.dot_general(kd_scr[b].astype(BF16), v_ref[b].astype(BF16),
                             (((0,), (0,)), ((), ())), preferred_element_type=F32)
        s_scr[b] = e_col * s + kv
        return carry

    if bs == 1:
        per_seq(0, 0)
    else:
        lax.fori_loop(0, bs, per_seq, 0)

    o = o_scr[...].reshape(rows, DV_B)
    yn = o * lax.rsqrt(jnp.mean(o * o, axis=-1, keepdims=True) + EPS) * g_ref[...]
    y_ref[...] = (yn * _silu(r_ref[...].reshape(rows, DV_B))).astype(BF16)

    @pl.when(t == pl.num_programs(2) - 1)
    def _():
        snew_ref[:, 0] = s_scr[...]


def _gla(z, la, g_gla, s0):
    n_seq, t, _ = z.shape
    bs, tt = _row_tiling(n_seq, t, ROW_TILE)
    nt = t // tt
    rows = n_seq * t
    mall, masks = _gla_tables(ROW_TILE, tt)
    mall = jnp.asarray(mall, BF16)
    masks = jnp.asarray(masks, F32)
    zs = lambda w, c0: pl.BlockSpec((bs, tt, w), lambda i, h, j: (i, j, c0 + h))
    s_spec = pl.BlockSpec((bs, 1, DK_B, DV_B), lambda i, h, j: (i, h, 0, 0))
    return pl.pallas_call(
        _gla_kernel,
        out_shape=(jax.ShapeDtypeStruct((rows, W_B), BF16),
                   jax.ShapeDtypeStruct((n_seq, H_B, DK_B, DV_B), F32)),
        grid=(n_seq // bs, H_B, nt),
        in_specs=[zs(DK_B, ZQ0), zs(DK_B, ZK0), zs(DV_B, ZV0), zs(DV_B, ZR0),
                  pl.BlockSpec((bs, tt, DK_B), lambda i, h, j: (i, j, h)),
                  pl.BlockSpec(mall.shape, lambda i, h, j: (0, 0)),
                  pl.BlockSpec(masks.shape, lambda i, h, j: (0, 0, 0)),
                  pl.BlockSpec((1, DV_B), lambda i, h, j: (0, 0)),
                  s_spec],
        out_specs=(pl.BlockSpec((ROW_TILE, DV_B), lambda i, h, j: (i * nt + j, h)), s_spec),
        scratch_shapes=[pltpu.VMEM((bs, DK_B, DV_B), F32),
                        pltpu.VMEM((bs, tt, DK_B), F32),
                        pltpu.VMEM((bs, tt, DK_B), F32),
                        pltpu.VMEM((bs, tt, DK_B), F32),
                        pltpu.VMEM((bs, tt, DV_B), F32)],
        compiler_params=_cparams(("parallel", "parallel", "arbitrary")),
        name="gla")(z, z, z, z, la, mall, masks, g_gla, s0)


def _trunk(x, mod, s_gla, buf_conv, lw, g_final):
    n_seq, t, d = x.shape
    seq_len = min(t, CHUNK)
    reps = ROW_TILE // seq_len
    gla_out, conv_out, v_out = [], [], []
    for l in range(DEPTH):
        p = lw[l]
        mod3 = mod[l]
        h, la = _norm_call(x, p["g_mix"], mod3, 1, 0, gate_w=(p["w_a"], p["w_a2"], p["b_a2"]))
        z = _in_proj(h, p["w_main"], n_seq, t)
        wt = jnp.tile(p["w_s"][:, :seq_len, :seq_len], (1, reps, reps))
        bias_full = jnp.repeat(jnp.tile(p["b_s"][:, :seq_len].T, (reps, 1)), DH_A, axis=1)
        ya, yc, vn, buf_new = _mix_ac(z, wt, bias_full, p["w_conv"], buf_conv[l])
        yb, s_new = _gla(z, la, p["g_gla"], s_gla[l])
        x = _out_proj(ya, yb, yc, p["w_out"], x, mod3, 2)
        h2 = _norm_call(x, p["g_ffn"], mod3, 4, 3)
        a = _swiglu(h2, p["w_gate"], p["w_up"])
        x = _down_proj(a, p["w_down"], x, mod3, 5)
        gla_out.append(s_new)
        conv_out.append(buf_new)
        v_out.append(vn)
    y = _final_norm(x, g_final)
    return y, jnp.stack(gla_out), jnp.stack(conv_out), jnp.stack(v_out)


def kernel(x_prompt, x_sample, state_gla, state_conv, c_prompt, c_sample, g_mix, g_ffn, w_mod, b_mod,
           w_in, w_s, b_s, w_a2, b_a2, g_gla, w_conv, w_out, w_gate, w_up, w_down, g_final):
    bp = x_prompt.shape[0]
    bd = x_sample.shape[0]
    n_c = bp + bd
    mc = -(-n_c // 16) * 16
    c_all = jnp.concatenate([c_prompt, c_sample, jnp.zeros((mc - n_c, D_MODEL), F32)], axis=0)
    mod = _modulation(c_all, w_mod, b_mod)
    mod_p = [mod[l, :bp].reshape(bp, 1, N_MOD * D_MODEL) for l in range(DEPTH)]
    mod_s = [mod[l, bp:n_c].reshape(bd, 1, N_MOD * D_MODEL) for l in range(DEPTH)]

    lw = []
    for l in range(DEPTH):
        wi = w_in[l]
        lw.append(dict(
            g_mix=g_mix[l].reshape(1, D_MODEL), g_ffn=g_ffn[l].reshape(1, D_MODEL),
            w_main=jnp.concatenate([wi[:, :A_OFF], wi[:, A_OFF + GATE_RANK:]], axis=1).astype(BF16),
            w_a=jnp.pad(wi[:, A_OFF:A_OFF + GATE_RANK], ((0, 0), (0, LANES - GATE_RANK))).astype(BF16),
            w_a2=jnp.pad(w_a2[l], ((0, LANES - GATE_RANK), (0, 0))).astype(BF16),
            b_a2=b_a2[l].reshape(1, KW_B),
            w_s=w_s[l], b_s=b_s[l], g_gla=g_gla[l].reshape(1, DV_B), w_conv=w_conv[l],
            w_out=w_out[l].astype(BF16), w_gate=w_gate[l].astype(BF16),
            w_up=w_up[l].astype(BF16), w_down=w_down[l].astype(BF16)))
    gf = g_final.reshape(1, D_MODEL)

    gla0 = jnp.zeros((DEPTH, bp, H_B, DK_B, DV_B), F32)
    conv0 = jnp.zeros((DEPTH, bp, CONV_W - 1, W_C), F32)
    y_p, gla_p, conv_p, _ = _trunk(x_prompt, mod_p, gla0, conv0, lw, gf)
    y_s, gla_s, conv_s, v_s = _trunk(x_sample, mod_s, state_gla, state_conv, lw, gf)
    return (y_p, y_s, gla_p, conv_p, gla_s, conv_s, v_s)
```

```python
import functools

import numpy as np
import jax
import jax.numpy as jnp
from jax import lax
from jax.experimental import pallas as pl
from jax.experimental.pallas import tpu as pltpu

F32 = jnp.float32
BF16 = jnp.bfloat16

D_MODEL = 4096
DEPTH = 2
EPS = 1e-6
CHUNK = 128
W_A = D_MODEL // 4
G_A = 8
DH_A = W_A // G_A
W_B = D_MODEL // 2
H_B = 8
DV_B = W_B // H_B
DK_B = DV_B // 2
KW_B = H_B * DK_B
GATE_RANK = 16
GATE_TAU = 16.0
W_C = D_MODEL // 4
CONV_W = 3
D_FF = -(-8 * D_MODEL // (3 * 256)) * 256
N_MOD = 6
A_OFF = 2 * W_A + 2 * KW_B + 2 * W_B
P_TAIL = 3 * W_C

LANES = 128
ROW_TILE = 128
TM = 1024
VMEM_LIMIT = 56 * 1024 * 1024

ZQ0 = (2 * W_A) // DK_B
ZK0 = (2 * W_A + KW_B) // DK_B
ZV0 = (2 * W_A + 2 * KW_B) // DV_B
ZR0 = (2 * W_A + 2 * KW_B + W_B) // DV_B


def _cparams(sem):
    return pltpu.CompilerParams(dimension_semantics=sem, vmem_limit_bytes=VMEM_LIMIT)


def _silu(x):
    return x / (1.0 + jnp.exp(-x))


def _gelu_tanh(x):
    c = np.float32(np.sqrt(2.0 / np.pi))
    return 0.5 * x * (1.0 + jnp.tanh(c * (x + 0.044715 * (x * x * x))))


def _row_tiling(t, rows):
    if t >= rows:
        return 1, rows
    return rows // t, t


def _w_spec(w, layer, k, tn, col_map):
    if w.ndim == 2:
        return pl.BlockSpec((k, tn), lambda *g: (0, col_map(*g)))
    return pl.BlockSpec((None, k, tn), lambda *g: (layer, 0, col_map(*g)))


def _mod_kernel(c_ref, w_ref, b_ref, o_ref):
    s = _silu(c_ref[...]).astype(BF16)
    o_ref[0] = jnp.dot(s, w_ref[0].astype(BF16), preferred_element_type=F32) + b_ref[0]


def _modulation(c_all, w_mod, b_mod):
    mc = c_all.shape[0]
    n = N_MOD * D_MODEL
    tn = 512
    return pl.pallas_call(
        _mod_kernel,
        out_shape=jax.ShapeDtypeStruct((DEPTH, mc, n), F32),
        grid=(DEPTH, n // tn),
        in_specs=[
            pl.BlockSpec((mc, D_MODEL), lambda l, j: (0, 0)),
            pl.BlockSpec((1, D_MODEL, tn), lambda l, j: (l, 0, j)),
            pl.BlockSpec((1, 1, tn), lambda l, j: (l, 0, j)),
        ],
        out_specs=pl.BlockSpec((1, mc, tn), lambda l, j: (l, 0, j)),
        compiler_params=_cparams(("parallel", "parallel")),
        name="modulation",
    )(c_all, w_mod, b_mod.reshape(DEPTH, 1, n))


def _norm_mod(x, g, sc, sh):
    y = x * lax.rsqrt(jnp.mean(x * x, axis=-1, keepdims=True) + EPS) * g
    return y * (1.0 + sc) + sh


def _norm_gate_kernel(x_ref, g_ref, sc_ref, sh_ref, wa_ref, wa2_ref, ba2_ref, h_ref, la_ref):
    bs, tt, d = x_ref.shape
    h = _norm_mod(x_ref[...], g_ref[...], sc_ref[...], sh_ref[...]).reshape(bs * tt, d).astype(BF16)
    h_ref[...] = h
    a = jnp.dot(h, wa_ref[...], preferred_element_type=F32)
    pre = jnp.dot(a.astype(BF16), wa2_ref[...], preferred_element_type=F32) + ba2_ref[...]
    log_sig = jnp.minimum(pre, 0.0) - jnp.log(1.0 + jnp.exp(-jnp.abs(pre)))
    la_ref[...] = (log_sig / GATE_TAU).reshape(bs, tt, KW_B)


def _norm_kernel(x_ref, g_ref, sc_ref, sh_ref, h_ref):
    bs, tt, d = x_ref.shape
    h_ref[...] = _norm_mod(x_ref[...], g_ref[...], sc_ref[...], sh_ref[...]).reshape(bs * tt, d).astype(BF16)


def _norm_call(x, g, mod3, sc_idx, sh_idx, gate_w=None):
    n_seq, t, d = x.shape
    bs, tt = _row_tiling(t, 256)
    rows = bs * tt
    nt = t // tt
    grid = (n_seq // bs, nt)
    x_spec = pl.BlockSpec((bs, tt, d), lambda i, j: (i, j, 0))
    g_spec = pl.BlockSpec((1, d), lambda i, j: (0, 0))
    mod_spec = lambda c: pl.BlockSpec((bs, 1, d), lambda i, j: (i, 0, c))
    h_spec = pl.BlockSpec((rows, d), lambda i, j: (i * nt + j, 0))
    h_shape = jax.ShapeDtypeStruct((n_seq * t, d), BF16)
    if gate_w is None:
        return pl.pallas_call(
            _norm_kernel, out_shape=h_shape, grid=grid,
            in_specs=[x_spec, g_spec, mod_spec(sc_idx), mod_spec(sh_idx)],
            out_specs=h_spec, compiler_params=_cparams(("parallel", "parallel")),
            name="norm_mod")(x, g, mod3, mod3)
    wa, wa2, ba2 = gate_w
    return pl.pallas_call(
        _norm_gate_kernel,
        out_shape=(h_shape, jax.ShapeDtypeStruct((n_seq, t, KW_B), F32)),
        grid=grid,
        in_specs=[x_spec, g_spec, mod_spec(sc_idx), mod_spec(sh_idx),
                  pl.BlockSpec((d, LANES), lambda i, j: (0, 0)),
                  pl.BlockSpec((LANES, KW_B), lambda i, j: (0, 0)),
                  pl.BlockSpec((1, KW_B), lambda i, j: (0, 0))],
        out_specs=(h_spec, pl.BlockSpec((bs, tt, KW_B), lambda i, j: (i, j, 0))),
        compiler_params=_cparams(("parallel", "parallel")),
        name="norm_mod_gate")(x, g, mod3, mod3, wa, wa2, ba2)


def _final_norm_kernel(x_ref, g_ref, o_ref):
    x = x_ref[...]
    o_ref[...] = x * lax.rsqrt(jnp.mean(x * x, axis=-1, keepdims=True) + EPS) * g_ref[...]


def _final_norm(x, g):
    n_seq, t, d = x.shape
    bs, tt = _row_tiling(t, 256)
    return pl.pallas_call(
        _final_norm_kernel, out_shape=jax.ShapeDtypeStruct(x.shape, F32),
        grid=(n_seq // bs, t // tt),
        in_specs=[pl.BlockSpec((bs, tt, d), lambda i, j: (i, j, 0)),
                  pl.BlockSpec((1, d), lambda i, j: (0, 0))],
        out_specs=pl.BlockSpec((bs, tt, d), lambda i, j: (i, j, 0)),
        compiler_params=_cparams(("parallel", "parallel")),
        name="final_norm")(x, g)


def _bf16_tile(w_ref, wb_ref):
    if wb_ref is None:
        return w_ref[...]
    wb = w_ref[...].astype(BF16)
    wb_ref[...] = wb
    return wb


def _in_proj_kernel(h_ref, w_ref, z_ref, wb_ref=None):
    acc = jnp.dot(h_ref[...], _bf16_tile(w_ref, wb_ref), preferred_element_type=F32)
    z_ref[...] = acc.reshape(z_ref.shape)


def _in_proj(h, w, layer, n_cols, n_seq, t):
    rows, k = h.shape
    emit = w.dtype != BF16
    bs, tt = _row_tiling(t, TM)
    nt = t // tt
    tn = 512 if emit else 1024
    out_shape = [jax.ShapeDtypeStruct((n_seq, t, n_cols), F32)]
    out_specs = [pl.BlockSpec((bs, tt, tn), lambda i, j: (i // nt, i % nt, j))]
    if emit:
        assert rows == TM
        out_shape.append(jax.ShapeDtypeStruct((k, n_cols), BF16))
        out_specs.append(pl.BlockSpec((k, tn), lambda i, j: (0, j)))
    res = pl.pallas_call(
        _in_proj_kernel, out_shape=out_shape,
        grid=(rows // TM, n_cols // tn),
        in_specs=[pl.BlockSpec((TM, k), lambda i, j: (i, 0)),
                  _w_spec(w, layer, k, tn, lambda i, j: j)],
        out_specs=out_specs,
        compiler_params=_cparams(("parallel", "parallel")),
        name="in_proj_cast" if emit else "in_proj")(h, w)
    return (res[0], res[1]) if emit else (res[0], None)


def _out_proj_kernel(ya_ref, yb_ref, yc_ref, w_ref, x_ref, gt_ref, o_ref, wb_ref=None):
    if wb_ref is not None:
        wb_ref[...] = w_ref[...].astype(BF16)
        w_ref = wb_ref
    acc = jnp.dot(ya_ref[...], w_ref[0:W_A, :], preferred_element_type=F32)
    acc += jnp.dot(yb_ref[...], w_ref[W_A:W_A + W_B, :], preferred_element_type=F32)
    acc += jnp.dot(yc_ref[...], w_ref[W_A + W_B:, :], preferred_element_type=F32)
    o_ref[...] = x_ref[...] + gt_ref[...] * acc.reshape(o_ref.shape)


def _out_proj(ya, yb, yc, w, layer, x, mod3, gt_idx):
    n_seq, t, d = x.shape
    rows = n_seq * t
    emit = w.dtype != BF16
    bs, tt = _row_tiling(t, TM)
    nt = t // tt
    tn = 512
    xo_spec = pl.BlockSpec((bs, tt, tn), lambda i, j: (i // nt, i % nt, j))
    out_shape = [jax.ShapeDtypeStruct(x.shape, F32)]
    out_specs = [xo_spec]
    if emit:
        assert rows == TM
        out_shape.append(jax.ShapeDtypeStruct((d, d), BF16))
        out_specs.append(pl.BlockSpec((d, tn), lambda i, j: (0, j)))
    res = pl.pallas_call(
        _out_proj_kernel, out_shape=out_shape,
        grid=(rows // TM, d // tn),
        in_specs=[pl.BlockSpec((TM, W_A), lambda i, j: (i, 0)),
                  pl.BlockSpec((TM, W_B), lambda i, j: (i, 0)),
                  pl.BlockSpec((TM, W_C), lambda i, j: (i, 0)),
                  _w_spec(w, layer, d, tn, lambda i, j: j),
                  xo_spec,
                  pl.BlockSpec((bs, 1, tn), lambda i, j: (i // nt, 0, gt_idx * (d // tn) + j))],
        out_specs=out_specs,
        compiler_params=_cparams(("parallel", "parallel")),
        name="out_proj_cast" if emit else "out_proj")(ya, yb, yc, w, x, mod3)
    return (res[0], res[1]) if emit else (res[0], None)


def _swiglu_kernel(h_ref, wg_ref, wu_ref, a_ref, wgb_ref=None, wub_ref=None):
    h = h_ref[...]
    g = jnp.dot(h, _bf16_tile(wg_ref, wgb_ref), preferred_element_type=F32)
    u = jnp.dot(h, _bf16_tile(wu_ref, wub_ref), preferred_element_type=F32)
    a_ref[...] = (_silu(g) * u).astype(BF16)


def _swiglu(h, wg, wu, layer):
    rows, k = h.shape
    n = wg.shape[-1]
    emit = wg.dtype != BF16
    tn = 256
    out_shape = [jax.ShapeDtypeStruct((rows, n), BF16)]
    out_specs = [pl.BlockSpec((TM, tn), lambda i, j: (i, j))]
    if emit:
        assert rows == TM
        out_shape += [jax.ShapeDtypeStruct((k, n), BF16)] * 2
        out_specs += [pl.BlockSpec((k, tn), lambda i, j: (0, j))] * 2
    res = pl.pallas_call(
        _swiglu_kernel, out_shape=out_shape,
        grid=(rows // TM, n // tn),
        in_specs=[pl.BlockSpec((TM, k), lambda i, j: (i, 0)),
                  _w_spec(wg, layer, k, tn, lambda i, j: j),
                  _w_spec(wu, layer, k, tn, lambda i, j: j)],
        out_specs=out_specs,
        compiler_params=_cparams(("parallel", "parallel")),
        name="swiglu_cast" if emit else "swiglu")(h, wg, wu)
    return (res[0], res[1], res[2]) if emit else (res[0], None, None)


def _down_proj_kernel(a_ref, w_ref, x_ref, gt_ref, o_ref, *rest):
    acc_ref = rest[-1]
    wb_ref = rest[0] if len(rest) == 2 else None
    kk = pl.program_id(2)
    part = jnp.dot(a_ref[...], _bf16_tile(w_ref, wb_ref), preferred_element_type=F32)

    @pl.when(kk == 0)
    def _():
        acc_ref[...] = part

    @pl.when(kk == pl.num_programs(2) - 1)
    def _():
        o_ref[...] = x_ref[...] + gt_ref[...] * (acc_ref[...] + part).reshape(o_ref.shape)


def _down_proj(a, w, layer, x, mod3, gt_idx):
    n_seq, t, d = x.shape
    rows, k = a.shape
    emit = w.dtype != BF16
    bs, tt = _row_tiling(t, TM)
    nt = t // tt
    tn = 256 if emit else 512
    k_steps = 2
    tk = k // k_steps
    xo_spec = pl.BlockSpec((bs, tt, tn), lambda i, j, kk: (i // nt, i % nt, j))
    if w.ndim == 2:
        w_spec = pl.BlockSpec((tk, tn), lambda i, j, kk: (kk, j))
    else:
        w_spec = pl.BlockSpec((None, tk, tn), lambda i, j, kk: (layer, kk, j))
    out_shape = [jax.ShapeDtypeStruct(x.shape, F32)]
    out_specs = [xo_spec]
    if emit:
        assert rows == TM
        out_shape.append(jax.ShapeDtypeStruct((k, d), BF16))
        out_specs.append(pl.BlockSpec((tk, tn), lambda i, j, kk: (kk, j)))
    res = pl.pallas_call(
        _down_proj_kernel, out_shape=out_shape,
        grid=(rows // TM, d // tn, k_steps),
        in_specs=[pl.BlockSpec((TM, tk), lambda i, j, kk: (i, kk)),
                  w_spec, xo_spec,
                  pl.BlockSpec((bs, 1, tn), lambda i, j, kk: (i // nt, 0, gt_idx * (d // tn) + j))],
        out_specs=out_specs,
        scratch_shapes=[pltpu.VMEM((TM, tn), F32)],
        compiler_params=_cparams(("parallel", "parallel", "arbitrary")),
        name="down_proj_cast" if emit else "down_proj")(a, w, x, mod3)
    return (res[0], res[1]) if emit else (res[0], None)


def _mix_ac_kernel(u_ref, v_ref, b_ref, c_ref, hc_ref, wt_ref, bias_ref, wc_ref, buf_ref,
                   ya_ref, yc_ref, bufnew_ref, *rest, seq_len):
    zbuf = rest[-1]
    vn_ref = rest[0] if len(rest) == 2 else None
    bs, tt, w = u_ref.shape
    rows = bs * tt
    t = pl.program_id(1)
    pad = 8

    @pl.when(t == 0)
    def _():
        zbuf[:, pad - 2:pad, :] = buf_ref[...]

    zc = c_ref[...] * hc_ref[...]
    zbuf[:, pad:pad + tt, :] = zc
    y = zbuf[:, pad - 2:pad - 2 + tt, :] * wc_ref[0:1, :]
    y = y + zbuf[:, pad - 1:pad - 1 + tt, :] * wc_ref[1:2, :]
    y = y + zc * wc_ref[2:3, :]
    yc_ref[...] = (b_ref[...] * y).reshape(rows, w).astype(BF16)
    last2 = zbuf[:, pad + tt - 2:pad + tt, :]
    zbuf[:, pad - 2:pad, :] = last2

    @pl.when(t == pl.num_programs(1) - 1)
    def _():
        bufnew_ref[...] = last2

    gu = _gelu_tanh(u_ref[...].reshape(rows, w))
    gv = _gelu_tanh(v_ref[...].reshape(rows, w))
    ri = lax.broadcasted_iota(jnp.int32, (rows, rows), 0)
    ci = lax.broadcasted_iota(jnp.int32, (rows, rows), 1)
    keep = ci <= ri
    if seq_len < rows:
        keep = keep & ((ri // seq_len) == (ci // seq_len))
    for g in range(G_A):
        sl = slice(g * DH_A, (g + 1) * DH_A)
        vg = gv[:, sl]
        dv = vg - jnp.mean(vg, axis=-1, keepdims=True)
        vn = dv * lax.rsqrt(jnp.mean(dv * dv, axis=-1, keepdims=True) + EPS)
        wm = jnp.where(keep, wt_ref[g], 0.0).astype(BF16)
        mixed = jnp.dot(wm, vn.astype(BF16), preferred_element_type=F32) + bias_ref[:, sl]
        ya_ref[:, sl] = (gu[:, sl] * mixed).astype(BF16)
        if vn_ref is not None:
            vn_ref[:, :, sl] = vn.reshape(bs, tt, DH_A)


def _mix_ac(z_head, z_tail, wt, bias_full, w_conv, layer, buf, emit_v):
    n_seq, t, _ = z_head.shape
    bs, tt = _row_tiling(t, ROW_TILE)
    nt = t // tt
    rows = n_seq * t
    zspec = lambda c: pl.BlockSpec((bs, tt, W_A), lambda i, j: (i, j, c))
    y_spec = pl.BlockSpec((ROW_TILE, W_A), lambda i, j: (i * nt + j, 0))
    out_shape = [jax.ShapeDtypeStruct((rows, W_A), BF16),
                 jax.ShapeDtypeStruct((rows, W_C), BF16),
                 jax.ShapeDtypeStruct((n_seq, CONV_W - 1, W_C), F32)]
    out_specs = [y_spec, y_spec, pl.BlockSpec((bs, CONV_W - 1, W_C), lambda i, j: (i, 0, 0))]
    if emit_v:
        out_shape.append(jax.ShapeDtypeStruct((n_seq, t, W_A), F32))
        out_specs.append(pl.BlockSpec((bs, tt, W_A), lambda i, j: (i, j, 0)))
    return pl.pallas_call(
        functools.partial(_mix_ac_kernel, seq_len=min(t, CHUNK)),
        out_shape=out_shape,
        grid=(n_seq // bs, nt),
        in_specs=[zspec(0), zspec(1), zspec(0), zspec(1), zspec(2),
                  pl.BlockSpec((G_A, ROW_TILE, ROW_TILE), lambda i, j: (0, 0, 0)),
                  pl.BlockSpec((ROW_TILE, W_A), lambda i, j: (0, 0)),
                  pl.BlockSpec((None, CONV_W, W_C), lambda i, j: (layer, 0, 0)),
                  pl.BlockSpec((None, bs, CONV_W - 1, W_C), lambda i, j: (layer, i, 0, 0))],
        out_specs=out_specs,
        scratch_shapes=[pltpu.VMEM((bs, tt + 8, W_C), F32)],
        compiler_params=_cparams(("parallel", "arbitrary")),
        name="mix_ac")(z_head, z_head, z_tail, z_tail, z_tail, wt, bias_full, w_conv, buf)


def _gla_tables(rows, seq_len):
    i = np.arange(rows)[:, None]
    t = np.arange(rows)[None, :]
    same = (i // seq_len) == (t // seq_len)
    slabs = [same & (t <= i), same & (t > i)]
    masks = [i == t]
    s = seq_len // 2
    while s >= 1:
        second = (i // s) % 2 == 1
        start2 = (i // s) * s
        end1 = start2 + s - 1
        slabs.append(np.where(second, (t >= start2) & (t <= i), (t > i) & (t <= end1)))
        masks.append(((i // (2 * s)) == (t // (2 * s))) & second & ((t // s) % 2 == 0))
        s //= 2
    return (np.concatenate(slabs, 0).astype(np.float32), np.stack(masks).astype(np.float32))


def _gla_kernel(q_ref, k_ref, v_ref, r_ref, la_ref, mall_ref, mask_ref, g_ref, s0_ref, *rest):
    y_ref, snew_ref, s_scr = rest[-3:]
    bs, tt, _ = q_ref.shape
    hp = s0_ref.shape[1]
    rows = bs * tt
    n_lvl = mask_ref.shape[0] - 1
    t = pl.program_id(2)

    @pl.when(t == 0)
    def _():
        s_scr[...] = s0_ref[...]

    ri = lax.broadcasted_iota(jnp.int32, (DK_B, DK_B), 0)
    ci = lax.broadcasted_iota(jnp.int32, (DK_B, DK_B), 1)
    eye = ri == ci
    nt_dims = (((1,), (1,)), ((), ()))
    tn_dims = (((0,), (0,)), ((), ()))

    for hh in range(hp):
        ks = slice(hh * DK_B, (hh + 1) * DK_B)
        vs = slice(hh * DV_B, (hh + 1) * DV_B)
        q = q_ref[:, :, ks].reshape(rows, DK_B) * (DK_B ** -0.5)
        k = k_ref[:, :, ks].reshape(rows, DK_B)
        v = v_ref[:, :, vs].reshape(rows, DV_B)
        vb = v.astype(BF16)
        la = la_ref[:, :, ks].reshape(rows, DK_B)
        la_hi = la.astype(BF16)
        la_lo = (la - la_hi.astype(F32)).astype(BF16)
        x = jnp.dot(mall_ref[...], jnp.concatenate([la_hi, la_lo], axis=1), preferred_element_type=F32)
        e = jnp.exp(x[:, :DK_B] + x[:, DK_B:])

        a = mask_ref[0] * lax.dot_general(q.astype(BF16), k.astype(BF16), nt_dims,
                                          preferred_element_type=F32)
        for lvl in range(n_lvl):
            el = e[(2 + lvl) * rows:(3 + lvl) * rows]
            p = lax.dot_general((q * el).astype(BF16), (k * el).astype(BF16), nt_dims,
                                preferred_element_type=F32)
            a = a + mask_ref[lvl + 1] * p
        o_intra = jnp.dot(a.astype(BF16), vb, preferred_element_type=F32)
        e_g = e[0:rows]
        qg = q * e_g
        kd = k * e[rows:2 * rows]

        o_parts = []
        for b in range(bs):
            rs = slice(b * tt, (b + 1) * tt)
            s = s_scr[b, hh]
            o_parts.append(o_intra[rs] + jnp.dot(qg[rs].astype(BF16), s.astype(BF16),
                                                 preferred_element_type=F32))
            e_last = e_g[(b + 1) * tt - 1:(b + 1) * tt]
            e_col = jnp.sum(jnp.where(eye, jnp.broadcast_to(e_last, (DK_B, DK_B)), 0.0),
                            axis=1, keepdims=True)
            kv = lax.dot_general(kd[rs].astype(BF16), v[rs].astype(BF16), tn_dims,
                                 preferred_element_type=F32)
            s_scr[b, hh] = e_col * s + kv
        o = o_parts[0] if bs == 1 else jnp.concatenate(o_parts, axis=0)
        yn = o * lax.rsqrt(jnp.mean(o * o, axis=-1, keepdims=True) + EPS) * g_ref[...]
        y_ref[:, vs] = (yn * _silu(r_ref[:, :, vs].reshape(rows, DV_B))).astype(BF16)

    @pl.when(t == pl.num_programs(2) - 1)
    def _():
        snew_ref[...] = s_scr[...]


def _gla(z_head, la, g_gla, s0, layer, s_out_prev):
    n_seq, t, _ = z_head.shape
    bs, tt = _row_tiling(t, ROW_TILE)
    hp = 8 if bs == 1 else 1
    nt = t // tt
    rows = n_seq * t
    mall, masks = _gla_tables(ROW_TILE, tt)
    mall = jnp.asarray(mall, BF16)
    masks = jnp.asarray(masks, F32)
    zs = lambda w, c0: pl.BlockSpec((bs, tt, hp * w), lambda i, h, j: (i, j, c0 // hp + h))
    s_spec = pl.BlockSpec((None, bs, hp, DK_B, DV_B), lambda i, h, j: (layer, i, h, 0, 0))
    in_specs = [zs(DK_B, ZQ0), zs(DK_B, ZK0), zs(DV_B, ZV0), zs(DV_B, ZR0),
                pl.BlockSpec((bs, tt, hp * DK_B), lambda i, h, j: (i, j, h)),
                pl.BlockSpec(mall.shape, lambda i, h, j: (0, 0)),
                pl.BlockSpec(masks.shape, lambda i, h, j: (0, 0, 0)),
                pl.BlockSpec((None, 1, DV_B), lambda i, h, j: (layer, 0, 0)),
                s_spec]
    args = [z_head, z_head, z_head, z_head, la, mall, masks, g_gla, s0]
    aliases = {}
    if s_out_prev is not None:
        in_specs.append(pl.BlockSpec(memory_space=pl.ANY))
        aliases = {len(args): 1}
        args.append(s_out_prev)
    return pl.pallas_call(
        _gla_kernel,
        out_shape=(jax.ShapeDtypeStruct((rows, W_B), BF16),
                   jax.ShapeDtypeStruct((DEPTH, n_seq, H_B, DK_B, DV_B), F32)),
        grid=(n_seq // bs, H_B // hp, nt),
        in_specs=in_specs,
        out_specs=(pl.BlockSpec((ROW_TILE, hp * DV_B), lambda i, h, j: (i * nt + j, h)), s_spec),
        scratch_shapes=[pltpu.VMEM((bs, hp, DK_B, DV_B), F32)],
        input_output_aliases=aliases,
        compiler_params=_cparams(("parallel", "parallel", "arbitrary")),
        name="gla")(*args)


def _layer(x, layer, mod3, s_gla, buf_conv, p, wts, s_out_prev, emit_v):
    n_seq, t, d = x.shape
    seq_len = min(t, CHUNK)
    reps = ROW_TILE // seq_len
    h, la = _norm_call(x, p["g_mix"], mod3, 1, 0, gate_w=(p["w_a"], p["w_a2"], p["b_a2"]))
    z_head, wb_head = _in_proj(h, wts["w_head"], layer, A_OFF, n_seq, t)
    z_tail, wb_tail = _in_proj(h, wts["w_tail"], layer, P_TAIL, n_seq, t)
    wt = jnp.tile(p["w_s"][:, :seq_len, :seq_len], (1, reps, reps))
    bias_full = jnp.repeat(jnp.tile(p["b_s"][:, :seq_len].T, (reps, 1)), DH_A, axis=1)
    mix = _mix_ac(z_head, z_tail, wt, bias_full, p["w_conv"], layer, buf_conv, emit_v)
    ya, yc, buf_new = mix[:3]
    vn = mix[3] if emit_v else None
    yb, s_out = _gla(z_head, la, p["g_gla"], s_gla, layer, s_out_prev)
    x, wb_out = _out_proj(ya, yb, yc, wts["w_out"], layer, x, mod3, 2)
    h2 = _norm_call(x, p["g_ffn"], mod3, 4, 3)
    a, wb_gate, wb_up = _swiglu(h2, wts["w_gate"], wts["w_up"], layer)
    x, wb_down = _down_proj(a, wts["w_down"], layer, x, mod3, 5)
    wb = dict(w_head=wb_head, w_tail=wb_tail, w_out=wb_out, w_gate=wb_gate, w_up=wb_up, w_down=wb_down)
    return x, s_out, buf_new, vn, wb


def kernel(x_prompt, x_sample, state_gla, state_conv, c_prompt, c_sample, g_mix, g_ffn, w_mod, b_mod,
           w_in, w_s, b_s, w_a2, b_a2, g_gla, w_conv, w_out, w_gate, w_up, w_down, g_final):
    bp = x_prompt.shape[0]
    bd = x_sample.shape[0]
    n_c = bp + bd
    mc = -(-n_c // 16) * 16
    c_all = jnp.concatenate([c_prompt, c_sample, jnp.zeros((mc - n_c, D_MODEL), F32)], axis=0)
    mod = _modulation(c_all, w_mod, b_mod)
    gf = g_final.reshape(1, D_MODEL)
    g_gla3 = g_gla.reshape(DEPTH, 1, DV_B)
    gla0 = jnp.zeros((DEPTH, bp, H_B, DK_B, DV_B), F32)
    conv0 = jnp.zeros((DEPTH, bp, CONV_W - 1, W_C), F32)

    xs, xp = x_sample, x_prompt
    gla_s = gla_p = None
    conv_s, conv_p, v_s = [], [], []
    for l in range(DEPTH):
        wi = w_in[l]
        p = dict(
            g_mix=g_mix[l].reshape(1, D_MODEL), g_ffn=g_ffn[l].reshape(1, D_MODEL),
            w_a=jnp.pad(wi[:, A_OFF:A_OFF + GATE_RANK], ((0, 0), (0, LANES - GATE_RANK))).astype(BF16),
            w_a2=jnp.pad(w_a2[l], ((0, LANES - GATE_RANK), (0, 0))).astype(BF16),
            b_a2=b_a2[l].reshape(1, KW_B),
            w_s=w_s[l], b_s=b_s[l], g_gla=g_gla3, w_conv=w_conv)
        w_f32 = dict(w_head=w_in, w_tail=wi[:, A_OFF + GATE_RANK:], w_out=w_out,
                     w_gate=w_gate, w_up=w_up, w_down=w_down)
        mod_s = mod[l, bp:n_c].reshape(bd, 1, N_MOD * D_MODEL)
        mod_p = mod[l, :bp].reshape(bp, 1, N_MOD * D_MODEL)
        xs, gla_s, buf_s, vn_s, w_bf16 = _layer(xs, l, mod_s, state_gla, state_conv, p, w_f32, gla_s, True)
        xp, gla_p, buf_p, _, _ = _layer(xp, l, mod_p, gla0, conv0, p, w_bf16, gla_p, False)
        conv_s.append(buf_s)
        conv_p.append(buf_p)
        v_s.append(vn_s)
    y_s = _final_norm(xs, gf)
    y_p = _final_norm(xp, gf)
    return (y_p, y_s, gla_p, jnp.stack(conv_p), gla_s, jnp.stack(conv_s), jnp.stack(v_s))
```

```python
import functools

import numpy as np
import jax
import jax.numpy as jnp
from jax import lax
from jax.experimental import pallas as pl
from jax.experimental.pallas import tpu as pltpu

F32 = jnp.float32
BF16 = jnp.bfloat16

D_MODEL = 4096
DEPTH = 2
EPS = 1e-6
CHUNK = 128
W_A = D_MODEL // 4
G_A = 8
DH_A = W_A // G_A
W_B = D_MODEL // 2
H_B = 8
DV_B = W_B // H_B
DK_B = DV_B // 2
KW_B = H_B * DK_B
GATE_RANK = 16
GATE_TAU = 16.0
W_C = D_MODEL // 4
CONV_W = 3
D_FF = -(-8 * D_MODEL // (3 * 256)) * 256
N_MOD = 6
A_OFF = 2 * W_A + 2 * KW_B + 2 * W_B
P_TAIL = 3 * W_C

LANES = 128
ROW_TILE = 128
MIN_REF_LEVEL = 4
TM = 1024
VMEM_LIMIT = 56 * 1024 * 1024

ZQ0 = (2 * W_A) // DK_B
ZK0 = (2 * W_A + KW_B) // DK_B
ZV0 = (2 * W_A + 2 * KW_B) // DV_B
ZR0 = (2 * W_A + 2 * KW_B + W_B) // DV_B


def _cparams(sem):
    return pltpu.CompilerParams(dimension_semantics=sem, vmem_limit_bytes=VMEM_LIMIT)


def _silu(x):
    return x / (1.0 + jnp.exp(-x))


def _gelu_tanh(x):
    c = np.float32(np.sqrt(2.0 / np.pi))
    return 0.5 * x * (1.0 + jnp.tanh(c * (x + 0.044715 * (x * x * x))))


def _row_tiling(t, rows):
    if t >= rows:
        return 1, rows
    return rows // t, t


def _w_spec(w, layer, k, tn, col_map):
    if w.ndim == 2:
        return pl.BlockSpec((k, tn), lambda *g: (0, col_map(*g)))
    return pl.BlockSpec((None, k, tn), lambda *g: (layer, 0, col_map(*g)))


def _mod_kernel(c_ref, w_ref, b_ref, o_ref):
    s = _silu(c_ref[...]).astype(BF16)
    o_ref[0] = jnp.dot(s, w_ref[0].astype(BF16), preferred_element_type=F32) + b_ref[0]


def _modulation(c_all, w_mod, b_mod):
    mc = c_all.shape[0]
    n = N_MOD * D_MODEL
    tn = 512
    return pl.pallas_call(
        _mod_kernel,
        out_shape=jax.ShapeDtypeStruct((DEPTH, mc, n), F32),
        grid=(DEPTH, n // tn),
        in_specs=[
            pl.BlockSpec((mc, D_MODEL), lambda l, j: (0, 0)),
            pl.BlockSpec((1, D_MODEL, tn), lambda l, j: (l, 0, j)),
            pl.BlockSpec((1, 1, tn), lambda l, j: (l, 0, j)),
        ],
        out_specs=pl.BlockSpec((1, mc, tn), lambda l, j: (l, 0, j)),
        compiler_params=_cparams(("parallel", "parallel")),
        name="modulation",
    )(c_all, w_mod, b_mod.reshape(DEPTH, 1, n))


def _norm_mod(x, g, sc, sh):
    y = x * lax.rsqrt(jnp.mean(x * x, axis=-1, keepdims=True) + EPS) * g
    return y * (1.0 + sc) + sh


def _norm_gate_kernel(x_ref, g_ref, sc_ref, sh_ref, wa_ref, wa2_ref, ba2_ref, h_ref, la_ref):
    bs, tt, d = x_ref.shape
    h = _norm_mod(x_ref[...], g_ref[...], sc_ref[...], sh_ref[...]).reshape(bs * tt, d).astype(BF16)
    h_ref[...] = h
    a = jnp.dot(h, wa_ref[...], preferred_element_type=F32)
    pre = jnp.dot(a.astype(BF16), wa2_ref[...], preferred_element_type=F32) + ba2_ref[...]
    log_sig = jnp.minimum(pre, 0.0) - jnp.log(1.0 + jnp.exp(-jnp.abs(pre)))
    la_ref[...] = (log_sig / GATE_TAU).reshape(bs, tt, KW_B)


def _norm_kernel(x_ref, g_ref, sc_ref, sh_ref, h_ref):
    bs, tt, d = x_ref.shape
    h_ref[...] = _norm_mod(x_ref[...], g_ref[...], sc_ref[...], sh_ref[...]).reshape(bs * tt, d).astype(BF16)


def _norm_call(x, g, mod3, sc_idx, sh_idx, gate_w=None):
    n_seq, t, d = x.shape
    bs, tt = _row_tiling(t, 256)
    rows = bs * tt
    nt = t // tt
    grid = (n_seq // bs, nt)
    x_spec = pl.BlockSpec((bs, tt, d), lambda i, j: (i, j, 0))
    g_spec = pl.BlockSpec((1, d), lambda i, j: (0, 0))
    mod_spec = lambda c: pl.BlockSpec((bs, 1, d), lambda i, j: (i, 0, c))
    h_spec = pl.BlockSpec((rows, d), lambda i, j: (i * nt + j, 0))
    h_shape = jax.ShapeDtypeStruct((n_seq * t, d), BF16)
    if gate_w is None:
        return pl.pallas_call(
            _norm_kernel, out_shape=h_shape, grid=grid,
            in_specs=[x_spec, g_spec, mod_spec(sc_idx), mod_spec(sh_idx)],
            out_specs=h_spec, compiler_params=_cparams(("parallel", "parallel")),
            name="norm_mod")(x, g, mod3, mod3)
    wa, wa2, ba2 = gate_w
    return pl.pallas_call(
        _norm_gate_kernel,
        out_shape=(h_shape, jax.ShapeDtypeStruct((n_seq, t, KW_B), F32)),
        grid=grid,
        in_specs=[x_spec, g_spec, mod_spec(sc_idx), mod_spec(sh_idx),
                  pl.BlockSpec((d, LANES), lambda i, j: (0, 0)),
                  pl.BlockSpec((LANES, KW_B), lambda i, j: (0, 0)),
                  pl.BlockSpec((1, KW_B), lambda i, j: (0, 0))],
        out_specs=(h_spec, pl.BlockSpec((bs, tt, KW_B), lambda i, j: (i, j, 0))),
        compiler_params=_cparams(("parallel", "parallel")),
        name="norm_mod_gate")(x, g, mod3, mod3, wa, wa2, ba2)


def _final_norm_kernel(x_ref, g_ref, o_ref):
    x = x_ref[...]
    o_ref[...] = x * lax.rsqrt(jnp.mean(x * x, axis=-1, keepdims=True) + EPS) * g_ref[...]


def _final_norm(x, g):
    n_seq, t, d = x.shape
    bs, tt = _row_tiling(t, 256)
    return pl.pallas_call(
        _final_norm_kernel, out_shape=jax.ShapeDtypeStruct(x.shape, F32),
        grid=(n_seq // bs, t // tt),
        in_specs=[pl.BlockSpec((bs, tt, d), lambda i, j: (i, j, 0)),
                  pl.BlockSpec((1, d), lambda i, j: (0, 0))],
        out_specs=pl.BlockSpec((bs, tt, d), lambda i, j: (i, j, 0)),
        compiler_params=_cparams(("parallel", "parallel")),
        name="final_norm")(x, g)


def _bf16_tile(w_ref, wb_ref):
    if wb_ref is None:
        return w_ref[...]
    wb = w_ref[...].astype(BF16)
    wb_ref[...] = wb
    return wb


def _in_proj_kernel(h_ref, wt_ref, z_ref, wtb_ref=None):
    acc = lax.dot_general(h_ref[...], _bf16_tile(wt_ref, wtb_ref), (((1,), (1,)), ((), ())),
                          preferred_element_type=F32)
    z_ref[...] = acc.reshape(z_ref.shape)


def _in_proj(h, wt, layer, n_cols, n_seq, t):
    rows, k = h.shape
    emit = wt.dtype != BF16
    bs, tt = _row_tiling(t, TM)
    nt = t // tt
    tn = 512 if emit else 1024
    if wt.ndim == 2:
        w_spec = pl.BlockSpec((tn, k), lambda i, j: (j, 0))
    else:
        w_spec = pl.BlockSpec((None, tn, k), lambda i, j: (layer, j, 0))
    out_shape = [jax.ShapeDtypeStruct((n_seq, t, n_cols), F32)]
    out_specs = [pl.BlockSpec((bs, tt, tn), lambda i, j: (i // nt, i % nt, j))]
    if emit:
        assert rows == TM
        out_shape.append(jax.ShapeDtypeStruct((n_cols, k), BF16))
        out_specs.append(pl.BlockSpec((tn, k), lambda i, j: (j, 0)))
    res = pl.pallas_call(
        _in_proj_kernel, out_shape=out_shape,
        grid=(rows // TM, n_cols // tn),
        in_specs=[pl.BlockSpec((TM, k), lambda i, j: (i, 0)), w_spec],
        out_specs=out_specs,
        compiler_params=_cparams(("parallel", "parallel")),
        name="in_proj_cast" if emit else "in_proj")(h, wt)
    return (res[0], res[1]) if emit else (res[0], None)


def _out_proj_kernel(ya_ref, yb_ref, yc_ref, w_ref, x_ref, gt_ref, o_ref, wb_ref=None):
    if wb_ref is not None:
        wb_ref[...] = w_ref[...].astype(BF16)
        w_ref = wb_ref
    acc = jnp.dot(ya_ref[...], w_ref[0:W_A, :], preferred_element_type=F32)
    acc += jnp.dot(yb_ref[...], w_ref[W_A:W_A + W_B, :], preferred_element_type=F32)
    acc += jnp.dot(yc_ref[...], w_ref[W_A + W_B:, :], preferred_element_type=F32)
    o_ref[...] = x_ref[...] + gt_ref[...] * acc.reshape(o_ref.shape)


def _out_proj(ya, yb, yc, w, layer, x, mod3, gt_idx):
    n_seq, t, d = x.shape
    rows = n_seq * t
    emit = w.dtype != BF16
    bs, tt = _row_tiling(t, TM)
    nt = t // tt
    tn = 512
    xo_spec = pl.BlockSpec((bs, tt, tn), lambda i, j: (i // nt, i % nt, j))
    out_shape = [jax.ShapeDtypeStruct(x.shape, F32)]
    out_specs = [xo_spec]
    if emit:
        assert rows == TM
        out_shape.append(jax.ShapeDtypeStruct((d, d), BF16))
        out_specs.append(pl.BlockSpec((d, tn), lambda i, j: (0, j)))
    res = pl.pallas_call(
        _out_proj_kernel, out_shape=out_shape,
        grid=(rows // TM, d // tn),
        in_specs=[pl.BlockSpec((TM, W_A), lambda i, j: (i, 0)),
                  pl.BlockSpec((TM, W_B), lambda i, j: (i, 0)),
                  pl.BlockSpec((TM, W_C), lambda i, j: (i, 0)),
                  _w_spec(w, layer, d, tn, lambda i, j: j),
                  xo_spec,
                  pl.BlockSpec((bs, 1, tn), lambda i, j: (i // nt, 0, gt_idx * (d // tn) + j))],
        out_specs=out_specs,
        compiler_params=_cparams(("parallel", "parallel")),
        name="out_proj_cast" if emit else "out_proj")(ya, yb, yc, w, x, mod3)
    return (res[0], res[1]) if emit else (res[0], None)


def _swiglu_kernel(h_ref, wg_ref, wu_ref, a_ref, wgb_ref=None, wub_ref=None):
    h = h_ref[...]
    g = jnp.dot(h, _bf16_tile(wg_ref, wgb_ref), preferred_element_type=F32)
    u = jnp.dot(h, _bf16_tile(wu_ref, wub_ref), preferred_element_type=F32)
    a_ref[...] = (_silu(g) * u).astype(BF16)


def _swiglu(h, wg, wu, layer):
    rows, k = h.shape
    n = wg.shape[-1]
    emit = wg.dtype != BF16
    tn = 256
    tm = TM if emit or rows % (2 * TM) else 2 * TM
    out_shape = [jax.ShapeDtypeStruct((rows, n), BF16)]
    out_specs = [pl.BlockSpec((tm, tn), lambda i, j: (i, j))]
    if emit:
        assert rows == TM
        out_shape += [jax.ShapeDtypeStruct((k, n), BF16)] * 2
        out_specs += [pl.BlockSpec((k, tn), lambda i, j: (0, j))] * 2
    res = pl.pallas_call(
        _swiglu_kernel, out_shape=out_shape,
        grid=(rows // tm, n // tn),
        in_specs=[pl.BlockSpec((tm, k), lambda i, j: (i, 0)),
                  _w_spec(wg, layer, k, tn, lambda i, j: j),
                  _w_spec(wu, layer, k, tn, lambda i, j: j)],
        out_specs=out_specs,
        compiler_params=_cparams(("parallel", "parallel")),
        name="swiglu_cast" if emit else "swiglu")(h, wg, wu)
    return (res[0], res[1], res[2]) if emit else (res[0], None, None)


def _down_proj_kernel(a_ref, w_ref, x_ref, gt_ref, o_ref, *rest, k_outer):
    acc_ref = rest[-1]
    wb_ref = rest[0] if len(rest) == 2 else None
    kk = pl.program_id(1 if k_outer else 2)
    slot = pl.program_id(2) if k_outer else 0
    part = jnp.dot(a_ref[...], _bf16_tile(w_ref, wb_ref), preferred_element_type=F32)

    @pl.when(kk == 0)
    def _():
        acc_ref[slot] = part

    @pl.when(kk == 1)
    def _():
        o_ref[...] = x_ref[...] + gt_ref[...] * (acc_ref[slot] + part).reshape(o_ref.shape)


def _down_proj(a, w, layer, x, mod3, gt_idx):
    n_seq, t, d = x.shape
    rows, k = a.shape
    emit = w.dtype != BF16
    bs, tt = _row_tiling(t, TM)
    nt = t // tt
    tn = 256 if emit else 512
    n_j = d // tn
    tk = k // 2
    if emit:
        assert rows == TM
        grid = (1, 2, n_j)
        ids = lambda i, kk, j: (i, kk, j, j * kk)
        sem = ("parallel", "arbitrary", "arbitrary")
    else:
        grid = (rows // TM, n_j, 2)
        ids = lambda i, j, kk: (i, kk, j, j)
        sem = ("parallel", "parallel", "arbitrary")

    def spec(shape, fn, **kw):
        return pl.BlockSpec(shape, lambda *g: fn(*ids(*g)), **kw)

    xo_spec = spec((bs, tt, tn), lambda i, kk, j, jo: (i // nt, i % nt, jo))
    if w.ndim == 2:
        w_spec = spec((tk, tn), lambda i, kk, j, jo: (kk, j))
    else:
        w_spec = spec((None, tk, tn), lambda i, kk, j, jo: (layer, kk, j))
    a_kw = dict(pipeline_mode=pl.Buffered(1)) if emit else {}
    out_shape = [jax.ShapeDtypeStruct(x.shape, F32)]
    out_specs = [xo_spec]
    if emit:
        out_shape.append(jax.ShapeDtypeStruct((k, d), BF16))
        out_specs.append(spec((tk, tn), lambda i, kk, j, jo: (kk, j)))
    res = pl.pallas_call(
        functools.partial(_down_proj_kernel, k_outer=emit), out_shape=out_shape,
        grid=grid,
        in_specs=[spec((TM, tk), lambda i, kk, j, jo: (i, kk), **a_kw),
                  w_spec, xo_spec,
                  spec((bs, 1, tn), lambda i, kk, j, jo: (i // nt, 0, gt_idx * n_j + jo))],
        out_specs=out_specs,
        scratch_shapes=[pltpu.VMEM((n_j if emit else 1, TM, tn), F32)],
        compiler_params=_cparams(sem),
        name="down_proj_cast" if emit else "down_proj")(a, w, x, mod3)
    return (res[0], res[1]) if emit else (res[0], None)


def _mix_ac_kernel(u_ref, v_ref, b_ref, c_ref, hc_ref, wt_ref, bias_ref, wc_ref, buf_ref,
                   ya_ref, yc_ref, bufnew_ref, *rest, seq_len):
    zbuf = rest[-1]
    vn_ref = rest[0] if len(rest) == 2 else None
    bs, tt, w = u_ref.shape
    rows = bs * tt
    t = pl.program_id(1)
    pad = 8

    @pl.when(t == 0)
    def _():
        zbuf[:, pad - 2:pad, :] = buf_ref[...]

    zc = c_ref[...] * hc_ref[...]
    zbuf[:, pad:pad + tt, :] = zc
    y = zbuf[:, pad - 2:pad - 2 + tt, :] * wc_ref[0:1, :]
    y = y + zbuf[:, pad - 1:pad - 1 + tt, :] * wc_ref[1:2, :]
    y = y + zc * wc_ref[2:3, :]
    yc_ref[...] = (b_ref[...] * y).reshape(rows, w).astype(BF16)
    last2 = zbuf[:, pad + tt - 2:pad + tt, :]
    zbuf[:, pad - 2:pad, :] = last2

    @pl.when(t == pl.num_programs(1) - 1)
    def _():
        bufnew_ref[...] = last2

    gu = _gelu_tanh(u_ref[...].reshape(rows, w))
    gv = _gelu_tanh(v_ref[...].reshape(rows, w))
    ri = lax.broadcasted_iota(jnp.int32, (rows, rows), 0)
    ci = lax.broadcasted_iota(jnp.int32, (rows, rows), 1)
    keep = ci <= ri
    if seq_len < rows:
        keep = keep & ((ri // seq_len) == (ci // seq_len))
    for g in range(G_A):
        sl = slice(g * DH_A, (g + 1) * DH_A)
        vg = gv[:, sl]
        dv = vg - jnp.mean(vg, axis=-1, keepdims=True)
        vn = dv * lax.rsqrt(jnp.mean(dv * dv, axis=-1, keepdims=True) + EPS)
        wm = jnp.where(keep, wt_ref[g], 0.0).astype(BF16)
        mixed = jnp.dot(wm, vn.astype(BF16), preferred_element_type=F32) + bias_ref[:, sl]
        ya_ref[:, sl] = (gu[:, sl] * mixed).astype(BF16)
        if vn_ref is not None:
            vn_ref[:, :, sl] = vn.reshape(bs, tt, DH_A)


def _mix_ac(z_head, z_tail, wt, bias_full, w_conv, layer, buf, emit_v):
    n_seq, t, _ = z_head.shape
    bs, tt = _row_tiling(t, ROW_TILE)
    nt = t // tt
    rows = n_seq * t
    zspec = lambda c: pl.BlockSpec((bs, tt, W_A), lambda i, j: (i, j, c))
    y_spec = pl.BlockSpec((ROW_TILE, W_A), lambda i, j: (i * nt + j, 0))
    out_shape = [jax.ShapeDtypeStruct((rows, W_A), BF16),
                 jax.ShapeDtypeStruct((rows, W_C), BF16),
                 jax.ShapeDtypeStruct((n_seq, CONV_W - 1, W_C), F32)]
    out_specs = [y_spec, y_spec, pl.BlockSpec((bs, CONV_W - 1, W_C), lambda i, j: (i, 0, 0))]
    if emit_v:
        out_shape.append(jax.ShapeDtypeStruct((n_seq, t, W_A), F32))
        out_specs.append(pl.BlockSpec((bs, tt, W_A), lambda i, j: (i, j, 0)))
    return pl.pallas_call(
        functools.partial(_mix_ac_kernel, seq_len=min(t, CHUNK)),
        out_shape=out_shape,
        grid=(n_seq // bs, nt),
        in_specs=[zspec(0), zspec(1), zspec(0), zspec(1), zspec(2),
                  pl.BlockSpec((G_A, ROW_TILE, ROW_TILE), lambda i, j: (0, 0, 0)),
                  pl.BlockSpec((ROW_TILE, W_A), lambda i, j: (0, 0)),
                  pl.BlockSpec((None, CONV_W, W_C), lambda i, j: (layer, 0, 0)),
                  pl.BlockSpec((None, bs, CONV_W - 1, W_C), lambda i, j: (layer, i, 0, 0))],
        out_specs=out_specs,
        scratch_shapes=[pltpu.VMEM((bs, tt + 8, W_C), F32)],
        compiler_params=_cparams(("parallel", "arbitrary")),
        name="mix_ac")(z_head, z_head, z_tail, z_tail, z_tail, wt, bias_full, w_conv, buf)


def _gla_tables(rows, seq_len):
    i = np.arange(rows)[:, None]
    t = np.arange(rows)[None, :]
    same = (i // seq_len) == (t // seq_len)
    slabs = [same & (t <= i)]
    masks = [i == t]
    s = seq_len // 2
    while s >= 1:
        second = (i // s) % 2 == 1
        start2 = (i // s) * s
        end1 = start2 + s - 1
        if s < MIN_REF_LEVEL:
            slabs.append(np.where(second, (t >= start2) & (t <= i), (t > i) & (t <= end1)))
        masks.append(((i // (2 * s)) == (t // (2 * s))) & second & ((t // s) % 2 == 0))
        s //= 2
    return (np.concatenate(slabs, 0).astype(np.float32), np.stack(masks).astype(np.float32))


def _block_ref_exponent(g, block, ref_row, flip_from):
    rows, dk = g.shape
    gr = g.reshape(rows // block, block, dk)
    d = gr - gr[:, ref_row:ref_row + 1, :]
    pos = lax.broadcasted_iota(jnp.int32, gr.shape, 1)
    return jnp.where(pos >= flip_from, d, -d).reshape(rows, dk)


def _gla_kernel(q_ref, k_ref, v_ref, r_ref, la_ref, mall_ref, mask_ref, g_ref, s0_ref, *rest):
    y_ref, snew_ref, s_scr = rest[-3:]
    bs, tt, _ = q_ref.shape
    hp = s0_ref.shape[1]
    rows = bs * tt
    n_lvl = mask_ref.shape[0] - 1
    t = pl.program_id(2)

    @pl.when(t == 0)
    def _():
        s_scr[...] = s0_ref[...]

    ri = lax.broadcasted_iota(jnp.int32, (DK_B, DK_B), 0)
    ci = lax.broadcasted_iota(jnp.int32, (DK_B, DK_B), 1)
    eye = ri == ci
    nt_dims = (((1,), (1,)), ((), ()))
    tn_dims = (((0,), (0,)), ((), ()))

    for hh in range(hp):
        ks = slice(hh * DK_B, (hh + 1) * DK_B)
        vs = slice(hh * DV_B, (hh + 1) * DV_B)
        q = q_ref[:, :, ks].reshape(rows, DK_B) * (DK_B ** -0.5)
        k = k_ref[:, :, ks].reshape(rows, DK_B)
        v = v_ref[:, :, vs].reshape(rows, DV_B)
        vb = v.astype(BF16)
        la = la_ref[:, :, ks].reshape(rows, DK_B)
        la_hi = la.astype(BF16)
        la_lo = (la - la_hi.astype(F32)).astype(BF16)
        x = jnp.dot(mall_ref[...], jnp.concatenate([la_hi, la_lo], axis=1), preferred_element_type=F32)
        x = x[:, :DK_B] + x[:, DK_B:]
        g = x[0:rows]

        a = mask_ref[0] * lax.dot_general(q.astype(BF16), k.astype(BF16), nt_dims,
                                          preferred_element_type=F32)
        n_slab = 1
        for lvl in range(n_lvl):
            s = tt >> (lvl + 1)
            if s >= MIN_REF_LEVEL:
                xl = _block_ref_exponent(g, 2 * s, s - 1, s)
            else:
                xl = x[n_slab * rows:(n_slab + 1) * rows]
                n_slab += 1
            el = jnp.exp(xl)
            p = lax.dot_general((q * el).astype(BF16), (k * el).astype(BF16), nt_dims,
                                preferred_element_type=F32)
            a = a + mask_ref[lvl + 1] * p
        o_intra = jnp.dot(a.astype(BF16), vb, preferred_element_type=F32)
        e_g = jnp.exp(g)
        qg = q * e_g
        kd = k * jnp.exp(_block_ref_exponent(g, tt, tt - 1, tt))

        o_parts = []
        for b in range(bs):
            rs = slice(b * tt, (b + 1) * tt)
            s = s_scr[b, hh]
            o_parts.append(o_intra[rs] + jnp.dot(qg[rs].astype(BF16), s.astype(BF16),
                                                 preferred_element_type=F32))
            e_last = e_g[(b + 1) * tt - 1:(b + 1) * tt]
            e_col = jnp.sum(jnp.where(eye, jnp.broadcast_to(e_last, (DK_B, DK_B)), 0.0),
                            axis=1, keepdims=True)
            kv = lax.dot_general(kd[rs].astype(BF16), v[rs].astype(BF16), tn_dims,
                                 preferred_element_type=F32)
            s_scr[b, hh] = e_col * s + kv
        o = o_parts[0] if bs == 1 else jnp.concatenate(o_parts, axis=0)
        yn = o * lax.rsqrt(jnp.mean(o * o, axis=-1, keepdims=True) + EPS) * g_ref[...]
        y_ref[:, vs] = (yn * _silu(r_ref[:, :, vs].reshape(rows, DV_B))).astype(BF16)

    @pl.when(t == pl.num_programs(2) - 1)
    def _():
        snew_ref[...] = s_scr[...]


def _gla(z_head, la, g_gla, s0, layer, s_out_prev):
    n_seq, t, _ = z_head.shape
    bs, tt = _row_tiling(t, ROW_TILE)
    hp = 8 if bs == 1 else 1
    nt = t // tt
    rows = n_seq * t
    mall, masks = _gla_tables(ROW_TILE, tt)
    mall = jnp.asarray(mall, BF16)
    masks = jnp.asarray(masks, F32)
    zs = lambda w, c0: pl.BlockSpec((bs, tt, hp * w), lambda i, h, j: (i, j, c0 // hp + h))
    s_spec = pl.BlockSpec((None, bs, hp, DK_B, DV_B), lambda i, h, j: (layer, i, h, 0, 0))
    in_specs = [zs(DK_B, ZQ0), zs(DK_B, ZK0), zs(DV_B, ZV0), zs(DV_B, ZR0),
                pl.BlockSpec((bs, tt, hp * DK_B), lambda i, h, j: (i, j, h)),
                pl.BlockSpec(mall.shape, lambda i, h, j: (0, 0)),
                pl.BlockSpec(masks.shape, lambda i, h, j: (0, 0, 0)),
                pl.BlockSpec((None, 1, DV_B), lambda i, h, j: (layer, 0, 0)),
                s_spec]
    args = [z_head, z_head, z_head, z_head, la, mall, masks, g_gla, s0]
    aliases = {}
    if s_out_prev is not None:
        in_specs.append(pl.BlockSpec(memory_space=pl.ANY))
        aliases = {len(args): 1}
        args.append(s_out_prev)
    return pl.pallas_call(
        _gla_kernel,
        out_shape=(jax.ShapeDtypeStruct((rows, W_B), BF16),
                   jax.ShapeDtypeStruct((DEPTH, n_seq, H_B, DK_B, DV_B), F32)),
        grid=(n_seq // bs, H_B // hp, nt),
        in_specs=in_specs,
        out_specs=(pl.BlockSpec((ROW_TILE, hp * DV_B), lambda i, h, j: (i * nt + j, h)), s_spec),
        scratch_shapes=[pltpu.VMEM((bs, hp, DK_B, DV_B), F32)],
        input_output_aliases=aliases,
        compiler_params=_cparams(("parallel", "parallel", "arbitrary")),
        name="gla")(*args)


def _layer(x, layer, mod3, s_gla, buf_conv, p, wts, s_out_prev, emit_v):
    n_seq, t, d = x.shape
    seq_len = min(t, CHUNK)
    reps = ROW_TILE // seq_len
    h, la = _norm_call(x, p["g_mix"], mod3, 1, 0, gate_w=(p["w_a"], p["w_a2"], p["b_a2"]))
    z_head, wb_head = _in_proj(h, wts["w_head"], layer, A_OFF, n_seq, t)
    z_tail, wb_tail = _in_proj(h, wts["w_tail"], layer, P_TAIL, n_seq, t)
    wt = jnp.tile(p["w_s"][:, :seq_len, :seq_len], (1, reps, reps))
    bias_full = jnp.repeat(jnp.tile(p["b_s"][:, :seq_len].T, (reps, 1)), DH_A, axis=1)
    mix = _mix_ac(z_head, z_tail, wt, bias_full, p["w_conv"], layer, buf_conv, emit_v)
    ya, yc, buf_new = mix[:3]
    vn = mix[3] if emit_v else None
    yb, s_out = _gla(z_head, la, p["g_gla"], s_gla, layer, s_out_prev)
    x, wb_out = _out_proj(ya, yb, yc, wts["w_out"], layer, x, mod3, 2)
    h2 = _norm_call(x, p["g_ffn"], mod3, 4, 3)
    a, wb_gate, wb_up = _swiglu(h2, wts["w_gate"], wts["w_up"], layer)
    x, wb_down = _down_proj(a, wts["w_down"], layer, x, mod3, 5)
    wb = dict(w_head=wb_head, w_tail=wb_tail, w_out=wb_out, w_gate=wb_gate, w_up=wb_up, w_down=wb_down)
    return x, s_out, buf_new, vn, wb


def kernel(x_prompt, x_sample, state_gla, state_conv, c_prompt, c_sample, g_mix, g_ffn, w_mod, b_mod,
           w_in, w_s, b_s, w_a2, b_a2, g_gla, w_conv, w_out, w_gate, w_up, w_down, g_final):
    bp = x_prompt.shape[0]
    bd = x_sample.shape[0]
    n_c = bp + bd
    mc = -(-n_c // 16) * 16
    c_all = jnp.concatenate([c_prompt, c_sample, jnp.zeros((mc - n_c, D_MODEL), F32)], axis=0)
    mod = _modulation(c_all, w_mod, b_mod)
    gf = g_final.reshape(1, D_MODEL)
    g_gla3 = g_gla.reshape(DEPTH, 1, DV_B)
    gla0 = jnp.zeros((DEPTH, bp, H_B, DK_B, DV_B), F32)
    conv0 = jnp.zeros((DEPTH, bp, CONV_W - 1, W_C), F32)

    xs, xp = x_sample, x_prompt
    gla_s = gla_p = None
    conv_s, conv_p, v_s = [], [], []
    wt_in = jnp.swapaxes(w_in, 1, 2)
    for l in range(DEPTH):
        w_a = wt_in[l, A_OFF:A_OFF + GATE_RANK, :].T
        p = dict(
            g_mix=g_mix[l].reshape(1, D_MODEL), g_ffn=g_ffn[l].reshape(1, D_MODEL),
            w_a=jnp.pad(w_a, ((0, 0), (0, LANES - GATE_RANK))).astype(BF16),
            w_a2=jnp.pad(w_a2[l], ((0, LANES - GATE_RANK), (0, 0))).astype(BF16),
            b_a2=b_a2[l].reshape(1, KW_B),
            w_s=w_s[l], b_s=b_s[l], g_gla=g_gla3, w_conv=w_conv)
        w_f32 = dict(w_head=wt_in, w_tail=wt_in[l, A_OFF + GATE_RANK:, :], w_out=w_out,
                     w_gate=w_gate, w_up=w_up, w_down=w_down)
        mod_s = mod[l, bp:n_c].reshape(bd, 1, N_MOD * D_MODEL)
        mod_p = mod[l, :bp].reshape(bp, 1, N_MOD * D_MODEL)
        xs, gla_s, buf_s, vn_s, w_bf16 = _layer(xs, l, mod_s, state_gla, state_conv, p, w_f32, gla_s, True)
        xp, gla_p, buf_p, _, _ = _layer(xp, l, mod_p, gla0, conv0, p, w_bf16, gla_p, False)
        conv_s.append(buf_s)
        conv_p.append(buf_p)
        v_s.append(vn_s)
    y_s = _final_norm(xs, gf)
    y_p = _final_norm(xp, gf)
    return (y_p, y_s, gla_p, jnp.stack(conv_p), gla_s, jnp.stack(conv_s), jnp.stack(v_s))
```

```python
import functools

import numpy as np
import jax
import jax.numpy as jnp
from jax import lax
from jax.experimental import pallas as pl
from jax.experimental.pallas import tpu as pltpu

F32 = jnp.float32
BF16 = jnp.bfloat16

D_MODEL = 4096
DEPTH = 2
EPS = 1e-6
CHUNK = 128
W_A = D_MODEL // 4
G_A = 8
DH_A = W_A // G_A
W_B = D_MODEL // 2
H_B = 8
DV_B = W_B // H_B
DK_B = DV_B // 2
KW_B = H_B * DK_B
GATE_RANK = 16
GATE_TAU = 16.0
W_C = D_MODEL // 4
CONV_W = 3
D_FF = -(-8 * D_MODEL // (3 * 256)) * 256
N_MOD = 6
A_OFF = 2 * W_A + 2 * KW_B + 2 * W_B
P_TAIL = 3 * W_C

LOG2_E = float(np.log2(np.e))
LANES = 128
ROW_TILE = 128
MIN_REF_LEVEL = 4
TM = 1024
VMEM_LIMIT = 56 * 1024 * 1024

ZQ0 = (2 * W_A) // DK_B
ZK0 = (2 * W_A + KW_B) // DK_B
ZV0 = (2 * W_A + 2 * KW_B) // DV_B
ZR0 = (2 * W_A + 2 * KW_B + W_B) // DV_B


def _cparams(sem):
    return pltpu.CompilerParams(dimension_semantics=sem, vmem_limit_bytes=VMEM_LIMIT)


def _silu(x):
    return x / (1.0 + jnp.exp(-x))


def _gelu_tanh(x):
    c = np.float32(np.sqrt(2.0 / np.pi))
    return 0.5 * x * (1.0 + jnp.tanh(c * (x + 0.044715 * (x * x * x))))


def _row_tiling(t, rows):
    if t >= rows:
        return 1, rows
    return rows // t, t


def _w_spec(w, layer, k, tn, col_map):
    if w.ndim == 2:
        return pl.BlockSpec((k, tn), lambda *g: (0, col_map(*g)))
    return pl.BlockSpec((None, k, tn), lambda *g: (layer, 0, col_map(*g)))


def _mod_kernel(c_ref, w_ref, b_ref, o_ref):
    s = _silu(c_ref[...]).astype(BF16)
    o_ref[0] = jnp.dot(s, w_ref[0].astype(BF16), preferred_element_type=F32) + b_ref[0]


def _modulation(c_all, w_mod, b_mod):
    mc = c_all.shape[0]
    n = N_MOD * D_MODEL
    tn = 512
    return pl.pallas_call(
        _mod_kernel,
        out_shape=jax.ShapeDtypeStruct((DEPTH, mc, n), F32),
        grid=(DEPTH, n // tn),
        in_specs=[
            pl.BlockSpec((mc, D_MODEL), lambda l, j: (0, 0)),
            pl.BlockSpec((1, D_MODEL, tn), lambda l, j: (l, 0, j)),
            pl.BlockSpec((1, 1, tn), lambda l, j: (l, 0, j)),
        ],
        out_specs=pl.BlockSpec((1, mc, tn), lambda l, j: (l, 0, j)),
        compiler_params=_cparams(("parallel", "parallel")),
        name="modulation",
    )(c_all, w_mod, b_mod.reshape(DEPTH, 1, n))


def _norm_mod(x, g, sc, sh):
    y = x * lax.rsqrt(jnp.mean(x * x, axis=-1, keepdims=True) + EPS) * g
    return y * (1.0 + sc) + sh


def _norm_gate_kernel(x_ref, g_ref, sc_ref, sh_ref, wa_ref, wa2_ref, ba2_ref, h_ref, la_ref):
    bs, tt, d = x_ref.shape
    h = _norm_mod(x_ref[...], g_ref[...], sc_ref[...], sh_ref[...]).reshape(bs * tt, d).astype(BF16)
    h_ref[...] = h
    a = jnp.dot(h, wa_ref[...].astype(BF16), preferred_element_type=F32)
    pre = jnp.dot(a.astype(BF16), wa2_ref[...], preferred_element_type=F32) + ba2_ref[...]
    log_sig = jnp.minimum(pre, 0.0) - jnp.log(1.0 + jnp.exp(-jnp.abs(pre)))
    la_ref[...] = (log_sig / GATE_TAU).reshape(bs, tt, KW_B)


def _norm_kernel(x_ref, g_ref, sc_ref, sh_ref, h_ref):
    bs, tt, d = x_ref.shape
    h_ref[...] = _norm_mod(x_ref[...], g_ref[...], sc_ref[...], sh_ref[...]).reshape(bs * tt, d).astype(BF16)


def _norm_call(x, g, mod3, sc_idx, sh_idx, gate_w=None):
    n_seq, t, d = x.shape
    bs, tt = _row_tiling(t, 256)
    rows = bs * tt
    nt = t // tt
    grid = (n_seq // bs, nt)
    x_spec = pl.BlockSpec((bs, tt, d), lambda i, j: (i, j, 0))
    g_spec = pl.BlockSpec((1, d), lambda i, j: (0, 0))
    mod_spec = lambda c: pl.BlockSpec((bs, 1, d), lambda i, j: (i, 0, c))
    h_spec = pl.BlockSpec((rows, d), lambda i, j: (i * nt + j, 0))
    h_shape = jax.ShapeDtypeStruct((n_seq * t, d), BF16)
    if gate_w is None:
        return pl.pallas_call(
            _norm_kernel, out_shape=h_shape, grid=grid,
            in_specs=[x_spec, g_spec, mod_spec(sc_idx), mod_spec(sh_idx)],
            out_specs=h_spec, compiler_params=_cparams(("parallel", "parallel")),
            name="norm_mod")(x, g, mod3, mod3)
    wa, wa2, ba2 = gate_w
    return pl.pallas_call(
        _norm_gate_kernel,
        out_shape=(h_shape, jax.ShapeDtypeStruct((n_seq, t, KW_B), F32)),
        grid=grid,
        in_specs=[x_spec, g_spec, mod_spec(sc_idx), mod_spec(sh_idx),
                  pl.BlockSpec((d, LANES), lambda i, j: (0, 0)),
                  pl.BlockSpec((LANES, KW_B), lambda i, j: (0, 0)),
                  pl.BlockSpec((1, KW_B), lambda i, j: (0, 0))],
        out_specs=(h_spec, pl.BlockSpec((bs, tt, KW_B), lambda i, j: (i, j, 0))),
        compiler_params=_cparams(("parallel", "parallel")),
        name="norm_mod_gate")(x, g, mod3, mod3, wa, wa2, ba2)


def _final_norm_kernel(x_ref, g_ref, o_ref):
    x = x_ref[...]
    o_ref[...] = x * lax.rsqrt(jnp.mean(x * x, axis=-1, keepdims=True) + EPS) * g_ref[...]


def _final_norm(x, g):
    n_seq, t, d = x.shape
    bs, tt = _row_tiling(t, 256)
    return pl.pallas_call(
        _final_norm_kernel, out_shape=jax.ShapeDtypeStruct(x.shape, F32),
        grid=(n_seq // bs, t // tt),
        in_specs=[pl.BlockSpec((bs, tt, d), lambda i, j: (i, j, 0)),
                  pl.BlockSpec((1, d), lambda i, j: (0, 0))],
        out_specs=pl.BlockSpec((bs, tt, d), lambda i, j: (i, j, 0)),
        compiler_params=_cparams(("parallel", "parallel")),
        name="final_norm")(x, g)


def _bf16_tile(w_ref, wb_ref):
    if wb_ref is None:
        return w_ref[...]
    wb = w_ref[...].astype(BF16)
    wb_ref[...] = wb
    return wb


def _in_proj_kernel(h_ref, wt_ref, z_ref, wtb_ref=None):
    acc = lax.dot_general(h_ref[...], _bf16_tile(wt_ref, wtb_ref), (((1,), (1,)), ((), ())),
                          preferred_element_type=F32)
    z_ref[...] = acc.reshape(z_ref.shape)


def _in_proj(h, wt, layer, n_cols, n_seq, t):
    rows, k = h.shape
    emit = wt.dtype != BF16
    bs, tt = _row_tiling(t, TM)
    nt = t // tt
    tn = 512 if emit else 1024
    if wt.ndim == 2:
        w_spec = pl.BlockSpec((tn, k), lambda i, j: (j, 0))
    else:
        w_spec = pl.BlockSpec((None, tn, k), lambda i, j: (layer, j, 0))
    out_shape = [jax.ShapeDtypeStruct((n_seq, t, n_cols), F32)]
    out_specs = [pl.BlockSpec((bs, tt, tn), lambda i, j: (i // nt, i % nt, j))]
    if emit:
        assert rows == TM
        out_shape.append(jax.ShapeDtypeStruct((n_cols, k), BF16))
        out_specs.append(pl.BlockSpec((tn, k), lambda i, j: (j, 0)))
    res = pl.pallas_call(
        _in_proj_kernel, out_shape=out_shape,
        grid=(rows // TM, n_cols // tn),
        in_specs=[pl.BlockSpec((TM, k), lambda i, j: (i, 0)), w_spec],
        out_specs=out_specs,
        compiler_params=_cparams(("parallel", "parallel")),
        name="in_proj_cast" if emit else "in_proj")(h, wt)
    return (res[0], res[1]) if emit else (res[0], None)


def _out_proj_kernel(ya_ref, yb_ref, yc_ref, w_ref, x_ref, gt_ref, o_ref, wb_ref=None):
    if wb_ref is not None:
        wb_ref[...] = w_ref[...].astype(BF16)
        w_ref = wb_ref
    acc = jnp.dot(ya_ref[...], w_ref[0:W_A, :], preferred_element_type=F32)
    acc += jnp.dot(yb_ref[...], w_ref[W_A:W_A + W_B, :], preferred_element_type=F32)
    acc += jnp.dot(yc_ref[...], w_ref[W_A + W_B:, :], preferred_element_type=F32)
    o_ref[...] = x_ref[...] + gt_ref[...] * acc.reshape(o_ref.shape)


def _out_proj(ya, yb, yc, w, layer, x, mod3, gt_idx):
    n_seq, t, d = x.shape
    rows = n_seq * t
    emit = w.dtype != BF16
    bs, tt = _row_tiling(t, TM)
    nt = t // tt
    tn = 512 if emit else 1024
    xo_spec = pl.BlockSpec((bs, tt, tn), lambda i, j: (i // nt, i % nt, j))
    out_shape = [jax.ShapeDtypeStruct(x.shape, F32)]
    out_specs = [xo_spec]
    if emit:
        assert rows == TM
        out_shape.append(jax.ShapeDtypeStruct((d, d), BF16))
        out_specs.append(pl.BlockSpec((d, tn), lambda i, j: (0, j)))
    res = pl.pallas_call(
        _out_proj_kernel, out_shape=out_shape,
        grid=(rows // TM, d // tn),
        in_specs=[pl.BlockSpec((TM, W_A), lambda i, j: (i, 0)),
                  pl.BlockSpec((TM, W_B), lambda i, j: (i, 0)),
                  pl.BlockSpec((TM, W_C), lambda i, j: (i, 0)),
                  _w_spec(w, layer, d, tn, lambda i, j: j),
                  xo_spec,
                  pl.BlockSpec((bs, 1, tn), lambda i, j: (i // nt, 0, gt_idx * (d // tn) + j))],
        out_specs=out_specs,
        compiler_params=_cparams(("parallel", "parallel")),
        name="out_proj_cast" if emit else "out_proj")(ya, yb, yc, w, x, mod3)
    return (res[0], res[1]) if emit else (res[0], None)


def _swiglu_kernel(h_ref, wg_ref, wu_ref, a_ref, wgb_ref=None, wub_ref=None):
    h = h_ref[...]
    g = jnp.dot(h, _bf16_tile(wg_ref, wgb_ref), preferred_element_type=F32)
    u = jnp.dot(h, _bf16_tile(wu_ref, wub_ref), preferred_element_type=F32)
    a_ref[...] = (_silu(g) * u).astype(BF16)


def _swiglu(h, wg, wu, layer):
    rows, k = h.shape
    n = wg.shape[-1]
    emit = wg.dtype != BF16
    tn = 256
    tm = TM if emit or rows % (2 * TM) else 2 * TM
    out_shape = [jax.ShapeDtypeStruct((rows, n), BF16)]
    out_specs = [pl.BlockSpec((tm, tn), lambda i, j: (i, j))]
    if emit:
        assert rows == TM
        out_shape += [jax.ShapeDtypeStruct((k, n), BF16)] * 2
        out_specs += [pl.BlockSpec((k, tn), lambda i, j: (0, j))] * 2
    res = pl.pallas_call(
        _swiglu_kernel, out_shape=out_shape,
        grid=(rows // tm, n // tn),
        in_specs=[pl.BlockSpec((tm, k), lambda i, j: (i, 0)),
                  _w_spec(wg, layer, k, tn, lambda i, j: j),
                  _w_spec(wu, layer, k, tn, lambda i, j: j)],
        out_specs=out_specs,
        compiler_params=_cparams(("parallel", "parallel")),
        name="swiglu_cast" if emit else "swiglu")(h, wg, wu)
    return (res[0], res[1], res[2]) if emit else (res[0], None, None)


def _down_proj_cast_kernel(a_ref, w_ref, x_ref, gt_ref, o_ref, wb_ref, acc_ref):
    kk = pl.program_id(0)
    j = pl.program_id(1)
    part = jnp.dot(a_ref[...], _bf16_tile(w_ref, wb_ref), preferred_element_type=F32)

    @pl.when(kk == 0)
    def _():
        acc_ref[j] = part

    @pl.when(kk == 1)
    def _():
        o_ref[...] = x_ref[...] + gt_ref[...] * (acc_ref[j] + part).reshape(o_ref.shape)


def _down_proj_kernel(a_ref, w_ref, x_ref, gt_ref, o_ref):
    acc = jnp.dot(a_ref[...], w_ref[...], preferred_element_type=F32)
    o_ref[...] = x_ref[...] + gt_ref[...] * acc.reshape(o_ref.shape)


def _down_proj(a, w, layer, x, mod3, gt_idx):
    n_seq, t, d = x.shape
    rows, k = a.shape
    bs, tt = _row_tiling(t, TM)
    nt = t // tt
    if w.dtype == BF16:
        tn = 512
        n_j = d // tn
        xo_spec = pl.BlockSpec((bs, tt, tn), lambda i, j: (i // nt, i % nt, j))
        out = pl.pallas_call(
            _down_proj_kernel, out_shape=jax.ShapeDtypeStruct(x.shape, F32),
            grid=(rows // TM, n_j),
            in_specs=[pl.BlockSpec((TM, k), lambda i, j: (i, 0), pipeline_mode=pl.Buffered(1)),
                      pl.BlockSpec((k, tn), lambda i, j: (0, j)),
                      xo_spec,
                      pl.BlockSpec((bs, 1, tn), lambda i, j: (i // nt, 0, gt_idx * n_j + j))],
            out_specs=xo_spec,
            compiler_params=_cparams(("parallel", "parallel")),
            name="down_proj")(a, w, x, mod3)
        return out, None
    assert rows == TM
    tn = 256
    n_j = d // tn
    tk = k // 2
    xo_spec = pl.BlockSpec((bs, tt, tn), lambda kk, j: (0, 0, j * kk))
    out, wb = pl.pallas_call(
        _down_proj_cast_kernel,
        out_shape=(jax.ShapeDtypeStruct(x.shape, F32), jax.ShapeDtypeStruct((k, d), BF16)),
        grid=(2, n_j),
        in_specs=[pl.BlockSpec((TM, tk), lambda kk, j: (0, kk), pipeline_mode=pl.Buffered(1)),
                  pl.BlockSpec((None, tk, tn), lambda kk, j: (layer, kk, j)),
                  xo_spec,
                  pl.BlockSpec((bs, 1, tn), lambda kk, j: (0, 0, gt_idx * n_j + j * kk))],
        out_specs=(xo_spec, pl.BlockSpec((tk, tn), lambda kk, j: (kk, j))),
        scratch_shapes=[pltpu.VMEM((n_j, TM, tn), F32)],
        compiler_params=_cparams(("arbitrary", "arbitrary")),
        name="down_proj_cast")(a, w, x, mod3)
    return out, wb


def _mix_ac_kernel(u_ref, v_ref, b_ref, c_ref, hc_ref, wt_ref, bias_ref, wc_ref, buf_ref,
                   ya_ref, yc_ref, bufnew_ref, *rest, seq_len):
    zbuf = rest[-1]
    vn_ref = rest[0] if len(rest) == 2 else None
    bs, tt, w = u_ref.shape
    rows = bs * tt
    t = pl.program_id(1)
    pad = 8

    @pl.when(t == 0)
    def _():
        zbuf[:, pad - 2:pad, :] = buf_ref[...]

    zc = c_ref[...] * hc_ref[...]
    zbuf[:, pad:pad + tt, :] = zc
    y = zbuf[:, pad - 2:pad - 2 + tt, :] * wc_ref[0:1, :]
    y = y + zbuf[:, pad - 1:pad - 1 + tt, :] * wc_ref[1:2, :]
    y = y + zc * wc_ref[2:3, :]
    yc_ref[...] = (b_ref[...] * y).reshape(rows, w).astype(BF16)
    last2 = zbuf[:, pad + tt - 2:pad + tt, :]
    zbuf[:, pad - 2:pad, :] = last2

    @pl.when(t == pl.num_programs(1) - 1)
    def _():
        bufnew_ref[...] = last2

    gu = _gelu_tanh(u_ref[...].reshape(rows, w))
    gv = _gelu_tanh(v_ref[...].reshape(rows, w))
    ri = lax.broadcasted_iota(jnp.int32, (rows, rows), 0)
    ci = lax.broadcasted_iota(jnp.int32, (rows, rows), 1)
    keep = ci <= ri
    if seq_len < rows:
        keep = keep & ((ri // seq_len) == (ci // seq_len))
    for g in range(G_A):
        sl = slice(g * DH_A, (g + 1) * DH_A)
        vg = gv[:, sl]
        dv = vg - jnp.mean(vg, axis=-1, keepdims=True)
        vn = dv * lax.rsqrt(jnp.mean(dv * dv, axis=-1, keepdims=True) + EPS)
        wm = jnp.where(keep, wt_ref[g], 0.0).astype(BF16)
        mixed = jnp.dot(wm, vn.astype(BF16), preferred_element_type=F32) + bias_ref[:, sl]
        ya_ref[:, sl] = (gu[:, sl] * mixed).astype(BF16)
        if vn_ref is not None:
            vn_ref[:, :, sl] = vn.reshape(bs, tt, DH_A)


def _mix_ac(z_head, z_tail, wt, bias_full, w_conv, layer, buf, emit_v):
    n_seq, t, _ = z_head.shape
    bs, tt = _row_tiling(t, ROW_TILE)
    nt = t // tt
    rows = n_seq * t
    zspec = lambda c: pl.BlockSpec((bs, tt, W_A), lambda i, j: (i, j, c))
    y_spec = pl.BlockSpec((ROW_TILE, W_A), lambda i, j: (i * nt + j, 0))
    out_shape = [jax.ShapeDtypeStruct((rows, W_A), BF16),
                 jax.ShapeDtypeStruct((rows, W_C), BF16),
                 jax.ShapeDtypeStruct((n_seq, CONV_W - 1, W_C), F32)]
    out_specs = [y_spec, y_spec, pl.BlockSpec((bs, CONV_W - 1, W_C), lambda i, j: (i, 0, 0))]
    if emit_v:
        out_shape.append(jax.ShapeDtypeStruct((n_seq, t, W_A), F32))
        out_specs.append(pl.BlockSpec((bs, tt, W_A), lambda i, j: (i, j, 0)))
    return pl.pallas_call(
        functools.partial(_mix_ac_kernel, seq_len=min(t, CHUNK)),
        out_shape=out_shape,
        grid=(n_seq // bs, nt),
        in_specs=[zspec(0), zspec(1), zspec(0), zspec(1), zspec(2),
                  pl.BlockSpec((G_A, ROW_TILE, ROW_TILE), lambda i, j: (0, 0, 0)),
                  pl.BlockSpec((ROW_TILE, W_A), lambda i, j: (0, 0)),
                  pl.BlockSpec((None, CONV_W, W_C), lambda i, j: (layer, 0, 0)),
                  pl.BlockSpec((None, bs, CONV_W - 1, W_C), lambda i, j: (layer, i, 0, 0))],
        out_specs=out_specs,
        scratch_shapes=[pltpu.VMEM((bs, tt + 8, W_C), F32)],
        compiler_params=_cparams(("parallel", "arbitrary")),
        name="mix_ac")(z_head, z_head, z_tail, z_tail, z_tail, wt, bias_full, w_conv, buf)


def _gla_tables(rows, seq_len):
    i = np.arange(rows)[:, None]
    t = np.arange(rows)[None, :]
    same = (i // seq_len) == (t // seq_len)
    slabs = [same & (t <= i)]
    masks = [i == t]
    s = seq_len // 2
    while s >= 1:
        second = (i // s) % 2 == 1
        start2 = (i // s) * s
        end1 = start2 + s - 1
        if s < MIN_REF_LEVEL:
            slabs.append(np.where(second, (t >= start2) & (t <= i), (t > i) & (t <= end1)))
        masks.append(((i // (2 * s)) == (t // (2 * s))) & second & ((t // s) % 2 == 0))
        s //= 2
    return (np.concatenate(slabs, 0).astype(np.float32), np.stack(masks).astype(np.float32))


def _block_ref_exponent(g, block, ref_row, flip_from):
    rows, dk = g.shape
    gr = g.reshape(rows // block, block, dk)
    d = gr - gr[:, ref_row:ref_row + 1, :]
    pos = lax.broadcasted_iota(jnp.int32, gr.shape, 1)
    return jnp.where(pos >= flip_from, d, -d).reshape(rows, dk)


def _gla_kernel(q_ref, k_ref, v_ref, r_ref, la_ref, mall_ref, mask_ref, g_ref, s0_ref, *rest):
    y_ref, snew_ref, s_scr = rest[-3:]
    bs, tt, _ = q_ref.shape
    hp = s0_ref.shape[1]
    rows = bs * tt
    n_lvl = mask_ref.shape[0] - 1
    t = pl.program_id(2)

    @pl.when(t == 0)
    def _():
        s_scr[...] = s0_ref[...]

    ri = lax.broadcasted_iota(jnp.int32, (DK_B, DK_B), 0)
    ci = lax.broadcasted_iota(jnp.int32, (DK_B, DK_B), 1)
    eye = ri == ci
    nt_dims = (((1,), (1,)), ((), ()))
    tn_dims = (((0,), (0,)), ((), ()))

    for hh in range(hp):
        ks = slice(hh * DK_B, (hh + 1) * DK_B)
        vs = slice(hh * DV_B, (hh + 1) * DV_B)
        q = q_ref[:, :, ks].reshape(rows, DK_B) * (DK_B ** -0.5)
        k = k_ref[:, :, ks].reshape(rows, DK_B)
        v = v_ref[:, :, vs].reshape(rows, DV_B)
        vb = v.astype(BF16)
        la = la_ref[:, :, ks].reshape(rows, DK_B) * LOG2_E
        la_hi = la.astype(BF16)
        la_lo = (la - la_hi.astype(F32)).astype(BF16)
        x = jnp.dot(mall_ref[...], jnp.concatenate([la_hi, la_lo], axis=1), preferred_element_type=F32)
        x = x[:, :DK_B] + x[:, DK_B:]
        g = x[0:rows]

        a = mask_ref[0] * lax.dot_general(q.astype(BF16), k.astype(BF16), nt_dims,
                                          preferred_element_type=F32)
        n_slab = 1
        for lvl in range(n_lvl):
            s = tt >> (lvl + 1)
            if s >= MIN_REF_LEVEL:
                xl = _block_ref_exponent(g, 2 * s, s - 1, s)
            else:
                xl = x[n_slab * rows:(n_slab + 1) * rows]
                n_slab += 1
            el = jnp.exp2(xl)
            p = lax.dot_general((q * el).astype(BF16), (k * el).astype(BF16), nt_dims,
                                preferred_element_type=F32)
            a = a + mask_ref[lvl + 1] * p
        o_intra = jnp.dot(a.astype(BF16), vb, preferred_element_type=F32)
        e_g = jnp.exp2(g)
        qg = q * e_g
        kd = k * jnp.exp2(_block_ref_exponent(g, tt, tt - 1, tt))

        o_parts = []
        for b in range(bs):
            rs = slice(b * tt, (b + 1) * tt)
            s = s_scr[b, hh]
            o_parts.append(o_intra[rs] + jnp.dot(qg[rs].astype(BF16), s.astype(BF16),
                                                 preferred_element_type=F32))
            e_last = e_g[(b + 1) * tt - 1:(b + 1) * tt]
            e_col = jnp.sum(jnp.where(eye, jnp.broadcast_to(e_last, (DK_B, DK_B)), 0.0),
                            axis=1, keepdims=True)
            kv = lax.dot_general(kd[rs].astype(BF16), v[rs].astype(BF16), tn_dims,
                                 preferred_element_type=F32)
            s_scr[b, hh] = e_col * s + kv
        o = o_parts[0] if bs == 1 else jnp.concatenate(o_parts, axis=0)
        yn = o * lax.rsqrt(jnp.mean(o * o, axis=-1, keepdims=True) + EPS) * g_ref[...]
        y_ref[:, vs] = (yn * _silu(r_ref[:, :, vs].reshape(rows, DV_B))).astype(BF16)

    @pl.when(t == pl.num_programs(2) - 1)
    def _():
        snew_ref[...] = s_scr[...]


def _gla(z_head, la, g_gla, s0, layer, s_out_prev):
    n_seq, t, _ = z_head.shape
    bs, tt = _row_tiling(t, ROW_TILE)
    hp = 8 if bs == 1 else 1
    nt = t // tt
    rows = n_seq * t
    mall, masks = _gla_tables(ROW_TILE, tt)
    mall = jnp.asarray(mall, BF16)
    masks = jnp.asarray(masks, F32)
    zs = lambda w, c0: pl.BlockSpec((bs, tt, hp * w), lambda i, h, j: (i, j, c0 // hp + h))
    s_spec = pl.BlockSpec((None, bs, hp, DK_B, DV_B), lambda i, h, j: (layer, i, h, 0, 0))
    in_specs = [zs(DK_B, ZQ0), zs(DK_B, ZK0), zs(DV_B, ZV0), zs(DV_B, ZR0),
                pl.BlockSpec((bs, tt, hp * DK_B), lambda i, h, j: (i, j, h)),
                pl.BlockSpec(mall.shape, lambda i, h, j: (0, 0)),
                pl.BlockSpec(masks.shape, lambda i, h, j: (0, 0, 0)),
                pl.BlockSpec((None, 1, DV_B), lambda i, h, j: (layer, 0, 0)),
                s_spec]
    args = [z_head, z_head, z_head, z_head, la, mall, masks, g_gla, s0]
    aliases = {}
    if s_out_prev is not None:
        in_specs.append(pl.BlockSpec(memory_space=pl.ANY))
        aliases = {len(args): 1}
        args.append(s_out_prev)
    return pl.pallas_call(
        _gla_kernel,
        out_shape=(jax.ShapeDtypeStruct((rows, W_B), BF16),
                   jax.ShapeDtypeStruct((DEPTH, n_seq, H_B, DK_B, DV_B), F32)),
        grid=(n_seq // bs, H_B // hp, nt),
        in_specs=in_specs,
        out_specs=(pl.BlockSpec((ROW_TILE, hp * DV_B), lambda i, h, j: (i * nt + j, h)), s_spec),
        scratch_shapes=[pltpu.VMEM((bs, hp, DK_B, DV_B), F32)],
        input_output_aliases=aliases,
        compiler_params=_cparams(("parallel", "parallel", "arbitrary")),
        name="gla")(*args)


def _layer(x, layer, mod3, s_gla, buf_conv, p, wts, s_out_prev, emit_v):
    n_seq, t, d = x.shape
    seq_len = min(t, CHUNK)
    reps = ROW_TILE // seq_len
    h, la = _norm_call(x, p["g_mix"], mod3, 1, 0, gate_w=(p["w_a"], p["w_a2"], p["b_a2"]))
    z_head, wb_head = _in_proj(h, wts["w_head"], layer, A_OFF, n_seq, t)
    z_tail, wb_tail = _in_proj(h, wts["w_tail"], layer, P_TAIL, n_seq, t)
    wt = jnp.tile(p["w_s"][:, :seq_len, :seq_len], (1, reps, reps))
    bias_full = jnp.repeat(jnp.tile(p["b_s"][:, :seq_len].T, (reps, 1)), DH_A, axis=1)
    mix = _mix_ac(z_head, z_tail, wt, bias_full, p["w_conv"], layer, buf_conv, emit_v)
    ya, yc, buf_new = mix[:3]
    vn = mix[3] if emit_v else None
    yb, s_out = _gla(z_head, la, p["g_gla"], s_gla, layer, s_out_prev)
    x, wb_out = _out_proj(ya, yb, yc, wts["w_out"], layer, x, mod3, 2)
    h2 = _norm_call(x, p["g_ffn"], mod3, 4, 3)
    a, wb_gate, wb_up = _swiglu(h2, wts["w_gate"], wts["w_up"], layer)
    x, wb_down = _down_proj(a, wts["w_down"], layer, x, mod3, 5)
    wb = dict(w_head=wb_head, w_tail=wb_tail, w_out=wb_out, w_gate=wb_gate, w_up=wb_up, w_down=wb_down)
    return x, s_out, buf_new, vn, wb


def kernel(x_prompt, x_sample, state_gla, state_conv, c_prompt, c_sample, g_mix, g_ffn, w_mod, b_mod,
           w_in, w_s, b_s, w_a2, b_a2, g_gla, w_conv, w_out, w_gate, w_up, w_down, g_final):
    bp = x_prompt.shape[0]
    bd = x_sample.shape[0]
    n_c = bp + bd
    mc = -(-n_c // 16) * 16
    c_all = jnp.concatenate([c_prompt, c_sample, jnp.zeros((mc - n_c, D_MODEL), F32)], axis=0)
    mod = _modulation(c_all, w_mod, b_mod)
    gf = g_final.reshape(1, D_MODEL)
    g_gla3 = g_gla.reshape(DEPTH, 1, DV_B)
    gla0 = jnp.zeros((DEPTH, bp, H_B, DK_B, DV_B), F32)
    conv0 = jnp.zeros((DEPTH, bp, CONV_W - 1, W_C), F32)

    xs, xp = x_sample, x_prompt
    gla_s = gla_p = None
    conv_s, conv_p, v_s = [], [], []
    wt_in = jnp.swapaxes(w_in, 1, 2)
    for l in range(DEPTH):
        w_a = wt_in[l, A_OFF:A_OFF + GATE_RANK, :].T
        p = dict(
            g_mix=g_mix[l].reshape(1, D_MODEL), g_ffn=g_ffn[l].reshape(1, D_MODEL),
            w_a=jnp.pad(w_a, ((0, 0), (0, LANES - GATE_RANK))),
            w_a2=jnp.pad(w_a2[l], ((0, LANES - GATE_RANK), (0, 0))).astype(BF16),
            b_a2=b_a2[l].reshape(1, KW_B),
            w_s=w_s[l], b_s=b_s[l], g_gla=g_gla3, w_conv=w_conv)
        w_f32 = dict(w_head=wt_in, w_tail=wt_in[l, A_OFF + GATE_RANK:, :], w_out=w_out,
                     w_gate=w_gate, w_up=w_up, w_down=w_down)
        mod_s = mod[l, bp:n_c].reshape(bd, 1, N_MOD * D_MODEL)
        mod_p = mod[l, :bp].reshape(bp, 1, N_MOD * D_MODEL)
        xs, gla_s, buf_s, vn_s, w_bf16 = _layer(xs, l, mod_s, state_gla, state_conv, p, w_f32, gla_s, True)
        xp, gla_p, buf_p, _, _ = _layer(xp, l, mod_p, gla0, conv0, p, w_bf16, gla_p, False)
        conv_s.append(buf_s)
        conv_p.append(buf_p)
        v_s.append(vn_s)
    y_s = _final_norm(xs, gf)
    y_p = _final_norm(xp, gf)
    return (y_p, y_s, gla_p, jnp.stack(conv_p), gla_s, jnp.stack(conv_s), jnp.stack(v_s))
```

```python
import functools

import numpy as np
import jax
import jax.numpy as jnp
from jax import lax
from jax.experimental import pallas as pl
from jax.experimental.pallas import tpu as pltpu

F32 = jnp.float32
BF16 = jnp.bfloat16

D_MODEL = 4096
DEPTH = 2
EPS = 1e-6
CHUNK = 128
W_A = D_MODEL // 4
G_A = 8
DH_A = W_A // G_A
W_B = D_MODEL // 2
H_B = 8
DV_B = W_B // H_B
DK_B = DV_B // 2
KW_B = H_B * DK_B
GATE_RANK = 16
GATE_TAU = 16.0
W_C = D_MODEL // 4
CONV_W = 3
D_FF = -(-8 * D_MODEL // (3 * 256)) * 256
N_MOD = 6
A_OFF = 2 * W_A + 2 * KW_B + 2 * W_B
P_TAIL = 3 * W_C

LOG2_E = float(np.log2(np.e))
LANES = 128
ROW_TILE = 128
MIN_REF_LEVEL = 4
TM = 1024
VMEM_LIMIT = 56 * 1024 * 1024
DOWN_PROJ_VMEM_LIMIT = 62 * 1024 * 1024

ZQ0 = (2 * W_A) // DK_B
ZK0 = (2 * W_A + KW_B) // DK_B
ZV0 = (2 * W_A + 2 * KW_B) // DV_B
ZR0 = (2 * W_A + 2 * KW_B + W_B) // DV_B


def _cparams(sem):
    return pltpu.CompilerParams(dimension_semantics=sem, vmem_limit_bytes=VMEM_LIMIT)


def _silu(x):
    return x / (1.0 + jnp.exp(-x))


def _gelu_tanh(x):
    c = np.float32(np.sqrt(2.0 / np.pi))
    return 0.5 * x * (1.0 + jnp.tanh(c * (x + 0.044715 * (x * x * x))))


def _row_tiling(t, rows):
    if t >= rows:
        return 1, rows
    return rows // t, t


def _w_spec(w, layer, k, tn, col_map):
    if w.ndim == 2:
        return pl.BlockSpec((k, tn), lambda *g: (0, col_map(*g)))
    return pl.BlockSpec((None, k, tn), lambda *g: (layer, 0, col_map(*g)))


def _mod_kernel(c_ref, w_ref, b_ref, o_ref):
    s = _silu(c_ref[...]).astype(BF16)
    o_ref[0] = jnp.dot(s, w_ref[0].astype(BF16), preferred_element_type=F32) + b_ref[0]


def _modulation(c_all, w_mod, b_mod):
    mc = c_all.shape[0]
    n = N_MOD * D_MODEL
    tn = 512
    return pl.pallas_call(
        _mod_kernel,
        out_shape=jax.ShapeDtypeStruct((DEPTH, mc, n), F32),
        grid=(DEPTH, n // tn),
        in_specs=[
            pl.BlockSpec((mc, D_MODEL), lambda l, j: (0, 0)),
            pl.BlockSpec((1, D_MODEL, tn), lambda l, j: (l, 0, j)),
            pl.BlockSpec((1, 1, tn), lambda l, j: (l, 0, j)),
        ],
        out_specs=pl.BlockSpec((1, mc, tn), lambda l, j: (l, 0, j)),
        compiler_params=_cparams(("parallel", "parallel")),
        name="modulation",
    )(c_all, w_mod, b_mod.reshape(DEPTH, 1, n))


def _norm_mod(x, g, sc, sh):
    y = x * lax.rsqrt(jnp.mean(x * x, axis=-1, keepdims=True) + EPS) * g
    return y * (1.0 + sc) + sh


def _norm_gate_kernel(x_ref, g_ref, sc_ref, sh_ref, wa_ref, wa2_ref, ba2_ref, h_ref, la_ref):
    bs, tt, d = x_ref.shape
    h = _norm_mod(x_ref[...], g_ref[...], sc_ref[...], sh_ref[...]).reshape(bs * tt, d).astype(BF16)
    h_ref[...] = h
    a = jnp.dot(h, wa_ref[...].astype(BF16), preferred_element_type=F32)
    pre = jnp.dot(a.astype(BF16), wa2_ref[...], preferred_element_type=F32) + ba2_ref[...]
    log_sig = jnp.minimum(pre, 0.0) - jnp.log(1.0 + jnp.exp(-jnp.abs(pre)))
    la_ref[...] = (log_sig / GATE_TAU).reshape(bs, tt, KW_B)


def _norm_kernel(x_ref, g_ref, sc_ref, sh_ref, h_ref):
    bs, tt, d = x_ref.shape
    h_ref[...] = _norm_mod(x_ref[...], g_ref[...], sc_ref[...], sh_ref[...]).reshape(bs * tt, d).astype(BF16)


def _norm_call(x, g, mod3, sc_idx, sh_idx, gate_w=None):
    n_seq, t, d = x.shape
    bs, tt = _row_tiling(t, 256)
    rows = bs * tt
    nt = t // tt
    grid = (n_seq // bs, nt)
    x_spec = pl.BlockSpec((bs, tt, d), lambda i, j: (i, j, 0))
    g_spec = pl.BlockSpec((1, d), lambda i, j: (0, 0))
    mod_spec = lambda c: pl.BlockSpec((bs, 1, d), lambda i, j: (i, 0, c))
    h_spec = pl.BlockSpec((rows, d), lambda i, j: (i * nt + j, 0))
    h_shape = jax.ShapeDtypeStruct((n_seq * t, d), BF16)
    if gate_w is None:
        return pl.pallas_call(
            _norm_kernel, out_shape=h_shape, grid=grid,
            in_specs=[x_spec, g_spec, mod_spec(sc_idx), mod_spec(sh_idx)],
            out_specs=h_spec, compiler_params=_cparams(("parallel", "parallel")),
            name="norm_mod")(x, g, mod3, mod3)
    wa, wa2, ba2 = gate_w
    return pl.pallas_call(
        _norm_gate_kernel,
        out_shape=(h_shape, jax.ShapeDtypeStruct((n_seq, t, KW_B), F32)),
        grid=grid,
        in_specs=[x_spec, g_spec, mod_spec(sc_idx), mod_spec(sh_idx),
                  pl.BlockSpec((d, LANES), lambda i, j: (0, 0)),
                  pl.BlockSpec((LANES, KW_B), lambda i, j: (0, 0)),
                  pl.BlockSpec((1, KW_B), lambda i, j: (0, 0))],
        out_specs=(h_spec, pl.BlockSpec((bs, tt, KW_B), lambda i, j: (i, j, 0))),
        compiler_params=_cparams(("parallel", "parallel")),
        name="norm_mod_gate")(x, g, mod3, mod3, wa, wa2, ba2)


def _final_norm_kernel(x_ref, g_ref, o_ref):
    x = x_ref[...]
    o_ref[...] = x * lax.rsqrt(jnp.mean(x * x, axis=-1, keepdims=True) + EPS) * g_ref[...]


def _final_norm(x, g):
    n_seq, t, d = x.shape
    bs, tt = _row_tiling(t, 256)
    return pl.pallas_call(
        _final_norm_kernel, out_shape=jax.ShapeDtypeStruct(x.shape, F32),
        grid=(n_seq // bs, t // tt),
        in_specs=[pl.BlockSpec((bs, tt, d), lambda i, j: (i, j, 0)),
                  pl.BlockSpec((1, d), lambda i, j: (0, 0))],
        out_specs=pl.BlockSpec((bs, tt, d), lambda i, j: (i, j, 0)),
        compiler_params=_cparams(("parallel", "parallel")),
        name="final_norm")(x, g)


def _bf16_tile(w_ref, wb_ref):
    if wb_ref is None:
        return w_ref[...]
    wb = w_ref[...].astype(BF16)
    wb_ref[...] = wb
    return wb


def _in_proj_kernel(h_ref, wt_ref, z_ref, wtb_ref=None):
    acc = lax.dot_general(h_ref[...], _bf16_tile(wt_ref, wtb_ref), (((1,), (1,)), ((), ())),
                          preferred_element_type=F32)
    z_ref[...] = acc.reshape(z_ref.shape)


def _in_proj(h, wt, layer, n_cols, n_seq, t):
    rows, k = h.shape
    emit = wt.dtype != BF16
    bs, tt = _row_tiling(t, TM)
    nt = t // tt
    tn = 512 if emit else 1024
    if wt.ndim == 2:
        w_spec = pl.BlockSpec((tn, k), lambda i, j: (j, 0))
    else:
        w_spec = pl.BlockSpec((None, tn, k), lambda i, j: (layer, j, 0))
    out_shape = [jax.ShapeDtypeStruct((n_seq, t, n_cols), F32)]
    out_specs = [pl.BlockSpec((bs, tt, tn), lambda i, j: (i // nt, i % nt, j))]
    if emit:
        assert rows == TM
        out_shape.append(jax.ShapeDtypeStruct((n_cols, k), BF16))
        out_specs.append(pl.BlockSpec((tn, k), lambda i, j: (j, 0)))
    res = pl.pallas_call(
        _in_proj_kernel, out_shape=out_shape,
        grid=(rows // TM, n_cols // tn),
        in_specs=[pl.BlockSpec((TM, k), lambda i, j: (i, 0)), w_spec],
        out_specs=out_specs,
        compiler_params=_cparams(("parallel", "parallel")),
        name="in_proj_cast" if emit else "in_proj")(h, wt)
    return (res[0], res[1]) if emit else (res[0], None)


def _in_proj_tail_cast_kernel(h_ref, wa_ref, wb_ref, z_ref, wtb_ref):
    wt = jnp.concatenate([wa_ref[GATE_RANK:, :], wb_ref[...]], axis=0).astype(BF16)
    wtb_ref[...] = wt
    acc = lax.dot_general(h_ref[...], wt, (((1,), (1,)), ((), ())), preferred_element_type=F32)
    z_ref[...] = acc.reshape(z_ref.shape)


def _in_proj_tail_cast(h, wt_all, layer, n_seq, t):
    rows, k = h.shape
    assert rows == TM
    bs, tt = _row_tiling(t, TM)
    tn = 512
    return pl.pallas_call(
        _in_proj_tail_cast_kernel,
        out_shape=(jax.ShapeDtypeStruct((n_seq, t, P_TAIL), F32), jax.ShapeDtypeStruct((P_TAIL, k), BF16)),
        grid=(P_TAIL // tn,),
        in_specs=[pl.BlockSpec((TM, k), lambda j: (0, 0)),
                  pl.BlockSpec((None, tn, k), lambda j: (layer, A_OFF // tn + j, 0)),
                  pl.BlockSpec((None, GATE_RANK, k),
                               lambda j: (layer, (A_OFF + tn * (j + 1)) // GATE_RANK, 0))],
        out_specs=(pl.BlockSpec((bs, tt, tn), lambda j: (0, 0, j)),
                   pl.BlockSpec((tn, k), lambda j: (j, 0))),
        compiler_params=_cparams(("parallel",)),
        name="in_proj_tail_cast")(h, wt_all, wt_all)


def _out_proj_kernel(ya_ref, yb_ref, yc_ref, w_ref, x_ref, gt_ref, o_ref, wb_ref=None):
    if wb_ref is not None:
        wb_ref[...] = w_ref[...].astype(BF16)
        w_ref = wb_ref
    acc = jnp.dot(ya_ref[...], w_ref[0:W_A, :], preferred_element_type=F32)
    acc += jnp.dot(yb_ref[...], w_ref[W_A:W_A + W_B, :], preferred_element_type=F32)
    acc += jnp.dot(yc_ref[...], w_ref[W_A + W_B:, :], preferred_element_type=F32)
    o_ref[...] = x_ref[...] + gt_ref[...] * acc.reshape(o_ref.shape)


def _out_proj(ya, yb, yc, w, layer, x, mod3, gt_idx):
    n_seq, t, d = x.shape
    rows = n_seq * t
    emit = w.dtype != BF16
    bs, tt = _row_tiling(t, TM)
    nt = t // tt
    tn = 512 if emit else 1024
    xo_spec = pl.BlockSpec((bs, tt, tn), lambda i, j: (i // nt, i % nt, j))
    out_shape = [jax.ShapeDtypeStruct(x.shape, F32)]
    out_specs = [xo_spec]
    if emit:
        assert rows == TM
        out_shape.append(jax.ShapeDtypeStruct((d, d), BF16))
        out_specs.append(pl.BlockSpec((d, tn), lambda i, j: (0, j)))
    res = pl.pallas_call(
        _out_proj_kernel, out_shape=out_shape,
        grid=(rows // TM, d // tn),
        in_specs=[pl.BlockSpec((TM, W_A), lambda i, j: (i, 0)),
                  pl.BlockSpec((TM, W_B), lambda i, j: (i, 0)),
                  pl.BlockSpec((TM, W_C), lambda i, j: (i, 0)),
                  _w_spec(w, layer, d, tn, lambda i, j: j),
                  xo_spec,
                  pl.BlockSpec((bs, 1, tn), lambda i, j: (i // nt, 0, gt_idx * (d // tn) + j))],
        out_specs=out_specs,
        compiler_params=_cparams(("parallel", "parallel")),
        name="out_proj_cast" if emit else "out_proj")(ya, yb, yc, w, x, mod3)
    return (res[0], res[1]) if emit else (res[0], None)


def _swiglu_kernel(h_ref, wg_ref, wu_ref, a_ref, wgb_ref=None, wub_ref=None):
    h = h_ref[...]
    g = jnp.dot(h, _bf16_tile(wg_ref, wgb_ref), preferred_element_type=F32)
    u = jnp.dot(h, _bf16_tile(wu_ref, wub_ref), preferred_element_type=F32)
    a_ref[...] = (_silu(g) * u).astype(BF16)


def _swiglu(h, wg, wu, layer):
    rows, k = h.shape
    n = wg.shape[-1]
    emit = wg.dtype != BF16
    tn = 256
    tm = TM if emit or rows % (2 * TM) else 2 * TM
    out_shape = [jax.ShapeDtypeStruct((rows, n), BF16)]
    out_specs = [pl.BlockSpec((tm, tn), lambda i, j: (i, j))]
    if emit:
        assert rows == TM
        out_shape += [jax.ShapeDtypeStruct((k, n), BF16)] * 2
        out_specs += [pl.BlockSpec((k, tn), lambda i, j: (0, j))] * 2
    res = pl.pallas_call(
        _swiglu_kernel, out_shape=out_shape,
        grid=(rows // tm, n // tn),
        in_specs=[pl.BlockSpec((tm, k), lambda i, j: (i, 0)),
                  _w_spec(wg, layer, k, tn, lambda i, j: j),
                  _w_spec(wu, layer, k, tn, lambda i, j: j)],
        out_specs=out_specs,
        compiler_params=_cparams(("parallel", "parallel")),
        name="swiglu_cast" if emit else "swiglu")(h, wg, wu)
    return (res[0], res[1], res[2]) if emit else (res[0], None, None)


def _down_proj_cast_kernel(a_ref, w_ref, x_ref, gt_ref, o_ref, wb_ref, acc_ref):
    kk = pl.program_id(0)
    j = pl.program_id(1)
    part = jnp.dot(a_ref[...], _bf16_tile(w_ref, wb_ref), preferred_element_type=F32)

    @pl.when(kk == 0)
    def _():
        acc_ref[j] = part

    @pl.when(kk == 1)
    def _():
        o_ref[...] = x_ref[...] + gt_ref[...] * (acc_ref[j] + part).reshape(o_ref.shape)


def _down_proj_kernel(a_ref, w_ref, x_ref, gt_ref, o_ref):
    acc = jnp.dot(a_ref[...], w_ref[...], preferred_element_type=F32)
    o_ref[...] = x_ref[...] + gt_ref[...] * acc.reshape(o_ref.shape)


def _down_proj(a, w, layer, x, mod3, gt_idx):
    n_seq, t, d = x.shape
    rows, k = a.shape
    bs, tt = _row_tiling(t, TM)
    nt = t // tt
    if w.dtype == BF16:
        tn = 256
        n_j = d // tn
        xo_spec = pl.BlockSpec((bs, tt, tn), lambda i, j: (i // nt, i % nt, j))
        out = pl.pallas_call(
            _down_proj_kernel, out_shape=jax.ShapeDtypeStruct(x.shape, F32),
            grid=(rows // TM, n_j),
            in_specs=[pl.BlockSpec((TM, k), lambda i, j: (i, 0)),
                      pl.BlockSpec((k, tn), lambda i, j: (0, j)),
                      xo_spec,
                      pl.BlockSpec((bs, 1, tn), lambda i, j: (i // nt, 0, gt_idx * n_j + j))],
            out_specs=xo_spec,
            compiler_params=pltpu.CompilerParams(dimension_semantics=("parallel", "parallel"),
                                                 vmem_limit_bytes=DOWN_PROJ_VMEM_LIMIT),
            name="down_proj")(a, w, x, mod3)
        return out, None
    assert rows == TM
    tn = 256
    n_j = d // tn
    tk = k // 2
    xo_spec = pl.BlockSpec((bs, tt, tn), lambda kk, j: (0, 0, j * kk))
    out, wb = pl.pallas_call(
        _down_proj_cast_kernel,
        out_shape=(jax.ShapeDtypeStruct(x.shape, F32), jax.ShapeDtypeStruct((k, d), BF16)),
        grid=(2, n_j),
        in_specs=[pl.BlockSpec((TM, tk), lambda kk, j: (0, kk), pipeline_mode=pl.Buffered(1)),
                  pl.BlockSpec((None, tk, tn), lambda kk, j: (layer, kk, j)),
                  xo_spec,
                  pl.BlockSpec((bs, 1, tn), lambda kk, j: (0, 0, gt_idx * n_j + j * kk))],
        out_specs=(xo_spec, pl.BlockSpec((tk, tn), lambda kk, j: (kk, j))),
        scratch_shapes=[pltpu.VMEM((n_j, TM, tn), F32)],
        compiler_params=_cparams(("arbitrary", "arbitrary")),
        name="down_proj_cast")(a, w, x, mod3)
    return out, wb


def _mix_ac_kernel(u_ref, v_ref, b_ref, c_ref, hc_ref, wt_ref, bias_ref, wc_ref, buf_ref,
                   ya_ref, yc_ref, bufnew_ref, *rest, seq_len):
    zbuf = rest[-1]
    vn_ref = rest[0] if len(rest) == 2 else None
    bs, tt, w = u_ref.shape
    rows = bs * tt
    t = pl.program_id(1)
    pad = 8

    @pl.when(t == 0)
    def _():
        zbuf[:, pad - 2:pad, :] = buf_ref[...]

    zc = c_ref[...] * hc_ref[...]
    zbuf[:, pad:pad + tt, :] = zc
    y = zbuf[:, pad - 2:pad - 2 + tt, :] * wc_ref[0:1, :]
    y = y + zbuf[:, pad - 1:pad - 1 + tt, :] * wc_ref[1:2, :]
    y = y + zc * wc_ref[2:3, :]
    yc_ref[...] = (b_ref[...] * y).reshape(rows, w).astype(BF16)
    last2 = zbuf[:, pad + tt - 2:pad + tt, :]
    zbuf[:, pad - 2:pad, :] = last2

    @pl.when(t == pl.num_programs(1) - 1)
    def _():
        bufnew_ref[...] = last2

    gu = _gelu_tanh(u_ref[...].reshape(rows, w))
    gv = _gelu_tanh(v_ref[...].reshape(rows, w))
    ri = lax.broadcasted_iota(jnp.int32, (rows, rows), 0)
    ci = lax.broadcasted_iota(jnp.int32, (rows, rows), 1)
    keep = ci <= ri
    if seq_len < rows:
        keep = keep & ((ri // seq_len) == (ci // seq_len))
    for g in range(G_A):
        sl = slice(g * DH_A, (g + 1) * DH_A)
        vg = gv[:, sl]
        dv = vg - jnp.mean(vg, axis=-1, keepdims=True)
        vn = dv * lax.rsqrt(jnp.mean(dv * dv, axis=-1, keepdims=True) + EPS)
        wm = jnp.where(keep, wt_ref[g], 0.0).astype(BF16)
        mixed = jnp.dot(wm, vn.astype(BF16), preferred_element_type=F32) + bias_ref[:, sl]
        ya_ref[:, sl] = (gu[:, sl] * mixed).astype(BF16)
        if vn_ref is not None:
            vn_ref[:, :, sl] = vn.reshape(bs, tt, DH_A)


def _mix_ac(z_head, z_tail, wt, bias_full, w_conv, layer, buf, emit_v):
    n_seq, t, _ = z_head.shape
    bs, tt = _row_tiling(t, ROW_TILE)
    nt = t // tt
    rows = n_seq * t
    zspec = lambda c: pl.BlockSpec((bs, tt, W_A), lambda i, j: (i, j, c))
    y_spec = pl.BlockSpec((ROW_TILE, W_A), lambda i, j: (i * nt + j, 0))
    out_shape = [jax.ShapeDtypeStruct((rows, W_A), BF16),
                 jax.ShapeDtypeStruct((rows, W_C), BF16),
                 jax.ShapeDtypeStruct((n_seq, CONV_W - 1, W_C), F32)]
    out_specs = [y_spec, y_spec, pl.BlockSpec((bs, CONV_W - 1, W_C), lambda i, j: (i, 0, 0))]
    if emit_v:
        out_shape.append(jax.ShapeDtypeStruct((n_seq, t, W_A), F32))
        out_specs.append(pl.BlockSpec((bs, tt, W_A), lambda i, j: (i, j, 0)))
    return pl.pallas_call(
        functools.partial(_mix_ac_kernel, seq_len=min(t, CHUNK)),
        out_shape=out_shape,
        grid=(n_seq // bs, nt),
        in_specs=[zspec(0), zspec(1), zspec(0), zspec(1), zspec(2),
                  pl.BlockSpec((G_A, ROW_TILE, ROW_TILE), lambda i, j: (0, 0, 0)),
                  pl.BlockSpec((ROW_TILE, W_A), lambda i, j: (0, 0)),
                  pl.BlockSpec((None, CONV_W, W_C), lambda i, j: (layer, 0, 0)),
                  pl.BlockSpec((None, bs, CONV_W - 1, W_C), lambda i, j: (layer, i, 0, 0))],
        out_specs=out_specs,
        scratch_shapes=[pltpu.VMEM((bs, tt + 8, W_C), F32)],
        compiler_params=_cparams(("parallel", "arbitrary")),
        name="mix_ac")(z_head, z_head, z_tail, z_tail, z_tail, wt, bias_full, w_conv, buf)


def _gla_tables(rows, seq_len):
    i = np.arange(rows)[:, None]
    t = np.arange(rows)[None, :]
    same = (i // seq_len) == (t // seq_len)
    slabs = [same & (t <= i)]
    masks = [i == t]
    s = seq_len // 2
    while s >= 1:
        second = (i // s) % 2 == 1
        start2 = (i // s) * s
        end1 = start2 + s - 1
        if s < MIN_REF_LEVEL:
            slabs.append(np.where(second, (t >= start2) & (t <= i), (t > i) & (t <= end1)))
        masks.append(((i // (2 * s)) == (t // (2 * s))) & second & ((t // s) % 2 == 0))
        s //= 2
    return (np.concatenate(slabs, 0).astype(np.float32), np.stack(masks).astype(np.float32))


def _block_ref_exponent(g, block, ref_row, flip_from):
    rows, dk = g.shape
    gr = g.reshape(rows // block, block, dk)
    d = gr - gr[:, ref_row:ref_row + 1, :]
    pos = lax.broadcasted_iota(jnp.int32, gr.shape, 1)
    return jnp.where(pos >= flip_from, d, -d).reshape(rows, dk)


def _gla_kernel(q_ref, k_ref, v_ref, r_ref, la_ref, mall_ref, mask_ref, g_ref, s0_ref, *rest):
    y_ref, snew_ref, s_scr = rest[-3:]
    bs, tt, _ = q_ref.shape
    hp = s0_ref.shape[1]
    rows = bs * tt
    n_lvl = mask_ref.shape[0] - 1
    t = pl.program_id(2)

    @pl.when(t == 0)
    def _():
        s_scr[...] = s0_ref[...]

    ri = lax.broadcasted_iota(jnp.int32, (DK_B, DK_B), 0)
    ci = lax.broadcasted_iota(jnp.int32, (DK_B, DK_B), 1)
    eye = ri == ci
    nt_dims = (((1,), (1,)), ((), ()))
    tn_dims = (((0,), (0,)), ((), ()))

    for hh in range(hp):
        ks = slice(hh * DK_B, (hh + 1) * DK_B)
        vs = slice(hh * DV_B, (hh + 1) * DV_B)
        q = q_ref[:, :, ks].reshape(rows, DK_B) * (DK_B ** -0.5)
        k = k_ref[:, :, ks].reshape(rows, DK_B)
        v = v_ref[:, :, vs].reshape(rows, DV_B)
        vb = v.astype(BF16)
        la = la_ref[:, :, ks].reshape(rows, DK_B) * LOG2_E
        la_hi = la.astype(BF16)
        la_lo = (la - la_hi.astype(F32)).astype(BF16)
        x = jnp.dot(mall_ref[...], jnp.concatenate([la_hi, la_lo], axis=1), preferred_element_type=F32)
        x = x[:, :DK_B] + x[:, DK_B:]
        g = x[0:rows]

        a = mask_ref[0] * lax.dot_general(q.astype(BF16), k.astype(BF16), nt_dims,
                                          preferred_element_type=F32)
        n_slab = 1
        for lvl in range(n_lvl):
            s = tt >> (lvl + 1)
            if s >= MIN_REF_LEVEL:
                xl = _block_ref_exponent(g, 2 * s, s - 1, s)
            else:
                xl = x[n_slab * rows:(n_slab + 1) * rows]
                n_slab += 1
            el = jnp.exp2(xl)
            p = lax.dot_general((q * el).astype(BF16), (k * el).astype(BF16), nt_dims,
                                preferred_element_type=F32)
            a = a + mask_ref[lvl + 1] * p
        o_intra = jnp.dot(a.astype(BF16), vb, preferred_element_type=F32)
        e_g = jnp.exp2(g)
        qg = q * e_g
        kd = k * jnp.exp2(_block_ref_exponent(g, tt, tt - 1, tt))

        o_parts = []
        for b in range(bs):
            rs = slice(b * tt, (b + 1) * tt)
            s = s_scr[b, hh]
            o_parts.append(o_intra[rs] + jnp.dot(qg[rs].astype(BF16), s.astype(BF16),
                                                 preferred_element_type=F32))
            e_last = e_g[(b + 1) * tt - 1:(b + 1) * tt]
            e_col = jnp.sum(jnp.where(eye, jnp.broadcast_to(e_last, (DK_B, DK_B)), 0.0),
                            axis=1, keepdims=True)
            kv = lax.dot_general(kd[rs].astype(BF16), v[rs].astype(BF16), tn_dims,
                                 preferred_element_type=F32)
            s_scr[b, hh] = e_col * s + kv
        o = o_parts[0] if bs == 1 else jnp.concatenate(o_parts, axis=0)
        yn = o * lax.rsqrt(jnp.mean(o * o, axis=-1, keepdims=True) + EPS) * g_ref[...]
        y_ref[:, vs] = (yn * _silu(r_ref[:, :, vs].reshape(rows, DV_B))).astype(BF16)

    @pl.when(t == pl.num_programs(2) - 1)
    def _():
        snew_ref[...] = s_scr[...]


def _gla(z_head, la, g_gla, s0, layer, s_out_prev):
    n_seq, t, _ = z_head.shape
    tile = ROW_TILE if t >= ROW_TILE else 2 * ROW_TILE
    bs, tt = _row_tiling(t, tile)
    hp = 8 if bs == 1 else 1
    nt = t // tt
    rows = n_seq * t
    mall, masks = _gla_tables(tile, tt)
    mall = jnp.asarray(mall, BF16)
    masks = jnp.asarray(masks, F32)
    zs = lambda w, c0: pl.BlockSpec((bs, tt, hp * w), lambda i, h, j: (i, j, c0 // hp + h))
    s_spec = pl.BlockSpec((None, bs, hp, DK_B, DV_B), lambda i, h, j: (layer, i, h, 0, 0))
    in_specs = [zs(DK_B, ZQ0), zs(DK_B, ZK0), zs(DV_B, ZV0), zs(DV_B, ZR0),
                pl.BlockSpec((bs, tt, hp * DK_B), lambda i, h, j: (i, j, h)),
                pl.BlockSpec(mall.shape, lambda i, h, j: (0, 0)),
                pl.BlockSpec(masks.shape, lambda i, h, j: (0, 0, 0)),
                pl.BlockSpec((None, 1, DV_B), lambda i, h, j: (layer, 0, 0)),
                s_spec]
    args = [z_head, z_head, z_head, z_head, la, mall, masks, g_gla, s0]
    aliases = {}
    if s_out_prev is not None:
        in_specs.append(pl.BlockSpec(memory_space=pl.ANY))
        aliases = {len(args): 1}
        args.append(s_out_prev)
    return pl.pallas_call(
        _gla_kernel,
        out_shape=(jax.ShapeDtypeStruct((rows, W_B), BF16),
                   jax.ShapeDtypeStruct((DEPTH, n_seq, H_B, DK_B, DV_B), F32)),
        grid=(n_seq // bs, H_B // hp, nt),
        in_specs=in_specs,
        out_specs=(pl.BlockSpec((tile, hp * DV_B), lambda i, h, j: (i * nt + j, h)), s_spec),
        scratch_shapes=[pltpu.VMEM((bs, hp, DK_B, DV_B), F32)],
        input_output_aliases=aliases,
        compiler_params=_cparams(("parallel", "parallel", "arbitrary")),
        name="gla")(*args)


def _layer(x, layer, mod3, s_gla, buf_conv, p, wts, s_out_prev, emit_v):
    n_seq, t, d = x.shape
    seq_len = min(t, CHUNK)
    reps = ROW_TILE // seq_len
    h, la = _norm_call(x, p["g_mix"], mod3, 1, 0, gate_w=(p["w_a"], p["w_a2"], p["b_a2"]))
    z_head, wb_head = _in_proj(h, wts["w_head"], layer, A_OFF, n_seq, t)
    if wts["w_tail"] is None:
        z_tail, wb_tail = _in_proj_tail_cast(h, wts["w_head"], layer, n_seq, t)
    else:
        z_tail, wb_tail = _in_proj(h, wts["w_tail"], layer, P_TAIL, n_seq, t)
    wt = jnp.tile(p["w_s"][:, :seq_len, :seq_len], (1, reps, reps))
    bias_full = jnp.repeat(jnp.tile(p["b_s"][:, :seq_len].T, (reps, 1)), DH_A, axis=1)
    mix = _mix_ac(z_head, z_tail, wt, bias_full, p["w_conv"], layer, buf_conv, emit_v)
    ya, yc, buf_new = mix[:3]
    vn = mix[3] if emit_v else None
    yb, s_out = _gla(z_head, la, p["g_gla"], s_gla, layer, s_out_prev)
    x, wb_out = _out_proj(ya, yb, yc, wts["w_out"], layer, x, mod3, 2)
    h2 = _norm_call(x, p["g_ffn"], mod3, 4, 3)
    a, wb_gate, wb_up = _swiglu(h2, wts["w_gate"], wts["w_up"], layer)
    x, wb_down = _down_proj(a, wts["w_down"], layer, x, mod3, 5)
    wb = dict(w_head=wb_head, w_tail=wb_tail, w_out=wb_out, w_gate=wb_gate, w_up=wb_up, w_down=wb_down)
    return x, s_out, buf_new, vn, wb


def kernel(x_prompt, x_sample, state_gla, state_conv, c_prompt, c_sample, g_mix, g_ffn, w_mod, b_mod,
           w_in, w_s, b_s, w_a2, b_a2, g_gla, w_conv, w_out, w_gate, w_up, w_down, g_final):
    bp = x_prompt.shape[0]
    bd = x_sample.shape[0]
    n_c = bp + bd
    mc = -(-n_c // 16) * 16
    c_all = jnp.concatenate([c_prompt, c_sample, jnp.zeros((mc - n_c, D_MODEL), F32)], axis=0)
    mod = _modulation(c_all, w_mod, b_mod)
    gf = g_final.reshape(1, D_MODEL)
    g_gla3 = g_gla.reshape(DEPTH, 1, DV_B)
    gla0 = jnp.zeros((DEPTH, bp, H_B, DK_B, DV_B), F32)
    conv0 = jnp.zeros((DEPTH, bp, CONV_W - 1, W_C), F32)

    xs, xp = x_sample, x_prompt
    gla_s = gla_p = None
    conv_s, conv_p, v_s = [], [], []
    wt_in = jnp.swapaxes(w_in, 1, 2)
    for l in range(DEPTH):
        w_a = wt_in[l, A_OFF:A_OFF + GATE_RANK, :].T
        p = dict(
            g_mix=g_mix[l].reshape(1, D_MODEL), g_ffn=g_ffn[l].reshape(1, D_MODEL),
            w_a=jnp.pad(w_a, ((0, 0), (0, LANES - GATE_RANK))),
            w_a2=jnp.pad(w_a2[l], ((0, LANES - GATE_RANK), (0, 0))).astype(BF16),
            b_a2=b_a2[l].reshape(1, KW_B),
            w_s=w_s[l], b_s=b_s[l], g_gla=g_gla3, w_conv=w_conv)
        w_f32 = dict(w_head=wt_in, w_tail=None, w_out=w_out,
                     w_gate=w_gate, w_up=w_up, w_down=w_down)
        mod_s = mod[l, bp:n_c].reshape(bd, 1, N_MOD * D_MODEL)
        mod_p = mod[l, :bp].reshape(bp, 1, N_MOD * D_MODEL)
        xs, gla_s, buf_s, vn_s, w_bf16 = _layer(xs, l, mod_s, state_gla, state_conv, p, w_f32, gla_s, True)
        xp, gla_p, buf_p, _, _ = _layer(xp, l, mod_p, gla0, conv0, p, w_bf16, gla_p, False)
        conv_s.append(buf_s)
        conv_p.append(buf_p)
        v_s.append(vn_s)
    y_s = _final_norm(xs, gf)
    y_p = _final_norm(xp, gf)
    return (y_p, y_s, gla_p, jnp.stack(conv_p), gla_s, jnp.stack(conv_s), jnp.stack(v_s))
```

```python
import functools

import numpy as np
import jax
import jax.numpy as jnp
from jax import lax
from jax.experimental import pallas as pl
from jax.experimental.pallas import tpu as pltpu

F32 = jnp.float32
BF16 = jnp.bfloat16

D_MODEL = 4096
DEPTH = 2
EPS = 1e-6
CHUNK = 128
W_A = D_MODEL // 4
G_A = 8
DH_A = W_A // G_A
W_B = D_MODEL // 2
H_B = 8
DV_B = W_B // H_B
DK_B = DV_B // 2
KW_B = H_B * DK_B
GATE_RANK = 16
GATE_TAU = 16.0
W_C = D_MODEL // 4
CONV_W = 3
D_FF = -(-8 * D_MODEL // (3 * 256)) * 256
N_MOD = 6
A_OFF = 2 * W_A + 2 * KW_B + 2 * W_B
P_TAIL = 3 * W_C

LOG2_E = float(np.log2(np.e))
LANES = 128
ROW_TILE = 128
MIN_REF_LEVEL = 4
TM = 1024
ROW_SPLIT = 2
T_AXIS = 1
PHASES = ("init", "body", "final")
VMEM_LIMIT = 56 * 1024 * 1024
DOWN_PROJ_VMEM_LIMIT = 62 * 1024 * 1024

ZQ0 = (2 * W_A) // DK_B
ZK0 = (2 * W_A + KW_B) // DK_B
ZV0 = (2 * W_A + 2 * KW_B) // DV_B
ZR0 = (2 * W_A + 2 * KW_B + W_B) // DV_B


def _cparams(sem):
    return pltpu.CompilerParams(dimension_semantics=sem, vmem_limit_bytes=VMEM_LIMIT)


def _silu(x):
    return x / (1.0 + jnp.exp(-x))


def _gelu_tanh(x):
    c = np.float32(np.sqrt(2.0 / np.pi))
    return 0.5 * x * (1.0 + jnp.tanh(c * (x + 0.044715 * (x * x * x))))


def _row_tiling(t, rows):
    if t >= rows:
        return 1, rows
    return rows // t, t


def _w_spec(w, layer, k, tn, col_map):
    if w.ndim == 2:
        return pl.BlockSpec((k, tn), lambda *g: (0, col_map(*g)))
    return pl.BlockSpec((None, k, tn), lambda *g: (layer, 0, col_map(*g)))


def _mod_kernel(c_ref, w_ref, b_ref, o_ref):
    s = _silu(c_ref[...]).astype(BF16)
    o_ref[0] = jnp.dot(s, w_ref[0].astype(BF16), preferred_element_type=F32) + b_ref[0]


def _modulation(c_all, w_mod, b_mod):
    mc = c_all.shape[0]
    n = N_MOD * D_MODEL
    tn = 512
    return pl.pallas_call(
        _mod_kernel,
        out_shape=jax.ShapeDtypeStruct((DEPTH, mc, n), F32),
        grid=(DEPTH, n // tn),
        in_specs=[
            pl.BlockSpec((mc, D_MODEL), lambda l, j: (0, 0)),
            pl.BlockSpec((1, D_MODEL, tn), lambda l, j: (l, 0, j)),
            pl.BlockSpec((1, 1, tn), lambda l, j: (l, 0, j)),
        ],
        out_specs=pl.BlockSpec((1, mc, tn), lambda l, j: (l, 0, j)),
        compiler_params=_cparams(("parallel", "parallel")),
        name="modulation",
    )(c_all, w_mod, b_mod.reshape(DEPTH, 1, n))


def _norm_mod(x, g, sc, sh):
    y = x * lax.rsqrt(jnp.mean(x * x, axis=-1, keepdims=True) + EPS) * g
    return y * (1.0 + sc) + sh


def _norm_gate_kernel(x_ref, g_ref, sc_ref, sh_ref, wa_ref, wa2_ref, ba2_ref, h_ref, la_ref):
    bs, tt, d = x_ref.shape
    h = _norm_mod(x_ref[...], g_ref[...], sc_ref[...], sh_ref[...]).reshape(bs * tt, d).astype(BF16)
    h_ref[...] = h
    a = jnp.dot(h, wa_ref[...].astype(BF16), preferred_element_type=F32)
    pre = jnp.dot(a.astype(BF16), wa2_ref[...], preferred_element_type=F32) + ba2_ref[...]
    log_sig = jnp.minimum(pre, 0.0) - jnp.log(1.0 + jnp.exp(-jnp.abs(pre)))
    la_ref[...] = (log_sig / GATE_TAU).reshape(bs, tt, KW_B)


def _norm_kernel(x_ref, g_ref, sc_ref, sh_ref, h_ref):
    bs, tt, d = x_ref.shape
    h_ref[...] = _norm_mod(x_ref[...], g_ref[...], sc_ref[...], sh_ref[...]).reshape(bs * tt, d).astype(BF16)


def _norm_call(x, g, mod3, sc_idx, sh_idx, gate_w=None):
    n_seq, t, d = x.shape
    bs, tt = _row_tiling(t, 256)
    rows = bs * tt
    nt = t // tt
    grid = (n_seq // bs, nt)
    x_spec = pl.BlockSpec((bs, tt, d), lambda i, j: (i, j, 0))
    g_spec = pl.BlockSpec((1, d), lambda i, j: (0, 0))
    mod_spec = lambda c: pl.BlockSpec((bs, 1, d), lambda i, j: (i, 0, c))
    h_spec = pl.BlockSpec((rows, d), lambda i, j: (i * nt + j, 0))
    h_shape = jax.ShapeDtypeStruct((n_seq * t, d), BF16)
    if gate_w is None:
        return pl.pallas_call(
            _norm_kernel, out_shape=h_shape, grid=grid,
            in_specs=[x_spec, g_spec, mod_spec(sc_idx), mod_spec(sh_idx)],
            out_specs=h_spec, compiler_params=_cparams(("parallel", "parallel")),
            name="norm_mod")(x, g, mod3, mod3)
    wa, wa2, ba2 = gate_w
    return pl.pallas_call(
        _norm_gate_kernel,
        out_shape=(h_shape, jax.ShapeDtypeStruct((n_seq, t, KW_B), F32)),
        grid=grid,
        in_specs=[x_spec, g_spec, mod_spec(sc_idx), mod_spec(sh_idx),
                  pl.BlockSpec((d, LANES), lambda i, j: (0, 0)),
                  pl.BlockSpec((LANES, KW_B), lambda i, j: (0, 0)),
                  pl.BlockSpec((1, KW_B), lambda i, j: (0, 0))],
        out_specs=(h_spec, pl.BlockSpec((bs, tt, KW_B), lambda i, j: (i, j, 0))),
        compiler_params=_cparams(("parallel", "parallel")),
        name="norm_mod_gate")(x, g, mod3, mod3, wa, wa2, ba2)


def _final_norm_kernel(x_ref, g_ref, o_ref):
    x = x_ref[...]
    o_ref[...] = x * lax.rsqrt(jnp.mean(x * x, axis=-1, keepdims=True) + EPS) * g_ref[...]


def _final_norm(x, g):
    n_seq, t, d = x.shape
    bs, tt = _row_tiling(t, 256)
    return pl.pallas_call(
        _final_norm_kernel, out_shape=jax.ShapeDtypeStruct(x.shape, F32),
        grid=(n_seq // bs, t // tt),
        in_specs=[pl.BlockSpec((bs, tt, d), lambda i, j: (i, j, 0)),
                  pl.BlockSpec((1, d), lambda i, j: (0, 0))],
        out_specs=pl.BlockSpec((bs, tt, d), lambda i, j: (i, j, 0)),
        compiler_params=_cparams(("parallel", "parallel")),
        name="final_norm")(x, g)


def _bf16_tile(w_ref, wb_ref):
    if wb_ref is None:
        return w_ref[...]
    wb = w_ref[...].astype(BF16)
    wb_ref[...] = wb
    return wb


def _in_proj_kernel(h_ref, wt_ref, z_ref, wtb_ref=None):
    acc = lax.dot_general(h_ref[...], _bf16_tile(wt_ref, wtb_ref), (((1,), (1,)), ((), ())),
                          preferred_element_type=F32)
    z_ref[...] = acc.reshape(z_ref.shape)


def _in_proj(h, wt, layer, n_cols, n_seq, t):
    rows, k = h.shape
    emit = wt.dtype != BF16
    bs, tt = _row_tiling(t, TM)
    nt = t // tt
    tn = 512 if emit else 1024
    if wt.ndim == 2:
        w_spec = pl.BlockSpec((tn, k), lambda i, j: (j, 0))
    else:
        w_spec = pl.BlockSpec((None, tn, k), lambda i, j: (layer, j, 0))
    out_shape = [jax.ShapeDtypeStruct((n_seq, t, n_cols), F32)]
    out_specs = [pl.BlockSpec((bs, tt, tn), lambda i, j: (i // nt, i % nt, j))]
    if emit:
        assert rows == TM
        out_shape.append(jax.ShapeDtypeStruct((n_cols, k), BF16))
        out_specs.append(pl.BlockSpec((tn, k), lambda i, j: (j, 0)))
    res = pl.pallas_call(
        _in_proj_kernel, out_shape=out_shape,
        grid=(rows // TM, n_cols // tn),
        in_specs=[pl.BlockSpec((TM, k), lambda i, j: (i, 0)), w_spec],
        out_specs=out_specs,
        compiler_params=_cparams(("parallel", "parallel")),
        name="in_proj_cast" if emit else "in_proj")(h, wt)
    return (res[0], res[1]) if emit else (res[0], None)


def _in_proj_tail_cast_kernel(h_ref, wa_ref, wb_ref, z_ref, wtb_ref):
    wt = jnp.concatenate([wa_ref[GATE_RANK:, :], wb_ref[...]], axis=0).astype(BF16)
    wtb_ref[...] = wt
    acc = lax.dot_general(h_ref[...], wt, (((1,), (1,)), ((), ())), preferred_element_type=F32)
    z_ref[...] = acc.reshape(z_ref.shape)


def _in_proj_tail_cast(h, wt_all, layer, n_seq, t):
    rows, k = h.shape
    assert rows == TM
    bs, tt = _row_tiling(t, TM)
    tn = 512
    return pl.pallas_call(
        _in_proj_tail_cast_kernel,
        out_shape=(jax.ShapeDtypeStruct((n_seq, t, P_TAIL), F32), jax.ShapeDtypeStruct((P_TAIL, k), BF16)),
        grid=(P_TAIL // tn,),
        in_specs=[pl.BlockSpec((TM, k), lambda j: (0, 0)),
                  pl.BlockSpec((None, tn, k), lambda j: (layer, A_OFF // tn + j, 0)),
                  pl.BlockSpec((None, GATE_RANK, k),
                               lambda j: (layer, (A_OFF + tn * (j + 1)) // GATE_RANK, 0))],
        out_specs=(pl.BlockSpec((bs, tt, tn), lambda j: (0, 0, j)),
                   pl.BlockSpec((tn, k), lambda j: (j, 0))),
        compiler_params=_cparams(("parallel",)),
        name="in_proj_tail_cast")(h, wt_all, wt_all)


def _out_proj_kernel(ya_ref, yb_ref, yc_ref, w_ref, x_ref, gt_ref, o_ref, wb_ref=None):
    if wb_ref is not None:
        wb_ref[...] = w_ref[...].astype(BF16)
        w_ref = wb_ref
    acc = jnp.dot(ya_ref[...], w_ref[0:W_A, :], preferred_element_type=F32)
    acc += jnp.dot(yb_ref[...], w_ref[W_A:W_A + W_B, :], preferred_element_type=F32)
    acc += jnp.dot(yc_ref[...], w_ref[W_A + W_B:, :], preferred_element_type=F32)
    o_ref[...] = x_ref[...] + gt_ref[...] * acc.reshape(o_ref.shape)


def _out_proj(ya, yb, yc, w, layer, x, mod3, gt_idx):
    n_seq, t, d = x.shape
    rows = n_seq * t
    emit = w.dtype != BF16
    bs, tt = _row_tiling(t, TM)
    nt = t // tt
    tn = 512 if emit else 1024
    xo_spec = pl.BlockSpec((bs, tt, tn), lambda i, j: (i // nt, i % nt, j))
    out_shape = [jax.ShapeDtypeStruct(x.shape, F32)]
    out_specs = [xo_spec]
    if emit:
        assert rows == TM
        out_shape.append(jax.ShapeDtypeStruct((d, d), BF16))
        out_specs.append(pl.BlockSpec((d, tn), lambda i, j: (0, j)))
    res = pl.pallas_call(
        _out_proj_kernel, out_shape=out_shape,
        grid=(rows // TM, d // tn),
        in_specs=[pl.BlockSpec((TM, W_A), lambda i, j: (i, 0)),
                  pl.BlockSpec((TM, W_B), lambda i, j: (i, 0)),
                  pl.BlockSpec((TM, W_C), lambda i, j: (i, 0)),
                  _w_spec(w, layer, d, tn, lambda i, j: j),
                  xo_spec,
                  pl.BlockSpec((bs, 1, tn), lambda i, j: (i // nt, 0, gt_idx * (d // tn) + j))],
        out_specs=out_specs,
        compiler_params=_cparams(("parallel", "parallel")),
        name="out_proj_cast" if emit else "out_proj")(ya, yb, yc, w, x, mod3)
    return (res[0], res[1]) if emit else (res[0], None)


def _swiglu_kernel(h_ref, wg_ref, wu_ref, a_ref, wgb_ref=None, wub_ref=None):
    wg = _bf16_tile(wg_ref, wgb_ref)
    wu = _bf16_tile(wu_ref, wub_ref)
    rows = h_ref.shape[0]
    part = rows // ROW_SPLIT
    for r0 in range(0, rows, part):
        h = h_ref[r0:r0 + part, :]
        g = jnp.dot(h, wg, preferred_element_type=F32)
        u = jnp.dot(h, wu, preferred_element_type=F32)
        a_ref[r0:r0 + part, :] = (_silu(g) * u).astype(BF16)


def _swiglu(h, wg, wu, layer):
    rows, k = h.shape
    n = wg.shape[-1]
    emit = wg.dtype != BF16
    tn = 256
    tm = TM if emit or rows % (2 * TM) else 2 * TM
    out_shape = [jax.ShapeDtypeStruct((rows, n), BF16)]
    out_specs = [pl.BlockSpec((tm, tn), lambda i, j: (i, j))]
    if emit:
        assert rows == TM
        out_shape += [jax.ShapeDtypeStruct((k, n), BF16)] * 2
        out_specs += [pl.BlockSpec((k, tn), lambda i, j: (0, j))] * 2
    res = pl.pallas_call(
        _swiglu_kernel, out_shape=out_shape,
        grid=(rows // tm, n // tn),
        in_specs=[pl.BlockSpec((tm, k), lambda i, j: (i, 0)),
                  _w_spec(wg, layer, k, tn, lambda i, j: j),
                  _w_spec(wu, layer, k, tn, lambda i, j: j)],
        out_specs=out_specs,
        compiler_params=_cparams(("parallel", "parallel")),
        name="swiglu_cast" if emit else "swiglu")(h, wg, wu)
    return (res[0], res[1], res[2]) if emit else (res[0], None, None)


def _down_proj_cast_kernel(a_ref, w_ref, x_ref, gt_ref, o_ref, wb_ref, acc_ref):
    kk = pl.program_id(0)
    j = pl.program_id(1)
    part = jnp.dot(a_ref[...], _bf16_tile(w_ref, wb_ref), preferred_element_type=F32)

    @pl.when(kk == 0)
    def _():
        acc_ref[j] = part

    @pl.when(kk == 1)
    def _():
        o_ref[...] = x_ref[...] + gt_ref[...] * (acc_ref[j] + part).reshape(o_ref.shape)


def _down_proj_kernel(a_ref, w_ref, x_ref, gt_ref, o_ref):
    acc = jnp.dot(a_ref[...], w_ref[...], preferred_element_type=F32)
    o_ref[...] = x_ref[...] + gt_ref[...] * acc.reshape(o_ref.shape)


def _down_proj(a, w, layer, x, mod3, gt_idx):
    n_seq, t, d = x.shape
    rows, k = a.shape
    bs, tt = _row_tiling(t, TM)
    nt = t // tt
    if w.dtype == BF16:
        tn = 256
        n_j = d // tn
        xo_spec = pl.BlockSpec((bs, tt, tn), lambda i, j: (i // nt, i % nt, j))
        out = pl.pallas_call(
            _down_proj_kernel, out_shape=jax.ShapeDtypeStruct(x.shape, F32),
            grid=(rows // TM, n_j),
            in_specs=[pl.BlockSpec((TM, k), lambda i, j: (i, 0)),
                      pl.BlockSpec((k, tn), lambda i, j: (0, j)),
                      xo_spec,
                      pl.BlockSpec((bs, 1, tn), lambda i, j: (i // nt, 0, gt_idx * n_j + j))],
            out_specs=xo_spec,
            compiler_params=pltpu.CompilerParams(dimension_semantics=("parallel", "parallel"),
                                                 vmem_limit_bytes=DOWN_PROJ_VMEM_LIMIT),
            name="down_proj")(a, w, x, mod3)
        return out, None
    assert rows == TM
    tn = 256
    n_j = d // tn
    tk = k // 2
    xo_spec = pl.BlockSpec((bs, tt, tn), lambda kk, j: (0, 0, j * kk))
    out, wb = pl.pallas_call(
        _down_proj_cast_kernel,
        out_shape=(jax.ShapeDtypeStruct(x.shape, F32), jax.ShapeDtypeStruct((k, d), BF16)),
        grid=(2, n_j),
        in_specs=[pl.BlockSpec((TM, tk), lambda kk, j: (0, kk), pipeline_mode=pl.Buffered(1)),
                  pl.BlockSpec((None, tk, tn), lambda kk, j: (layer, kk, j)),
                  xo_spec,
                  pl.BlockSpec((bs, 1, tn), lambda kk, j: (0, 0, gt_idx * n_j + j * kk))],
        out_specs=(xo_spec, pl.BlockSpec((tk, tn), lambda kk, j: (kk, j))),
        scratch_shapes=[pltpu.VMEM((n_j, TM, tn), F32)],
        compiler_params=_cparams(("arbitrary", "arbitrary")),
        name="down_proj_cast")(a, w, x, mod3)
    return out, wb


def _mix_ac_kernel(u_ref, v_ref, b_ref, c_ref, hc_ref, wt_ref, bias_ref, wc_ref, buf_ref,
                   ya_ref, yc_ref, bufnew_ref, *rest, seq_len, phases=PHASES, defer_to=None):
    zbuf = rest[-1]
    vn_ref = rest[0] if len(rest) == 2 else None
    bs, tt, w = u_ref.shape
    rows = bs * tt
    t = pl.program_id(T_AXIS)
    pad = 8

    if "init" in phases:
        @pl.when(t == 0)
        def _():
            zbuf[:, pad - 2:pad, :] = buf_ref[...]

    if "body" in phases:
        work = _mix_ac_work(u_ref, v_ref, b_ref, c_ref, hc_ref, wt_ref, bias_ref, wc_ref, ya_ref, yc_ref,
                            vn_ref, zbuf, pad, seq_len)
        if defer_to is None:
            for piece in work:
                piece()
        else:
            defer_to.extend(work)

    if "final" in phases:
        @pl.when(t == pl.num_programs(T_AXIS) - 1)
        def _():
            bufnew_ref[...] = zbuf[:, pad - 2:pad, :]


def _mix_ac_work(u_ref, v_ref, b_ref, c_ref, hc_ref, wt_ref, bias_ref, wc_ref, ya_ref, yc_ref, vn_ref, zbuf,
                 pad, seq_len):
    bs, tt, _ = u_ref.shape
    rows = bs * tt

    def piece(g):
        sl = slice(g * DH_A, (g + 1) * DH_A)
        zc = c_ref[:, :, sl] * hc_ref[:, :, sl]
        zbuf[:, pad:pad + tt, sl] = zc
        y = zbuf[:, pad - 2:pad - 2 + tt, sl] * wc_ref[0:1, sl]
        y = y + zbuf[:, pad - 1:pad - 1 + tt, sl] * wc_ref[1:2, sl]
        y = y + zc * wc_ref[2:3, sl]
        yc_ref[:, sl] = (b_ref[:, :, sl] * y).reshape(rows, DH_A).astype(BF16)
        zbuf[:, pad - 2:pad, sl] = zbuf[:, pad + tt - 2:pad + tt, sl]

        gu = _gelu_tanh(u_ref[:, :, sl].reshape(rows, DH_A))
        vg = _gelu_tanh(v_ref[:, :, sl].reshape(rows, DH_A))
        dv = vg - jnp.mean(vg, axis=-1, keepdims=True)
        vn = dv * lax.rsqrt(jnp.mean(dv * dv, axis=-1, keepdims=True) + EPS)
        ri = lax.broadcasted_iota(jnp.int32, (rows, rows), 0)
        ci = lax.broadcasted_iota(jnp.int32, (rows, rows), 1)
        keep = ci <= ri
        if seq_len < rows:
            keep = keep & ((ri // seq_len) == (ci // seq_len))
        wm = jnp.where(keep, wt_ref[g], 0.0).astype(BF16)
        mixed = jnp.dot(wm, vn.astype(BF16), preferred_element_type=F32) + bias_ref[:, sl]
        ya_ref[:, sl] = (gu * mixed).astype(BF16)
        if vn_ref is not None:
            vn_ref[:, :, sl] = vn.reshape(bs, tt, DH_A)

    return [functools.partial(piece, g) for g in range(G_A)]


def _mix_ac_parts(z_head, z_tail, wt, bias_full, w_conv, layer, buf, emit_v):
    n_seq, t, _ = z_head.shape
    bs, tt = _row_tiling(t, ROW_TILE)
    nt = t // tt
    rows = n_seq * t
    zspec = lambda c: pl.BlockSpec((bs, tt, W_A), lambda i, j, *_: (i, j, c))
    y_spec = pl.BlockSpec((ROW_TILE, W_A), lambda i, j, *_: (i * nt + j, 0))
    out_shape = [jax.ShapeDtypeStruct((rows, W_A), BF16),
                 jax.ShapeDtypeStruct((rows, W_C), BF16),
                 jax.ShapeDtypeStruct((n_seq, CONV_W - 1, W_C), F32)]
    out_specs = [y_spec, y_spec, pl.BlockSpec((bs, CONV_W - 1, W_C), lambda i, j, *_: (i, 0, 0))]
    if emit_v:
        out_shape.append(jax.ShapeDtypeStruct((n_seq, t, W_A), F32))
        out_specs.append(pl.BlockSpec((bs, tt, W_A), lambda i, j, *_: (i, j, 0)))
    in_specs = [zspec(0), zspec(1), zspec(0), zspec(1), zspec(2),
                pl.BlockSpec((G_A, ROW_TILE, ROW_TILE), lambda *_: (0, 0, 0)),
                pl.BlockSpec((ROW_TILE, W_A), lambda *_: (0, 0)),
                pl.BlockSpec((None, CONV_W, W_C), lambda *_: (layer, 0, 0)),
                pl.BlockSpec((None, bs, CONV_W - 1, W_C), lambda i, *_: (layer, i, 0, 0))]
    return dict(args=[z_head, z_head, z_tail, z_tail, z_tail, wt, bias_full, w_conv, buf],
                in_specs=in_specs, out_shape=out_shape, out_specs=out_specs,
                scratch=[pltpu.VMEM((bs, tt + 8, W_C), F32)], grid=(n_seq // bs, nt),
                kernel=functools.partial(_mix_ac_kernel, seq_len=min(t, CHUNK)))


def _mix_ac(*operands):
    m = _mix_ac_parts(*operands)
    return pl.pallas_call(
        m["kernel"], out_shape=m["out_shape"], grid=m["grid"], in_specs=m["in_specs"],
        out_specs=m["out_specs"], scratch_shapes=m["scratch"],
        compiler_params=_cparams(("parallel", "arbitrary")), name="mix_ac")(*m["args"])


def _gla_tables(rows, seq_len):
    i = np.arange(rows)[:, None]
    t = np.arange(rows)[None, :]
    same = (i // seq_len) == (t // seq_len)
    slabs = [same & (t <= i)]
    masks = [i == t]
    s = seq_len // 2
    while s >= 1:
        second = (i // s) % 2 == 1
        start2 = (i // s) * s
        end1 = start2 + s - 1
        if s < MIN_REF_LEVEL:
            slabs.append(np.where(second, (t >= start2) & (t <= i), (t > i) & (t <= end1)))
        masks.append(((i // (2 * s)) == (t // (2 * s))) & second & ((t // s) % 2 == 0))
        s //= 2
    return (np.concatenate(slabs, 0).astype(np.float32), np.stack(masks).astype(np.float32))


def _block_ref_exponent(g, block, ref_row, flip_from):
    rows, dk = g.shape
    gr = g.reshape(rows // block, block, dk)
    d = gr - gr[:, ref_row:ref_row + 1, :]
    pos = lax.broadcasted_iota(jnp.int32, gr.shape, 1)
    return jnp.where(pos >= flip_from, d, -d).reshape(rows, dk)


def _gla_kernel(q_ref, k_ref, v_ref, r_ref, la_ref, mall_ref, mask_ref, g_ref, s0_ref, *rest, phases=PHASES,
                between=()):
    y_ref, snew_ref, s_scr = rest[-3:]
    t = pl.program_id(T_AXIS)

    if "init" in phases:
        @pl.when(t == 0)
        def _():
            s_scr[...] = s0_ref[...]

    if "body" in phases:
        _gla_body(q_ref, k_ref, v_ref, r_ref, la_ref, mall_ref, mask_ref, g_ref, y_ref, s_scr, between)

    if "final" in phases:
        @pl.when(t == pl.num_programs(T_AXIS) - 1)
        def _():
            snew_ref[...] = s_scr[...]


def _gla_body(q_ref, k_ref, v_ref, r_ref, la_ref, mall_ref, mask_ref, g_ref, y_ref, s_scr, between):
    bs, tt, _ = q_ref.shape
    hp = s_scr.shape[1]
    assert len(between) in (0, hp)
    rows = bs * tt
    n_lvl = mask_ref.shape[0] - 1
    ri = lax.broadcasted_iota(jnp.int32, (DK_B, DK_B), 0)
    ci = lax.broadcasted_iota(jnp.int32, (DK_B, DK_B), 1)
    eye = ri == ci
    nt_dims = (((1,), (1,)), ((), ()))
    tn_dims = (((0,), (0,)), ((), ()))

    for hh in range(hp):
        ks = slice(hh * DK_B, (hh + 1) * DK_B)
        vs = slice(hh * DV_B, (hh + 1) * DV_B)
        q = q_ref[:, :, ks].reshape(rows, DK_B) * (DK_B ** -0.5)
        k = k_ref[:, :, ks].reshape(rows, DK_B)
        v = v_ref[:, :, vs].reshape(rows, DV_B)
        vb = v.astype(BF16)
        la = la_ref[:, :, ks].reshape(rows, DK_B) * LOG2_E
        la_hi = la.astype(BF16)
        la_lo = (la - la_hi.astype(F32)).astype(BF16)
        x = jnp.dot(mall_ref[...], jnp.concatenate([la_hi, la_lo], axis=1), preferred_element_type=F32)
        x = x[:, :DK_B] + x[:, DK_B:]
        g = x[0:rows]

        a = mask_ref[0] * lax.dot_general(q.astype(BF16), k.astype(BF16), nt_dims,
                                          preferred_element_type=F32)
        n_slab = 1
        for lvl in range(n_lvl):
            s = tt >> (lvl + 1)
            if s >= MIN_REF_LEVEL:
                xl = _block_ref_exponent(g, 2 * s, s - 1, s)
            else:
                xl = x[n_slab * rows:(n_slab + 1) * rows]
                n_slab += 1
            el = jnp.exp2(xl)
            p = lax.dot_general((q * el).astype(BF16), (k * el).astype(BF16), nt_dims,
                                preferred_element_type=F32)
            a = a + mask_ref[lvl + 1] * p
        o_intra = jnp.dot(a.astype(BF16), vb, preferred_element_type=F32)
        e_g = jnp.exp2(g)
        qg = q * e_g
        kd = k * jnp.exp2(_block_ref_exponent(g, tt, tt - 1, tt))

        o_parts = []
        for b in range(bs):
            rs = slice(b * tt, (b + 1) * tt)
            s = s_scr[b, hh]
            o_parts.append(o_intra[rs] + jnp.dot(qg[rs].astype(BF16), s.astype(BF16),
                                                 preferred_element_type=F32))
            e_last = e_g[(b + 1) * tt - 1:(b + 1) * tt]
            e_col = jnp.sum(jnp.where(eye, jnp.broadcast_to(e_last, (DK_B, DK_B)), 0.0),
                            axis=1, keepdims=True)
            kv = lax.dot_general(kd[rs].astype(BF16), v[rs].astype(BF16), tn_dims,
                                 preferred_element_type=F32)
            s_scr[b, hh] = e_col * s + kv
        o = o_parts[0] if bs == 1 else jnp.concatenate(o_parts, axis=0)
        yn = o * lax.rsqrt(jnp.mean(o * o, axis=-1, keepdims=True) + EPS) * g_ref[...]
        y_ref[:, vs] = (yn * _silu(r_ref[:, :, vs].reshape(rows, DV_B))).astype(BF16)
        if between:
            between[hh]()


def _gla_parts(z_head, la, g_gla, s0, layer, s_out_prev):
    n_seq, t, _ = z_head.shape
    tile = ROW_TILE if t >= ROW_TILE else 2 * ROW_TILE
    bs, tt = _row_tiling(t, tile)
    hp = H_B if bs == 1 else 1
    nt = t // tt
    assert nt == 1 or hp == H_B
    rows = n_seq * t
    mall, masks = _gla_tables(tile, tt)
    mall = jnp.asarray(mall, BF16)
    masks = jnp.asarray(masks, F32)
    zs = lambda w, c0: pl.BlockSpec((bs, tt, hp * w), lambda i, j, h=0: (i, j, c0 // hp + h))
    s_spec = pl.BlockSpec((None, bs, hp, DK_B, DV_B), lambda i, j, h=0: (layer, i, h, 0, 0))
    in_specs = [zs(DK_B, ZQ0), zs(DK_B, ZK0), zs(DV_B, ZV0), zs(DV_B, ZR0),
                pl.BlockSpec((bs, tt, hp * DK_B), lambda i, j, h=0: (i, j, h)),
                pl.BlockSpec(mall.shape, lambda *_: (0, 0)),
                pl.BlockSpec(masks.shape, lambda *_: (0, 0, 0)),
                pl.BlockSpec((None, 1, DV_B), lambda *_: (layer, 0, 0)),
                s_spec]
    args = [z_head, z_head, z_head, z_head, la, mall, masks, g_gla, s0]
    if s_out_prev is not None:
        in_specs.append(pl.BlockSpec(memory_space=pl.ANY))
        args.append(s_out_prev)
    return dict(args=args, in_specs=in_specs, aliased=s_out_prev is not None,
                out_shape=[jax.ShapeDtypeStruct((rows, W_B), BF16),
                           jax.ShapeDtypeStruct((DEPTH, n_seq, H_B, DK_B, DV_B), F32)],
                out_specs=[pl.BlockSpec((tile, hp * DV_B), lambda i, j, h=0: (i * nt + j, h)), s_spec],
                scratch=[pltpu.VMEM((bs, hp, DK_B, DV_B), F32)], grid=(n_seq // bs, nt, H_B // hp))


def _gla(*operands):
    m = _gla_parts(*operands)
    aliases = {len(m["args"]) - 1: 1} if m["aliased"] else {}
    return pl.pallas_call(
        _gla_kernel, out_shape=m["out_shape"], grid=m["grid"], in_specs=m["in_specs"],
        out_specs=m["out_specs"], scratch_shapes=m["scratch"], input_output_aliases=aliases,
        compiler_params=_cparams(("parallel", "arbitrary", "parallel")), name="gla")(*m["args"])


def _mix_gla_kernel(*refs, n_in, n_out, mix_kernel):
    (mi, gi), (mo, go) = n_in, n_out
    o0 = mi + gi
    s0 = o0 + mo + go
    mix_refs = (*refs[:mi], *refs[o0:o0 + mo], refs[s0])
    gla_refs = (*refs[mi:o0], *refs[o0 + mo:s0], refs[s0 + 1])
    for phase in PHASES:
        pieces = []
        mix_kernel(*mix_refs, phases=(phase,), defer_to=pieces)
        _gla_kernel(*gla_refs, phases=(phase,), between=pieces)


def _mix_gla(mix_operands, gla_operands):
    m = _mix_ac_parts(*mix_operands)
    g = _gla_parts(*gla_operands)
    assert m["grid"] == g["grid"][:2] and g["grid"][2] == 1
    n_in = (len(m["args"]), len(g["args"]))
    n_out = (len(m["out_shape"]), len(g["out_shape"]))
    aliases = {sum(n_in) - 1: n_out[0] + 1} if g["aliased"] else {}
    res = pl.pallas_call(
        functools.partial(_mix_gla_kernel, n_in=n_in, n_out=n_out, mix_kernel=m["kernel"]),
        out_shape=m["out_shape"] + g["out_shape"], grid=m["grid"],
        in_specs=m["in_specs"] + g["in_specs"], out_specs=m["out_specs"] + g["out_specs"],
        scratch_shapes=m["scratch"] + g["scratch"], input_output_aliases=aliases,
        compiler_params=_cparams(("parallel", "arbitrary")), name="mix_gla")(*m["args"], *g["args"])
    return res[:n_out[0]], res[n_out[0]:]


def _layer(x, layer, mod3, s_gla, buf_conv, p, wts, s_out_prev, emit_v):
    n_seq, t, d = x.shape
    seq_len = min(t, CHUNK)
    reps = ROW_TILE // seq_len
    h, la = _norm_call(x, p["g_mix"], mod3, 1, 0, gate_w=(p["w_a"], p["w_a2"], p["b_a2"]))
    z_head, wb_head = _in_proj(h, wts["w_head"], layer, A_OFF, n_seq, t)
    if wts["w_tail"] is None:
        z_tail, wb_tail = _in_proj_tail_cast(h, wts["w_head"], layer, n_seq, t)
    else:
        z_tail, wb_tail = _in_proj(h, wts["w_tail"], layer, P_TAIL, n_seq, t)
    wt = jnp.tile(p["w_s"][:, :seq_len, :seq_len], (1, reps, reps))
    bias_full = jnp.repeat(jnp.tile(p["b_s"][:, :seq_len].T, (reps, 1)), DH_A, axis=1)
    mix_ops = (z_head, z_tail, wt, bias_full, p["w_conv"], layer, buf_conv, emit_v)
    gla_ops = (z_head, la, p["g_gla"], s_gla, layer, s_out_prev)
    if t >= ROW_TILE:
        mix, (yb, s_out) = _mix_gla(mix_ops, gla_ops)
    else:
        mix = _mix_ac(*mix_ops)
        yb, s_out = _gla(*gla_ops)
    ya, yc, buf_new = mix[:3]
    vn = mix[3] if emit_v else None
    x, wb_out = _out_proj(ya, yb, yc, wts["w_out"], layer, x, mod3, 2)
    h2 = _norm_call(x, p["g_ffn"], mod3, 4, 3)
    a, wb_gate, wb_up = _swiglu(h2, wts["w_gate"], wts["w_up"], layer)
    x, wb_down = _down_proj(a, wts["w_down"], layer, x, mod3, 5)
    wb = dict(w_head=wb_head, w_tail=wb_tail, w_out=wb_out, w_gate=wb_gate, w_up=wb_up, w_down=wb_down)
    return x, s_out, buf_new, vn, wb


def kernel(x_prompt, x_sample, state_gla, state_conv, c_prompt, c_sample, g_mix, g_ffn, w_mod, b_mod,
           w_in, w_s, b_s, w_a2, b_a2, g_gla, w_conv, w_out, w_gate, w_up, w_down, g_final):
    bp = x_prompt.shape[0]
    bd = x_sample.shape[0]
    n_c = bp + bd
    mc = -(-n_c // 16) * 16
    c_all = jnp.concatenate([c_prompt, c_sample, jnp.zeros((mc - n_c, D_MODEL), F32)], axis=0)
    mod = _modulation(c_all, w_mod, b_mod)
    gf = g_final.reshape(1, D_MODEL)
    g_gla3 = g_gla.reshape(DEPTH, 1, DV_B)
    gla0 = jnp.zeros((DEPTH, bp, H_B, DK_B, DV_B), F32)
    conv0 = jnp.zeros((DEPTH, bp, CONV_W - 1, W_C), F32)

    xs, xp = x_sample, x_prompt
    gla_s = gla_p = None
    conv_s, conv_p, v_s = [], [], []
    wt_in = jnp.swapaxes(w_in, 1, 2)
    for l in range(DEPTH):
        w_a = wt_in[l, A_OFF:A_OFF + GATE_RANK, :].T
        p = dict(
            g_mix=g_mix[l].reshape(1, D_MODEL), g_ffn=g_ffn[l].reshape(1, D_MODEL),
            w_a=jnp.pad(w_a, ((0, 0), (0, LANES - GATE_RANK))),
            w_a2=jnp.pad(w_a2[l], ((0, LANES - GATE_RANK), (0, 0))).astype(BF16),
            b_a2=b_a2[l].reshape(1, KW_B),
            w_s=w_s[l], b_s=b_s[l], g_gla=g_gla3, w_conv=w_conv)
        w_f32 = dict(w_head=wt_in, w_tail=None, w_out=w_out,
                     w_gate=w_gate, w_up=w_up, w_down=w_down)
        mod_s = mod[l, bp:n_c].reshape(bd, 1, N_MOD * D_MODEL)
        mod_p = mod[l, :bp].reshape(bp, 1, N_MOD * D_MODEL)
        xs, gla_s, buf_s, vn_s, w_bf16 = _layer(xs, l, mod_s, state_gla, state_conv, p, w_f32, gla_s, True)
        xp, gla_p, buf_p, _, _ = _layer(xp, l, mod_p, gla0, conv0, p, w_bf16, gla_p, False)
        conv_s.append(buf_s)
        conv_p.append(buf_p)
        v_s.append(vn_s)
    y_s = _final_norm(xs, gf)
    y_p = _final_norm(xp, gf)
    return (y_p, y_s, gla_p, jnp.stack(conv_p), gla_s, jnp.stack(conv_s), jnp.stack(v_s))
```

```python
import functools

import numpy as np
import jax
import jax.numpy as jnp
from jax import lax
from jax.experimental import pallas as pl
from jax.experimental.pallas import tpu as pltpu

F32 = jnp.float32
BF16 = jnp.bfloat16

D_MODEL = 4096
DEPTH = 2
EPS = 1e-6
CHUNK = 128
W_A = D_MODEL // 4
G_A = 8
DH_A = W_A // G_A
W_B = D_MODEL // 2
H_B = 8
DV_B = W_B // H_B
DK_B = DV_B // 2
KW_B = H_B * DK_B
GATE_RANK = 16
GATE_TAU = 16.0
W_C = D_MODEL // 4
CONV_W = 3
D_FF = -(-8 * D_MODEL // (3 * 256)) * 256
N_MOD = 6
A_OFF = 2 * W_A + 2 * KW_B + 2 * W_B
P_TAIL = 3 * W_C

LOG2_E = float(np.log2(np.e))
LANES = 128
ROW_TILE = 128
MIN_REF_LEVEL = 4
NORM_ROWS = 512
TM = 1024
ROW_SPLIT = 2
T_AXIS = 1
PHASES = ("init", "body", "final")
VMEM_LIMIT = 56 * 1024 * 1024
DOWN_PROJ_VMEM_LIMIT = 62 * 1024 * 1024

ZQ0 = (2 * W_A) // DK_B
ZK0 = (2 * W_A + KW_B) // DK_B
ZV0 = (2 * W_A + 2 * KW_B) // DV_B
ZR0 = (2 * W_A + 2 * KW_B + W_B) // DV_B


def _cparams(sem):
    return pltpu.CompilerParams(dimension_semantics=sem, vmem_limit_bytes=VMEM_LIMIT)


def _silu(x):
    return x / (1.0 + jnp.exp(-x))


def _gelu_tanh(x):
    c = np.float32(np.sqrt(2.0 / np.pi))
    return 0.5 * x * (1.0 + jnp.tanh(c * (x + 0.044715 * (x * x * x))))


def _row_tiling(t, rows):
    if t >= rows:
        return 1, rows
    return rows // t, t


def _w_spec(w, layer, k, tn, col_map):
    if w.ndim == 2:
        return pl.BlockSpec((k, tn), lambda *g: (0, col_map(*g)))
    return pl.BlockSpec((None, k, tn), lambda *g: (layer, 0, col_map(*g)))


def _mod_kernel(c_ref, w_ref, b_ref, o_ref):
    s = _silu(c_ref[...]).astype(BF16)
    o_ref[0] = jnp.dot(s, w_ref[0].astype(BF16), preferred_element_type=F32) + b_ref[0]


def _modulation(c_all, w_mod, b_mod):
    mc = c_all.shape[0]
    n = N_MOD * D_MODEL
    tn = 512
    return pl.pallas_call(
        _mod_kernel,
        out_shape=jax.ShapeDtypeStruct((DEPTH, mc, n), F32),
        grid=(DEPTH, n // tn),
        in_specs=[
            pl.BlockSpec((mc, D_MODEL), lambda l, j: (0, 0)),
            pl.BlockSpec((1, D_MODEL, tn), lambda l, j: (l, 0, j)),
            pl.BlockSpec((1, 1, tn), lambda l, j: (l, 0, j)),
        ],
        out_specs=pl.BlockSpec((1, mc, tn), lambda l, j: (l, 0, j)),
        compiler_params=_cparams(("parallel", "parallel")),
        name="modulation",
    )(c_all, w_mod, b_mod.reshape(DEPTH, 1, n))


def _norm_mod(x, g, sc, sh):
    y = x * lax.rsqrt(jnp.mean(x * x, axis=-1, keepdims=True) + EPS) * g
    return y * (1.0 + sc) + sh


def _norm_gate_kernel(x_ref, g_ref, sc_ref, sh_ref, wa_ref, wa2_ref, ba2_ref, h_ref, la_ref):
    bs, tt, d = x_ref.shape
    h = _norm_mod(x_ref[...], g_ref[...], sc_ref[...], sh_ref[...]).reshape(bs * tt, d).astype(BF16)
    h_ref[...] = h
    a = jnp.dot(h, wa_ref[...].astype(BF16), preferred_element_type=F32)
    pre = jnp.dot(a.astype(BF16), wa2_ref[...], preferred_element_type=F32) + ba2_ref[...]
    log_sig = jnp.minimum(pre, 0.0) - jnp.log(1.0 + jnp.exp(-jnp.abs(pre)))
    la_ref[...] = (log_sig / GATE_TAU).reshape(bs, tt, KW_B)


def _norm_kernel(x_ref, g_ref, sc_ref, sh_ref, h_ref):
    bs, tt, d = x_ref.shape
    h_ref[...] = _norm_mod(x_ref[...], g_ref[...], sc_ref[...], sh_ref[...]).reshape(bs * tt, d).astype(BF16)


def _norm_call(x, g, mod3, sc_idx, sh_idx, gate_w=None):
    n_seq, t, d = x.shape
    bs, tt = _row_tiling(t, NORM_ROWS)
    rows = bs * tt
    nt = t // tt
    grid = (n_seq // bs, nt)
    x_spec = pl.BlockSpec((bs, tt, d), lambda i, j: (i, j, 0))
    g_spec = pl.BlockSpec((1, d), lambda i, j: (0, 0))
    mod_spec = lambda c: pl.BlockSpec((bs, 1, d), lambda i, j: (i, 0, c))
    h_spec = pl.BlockSpec((rows, d), lambda i, j: (i * nt + j, 0))
    h_shape = jax.ShapeDtypeStruct((n_seq * t, d), BF16)
    if gate_w is None:
        return pl.pallas_call(
            _norm_kernel, out_shape=h_shape, grid=grid,
            in_specs=[x_spec, g_spec, mod_spec(sc_idx), mod_spec(sh_idx)],
            out_specs=h_spec, compiler_params=_cparams(("parallel", "parallel")),
            name="norm_mod")(x, g, mod3, mod3)
    wa, wa2, ba2 = gate_w
    return pl.pallas_call(
        _norm_gate_kernel,
        out_shape=(h_shape, jax.ShapeDtypeStruct((n_seq, t, KW_B), F32)),
        grid=grid,
        in_specs=[x_spec, g_spec, mod_spec(sc_idx), mod_spec(sh_idx),
                  pl.BlockSpec((d, LANES), lambda i, j: (0, 0)),
                  pl.BlockSpec((LANES, KW_B), lambda i, j: (0, 0)),
                  pl.BlockSpec((1, KW_B), lambda i, j: (0, 0))],
        out_specs=(h_spec, pl.BlockSpec((bs, tt, KW_B), lambda i, j: (i, j, 0))),
        compiler_params=_cparams(("parallel", "parallel")),
        name="norm_mod_gate")(x, g, mod3, mod3, wa, wa2, ba2)


def _final_norm_kernel(x_ref, g_ref, o_ref):
    x = x_ref[...]
    o_ref[...] = x * lax.rsqrt(jnp.mean(x * x, axis=-1, keepdims=True) + EPS) * g_ref[...]


def _final_norm(x, g):
    n_seq, t, d = x.shape
    bs, tt = _row_tiling(t, NORM_ROWS)
    return pl.pallas_call(
        _final_norm_kernel, out_shape=jax.ShapeDtypeStruct(x.shape, F32),
        grid=(n_seq // bs, t // tt),
        in_specs=[pl.BlockSpec((bs, tt, d), lambda i, j: (i, j, 0)),
                  pl.BlockSpec((1, d), lambda i, j: (0, 0))],
        out_specs=pl.BlockSpec((bs, tt, d), lambda i, j: (i, j, 0)),
        compiler_params=_cparams(("parallel", "parallel")),
        name="final_norm")(x, g)


def _bf16_tile(w_ref, wb_ref):
    if wb_ref is None:
        return w_ref[...]
    wb = w_ref[...].astype(BF16)
    wb_ref[...] = wb
    return wb


def _row_parts(bs, tt):
    if bs == 1:
        p = tt // ROW_SPLIT
        return [(slice(r, r + p), slice(0, 1), slice(r, r + p)) for r in range(0, tt, p)]
    p = bs // ROW_SPLIT
    return [(slice(b * tt, (b + p) * tt), slice(b, b + p), slice(0, tt)) for b in range(0, bs, p)]


def _in_proj_kernel(h_ref, wt_ref, z_ref, wtb_ref=None):
    wt = _bf16_tile(wt_ref, wtb_ref)
    bs, tt, tn = z_ref.shape
    for rows, sb, st in _row_parts(bs, tt):
        acc = lax.dot_general(h_ref[rows, :], wt, (((1,), (1,)), ((), ())), preferred_element_type=F32)
        z_ref[sb, st, :] = acc.reshape(sb.stop - sb.start, st.stop - st.start, tn)


def _in_proj(h, wt, layer, n_cols, n_seq, t):
    rows, k = h.shape
    emit = wt.dtype != BF16
    bs, tt = _row_tiling(t, TM)
    nt = t // tt
    tn = 512 if emit else 1024
    if wt.ndim == 2:
        w_spec = pl.BlockSpec((tn, k), lambda i, j: (j, 0))
    else:
        w_spec = pl.BlockSpec((None, tn, k), lambda i, j: (layer, j, 0))
    out_shape = [jax.ShapeDtypeStruct((n_seq, t, n_cols), F32)]
    out_specs = [pl.BlockSpec((bs, tt, tn), lambda i, j: (i // nt, i % nt, j))]
    if emit:
        assert rows == TM
        out_shape.append(jax.ShapeDtypeStruct((n_cols, k), BF16))
        out_specs.append(pl.BlockSpec((tn, k), lambda i, j: (j, 0)))
    res = pl.pallas_call(
        _in_proj_kernel, out_shape=out_shape,
        grid=(rows // TM, n_cols // tn),
        in_specs=[pl.BlockSpec((TM, k), lambda i, j: (i, 0)), w_spec],
        out_specs=out_specs,
        compiler_params=_cparams(("parallel", "parallel")),
        name="in_proj_cast" if emit else "in_proj")(h, wt)
    return (res[0], res[1]) if emit else (res[0], None)


def _in_proj_tail_cast_kernel(h_ref, wa_ref, wb_ref, z_ref, wtb_ref):
    wt = jnp.concatenate([wa_ref[GATE_RANK:, :], wb_ref[...]], axis=0).astype(BF16)
    wtb_ref[...] = wt
    acc = lax.dot_general(h_ref[...], wt, (((1,), (1,)), ((), ())), preferred_element_type=F32)
    z_ref[...] = acc.reshape(z_ref.shape)


def _in_proj_tail_cast(h, wt_all, layer, n_seq, t):
    rows, k = h.shape
    assert rows == TM
    bs, tt = _row_tiling(t, TM)
    tn = 512
    return pl.pallas_call(
        _in_proj_tail_cast_kernel,
        out_shape=(jax.ShapeDtypeStruct((n_seq, t, P_TAIL), F32), jax.ShapeDtypeStruct((P_TAIL, k), BF16)),
        grid=(P_TAIL // tn,),
        in_specs=[pl.BlockSpec((TM, k), lambda j: (0, 0)),
                  pl.BlockSpec((None, tn, k), lambda j: (layer, A_OFF // tn + j, 0)),
                  pl.BlockSpec((None, GATE_RANK, k),
                               lambda j: (layer, (A_OFF + tn * (j + 1)) // GATE_RANK, 0))],
        out_specs=(pl.BlockSpec((bs, tt, tn), lambda j: (0, 0, j)),
                   pl.BlockSpec((tn, k), lambda j: (j, 0))),
        compiler_params=_cparams(("parallel",)),
        name="in_proj_tail_cast")(h, wt_all, wt_all)


def _out_proj_kernel(ya_ref, yb_ref, yc_ref, w_ref, x_ref, gt_ref, o_ref, wb_ref=None):
    if wb_ref is not None:
        wb_ref[...] = w_ref[...].astype(BF16)
        w_ref = wb_ref
    bs, tt, tn = o_ref.shape
    for rows, sb, st in _row_parts(bs, tt):
        acc = jnp.dot(ya_ref[rows, :], w_ref[0:W_A, :], preferred_element_type=F32)
        acc += jnp.dot(yb_ref[rows, :], w_ref[W_A:W_A + W_B, :], preferred_element_type=F32)
        acc += jnp.dot(yc_ref[rows, :], w_ref[W_A + W_B:, :], preferred_element_type=F32)
        acc = acc.reshape(sb.stop - sb.start, st.stop - st.start, tn)
        o_ref[sb, st, :] = x_ref[sb, st, :] + gt_ref[sb, :, :] * acc


def _out_proj(ya, yb, yc, w, layer, x, mod3, gt_idx):
    n_seq, t, d = x.shape
    rows = n_seq * t
    emit = w.dtype != BF16
    bs, tt = _row_tiling(t, TM)
    nt = t // tt
    tn = 512 if emit else 1024
    xo_spec = pl.BlockSpec((bs, tt, tn), lambda i, j: (i // nt, i % nt, j))
    out_shape = [jax.ShapeDtypeStruct(x.shape, F32)]
    out_specs = [xo_spec]
    if emit:
        assert rows == TM
        out_shape.append(jax.ShapeDtypeStruct((d, d), BF16))
        out_specs.append(pl.BlockSpec((d, tn), lambda i, j: (0, j)))
    res = pl.pallas_call(
        _out_proj_kernel, out_shape=out_shape,
        grid=(rows // TM, d // tn),
        in_specs=[pl.BlockSpec((TM, W_A), lambda i, j: (i, 0)),
                  pl.BlockSpec((TM, W_B), lambda i, j: (i, 0)),
                  pl.BlockSpec((TM, W_C), lambda i, j: (i, 0)),
                  _w_spec(w, layer, d, tn, lambda i, j: j),
                  xo_spec,
                  pl.BlockSpec((bs, 1, tn), lambda i, j: (i // nt, 0, gt_idx * (d // tn) + j))],
        out_specs=out_specs,
        compiler_params=_cparams(("parallel", "parallel")),
        name="out_proj_cast" if emit else "out_proj")(ya, yb, yc, w, x, mod3)
    return (res[0], res[1]) if emit else (res[0], None)


def _swiglu_kernel(h_ref, wg_ref, wu_ref, a_ref, wgb_ref=None, wub_ref=None):
    wg = _bf16_tile(wg_ref, wgb_ref)
    wu = _bf16_tile(wu_ref, wub_ref)
    rows = h_ref.shape[0]
    part = rows // ROW_SPLIT
    for r0 in range(0, rows, part):
        h = h_ref[r0:r0 + part, :]
        g = jnp.dot(h, wg, preferred_element_type=F32)
        u = jnp.dot(h, wu, preferred_element_type=F32)
        a_ref[r0:r0 + part, :] = (_silu(g) * u).astype(BF16)


def _swiglu(h, wg, wu, layer):
    rows, k = h.shape
    n = wg.shape[-1]
    emit = wg.dtype != BF16
    tn = 256
    tm = TM if emit or rows % (2 * TM) else 2 * TM
    out_shape = [jax.ShapeDtypeStruct((rows, n), BF16)]
    out_specs = [pl.BlockSpec((tm, tn), lambda i, j: (i, j))]
    if emit:
        assert rows == TM
        out_shape += [jax.ShapeDtypeStruct((k, n), BF16)] * 2
        out_specs += [pl.BlockSpec((k, tn), lambda i, j: (0, j))] * 2
    res = pl.pallas_call(
        _swiglu_kernel, out_shape=out_shape,
        grid=(rows // tm, n // tn),
        in_specs=[pl.BlockSpec((tm, k), lambda i, j: (i, 0)),
                  _w_spec(wg, layer, k, tn, lambda i, j: j),
                  _w_spec(wu, layer, k, tn, lambda i, j: j)],
        out_specs=out_specs,
        compiler_params=_cparams(("parallel", "parallel")),
        name="swiglu_cast" if emit else "swiglu")(h, wg, wu)
    return (res[0], res[1], res[2]) if emit else (res[0], None, None)


def _down_proj_cast_kernel(a_ref, w_ref, x_ref, gt_ref, o_ref, wb_ref, acc_ref):
    kk = pl.program_id(0)
    j = pl.program_id(1)
    part = jnp.dot(a_ref[...], _bf16_tile(w_ref, wb_ref), preferred_element_type=F32)

    @pl.when(kk == 0)
    def _():
        acc_ref[j] = part

    @pl.when(kk == 1)
    def _():
        o_ref[...] = x_ref[...] + gt_ref[...] * (acc_ref[j] + part).reshape(o_ref.shape)


def _down_proj_kernel(a_ref, w_ref, x_ref, gt_ref, o_ref):
    bs, tt, tn = o_ref.shape
    for rows, sb, st in _row_parts(bs, tt):
        acc = jnp.dot(a_ref[rows, :], w_ref[...], preferred_element_type=F32)
        acc = acc.reshape(sb.stop - sb.start, st.stop - st.start, tn)
        o_ref[sb, st, :] = x_ref[sb, st, :] + gt_ref[sb, :, :] * acc


def _down_proj(a, w, layer, x, mod3, gt_idx):
    n_seq, t, d = x.shape
    rows, k = a.shape
    bs, tt = _row_tiling(t, TM)
    nt = t // tt
    if w.dtype == BF16:
        tn = 256
        n_j = d // tn
        xo_spec = pl.BlockSpec((bs, tt, tn), lambda i, j: (i // nt, i % nt, j))
        out = pl.pallas_call(
            _down_proj_kernel, out_shape=jax.ShapeDtypeStruct(x.shape, F32),
            grid=(rows // TM, n_j),
            in_specs=[pl.BlockSpec((TM, k), lambda i, j: (i, 0)),
                      pl.BlockSpec((k, tn), lambda i, j: (0, j)),
                      xo_spec,
                      pl.BlockSpec((bs, 1, tn), lambda i, j: (i // nt, 0, gt_idx * n_j + j))],
            out_specs=xo_spec,
            compiler_params=pltpu.CompilerParams(dimension_semantics=("parallel", "parallel"),
                                                 vmem_limit_bytes=DOWN_PROJ_VMEM_LIMIT),
            name="down_proj")(a, w, x, mod3)
        return out, None
    assert rows == TM
    tn = 256
    n_j = d // tn
    tk = k // 2
    xo_spec = pl.BlockSpec((bs, tt, tn), lambda kk, j: (0, 0, j * kk))
    out, wb = pl.pallas_call(
        _down_proj_cast_kernel,
        out_shape=(jax.ShapeDtypeStruct(x.shape, F32), jax.ShapeDtypeStruct((k, d), BF16)),
        grid=(2, n_j),
        in_specs=[pl.BlockSpec((TM, tk), lambda kk, j: (0, kk), pipeline_mode=pl.Buffered(1)),
                  pl.BlockSpec((None, tk, tn), lambda kk, j: (layer, kk, j)),
                  xo_spec,
                  pl.BlockSpec((bs, 1, tn), lambda kk, j: (0, 0, gt_idx * n_j + j * kk))],
        out_specs=(xo_spec, pl.BlockSpec((tk, tn), lambda kk, j: (kk, j))),
        scratch_shapes=[pltpu.VMEM((n_j, TM, tn), F32)],
        compiler_params=_cparams(("arbitrary", "arbitrary")),
        name="down_proj_cast")(a, w, x, mod3)
    return out, wb


def _mix_ac_kernel(u_ref, v_ref, b_ref, c_ref, hc_ref, wt_ref, bias_ref, wc_ref, buf_ref,
                   ya_ref, yc_ref, bufnew_ref, *rest, seq_len, phases=PHASES, defer_to=None):
    zbuf = rest[-1]
    vn_ref = rest[0] if len(rest) == 2 else None
    bs, tt, w = u_ref.shape
    rows = bs * tt
    t = pl.program_id(T_AXIS)
    pad = 8

    if "init" in phases:
        @pl.when(t == 0)
        def _():
            zbuf[:, pad - 2:pad, :] = buf_ref[...]

    if "body" in phases:
        work = _mix_ac_work(u_ref, v_ref, b_ref, c_ref, hc_ref, wt_ref, bias_ref, wc_ref, ya_ref, yc_ref,
                            vn_ref, zbuf, pad, seq_len)
        if defer_to is None:
            for piece in work:
                piece()
        else:
            defer_to.extend(work)

    if "final" in phases:
        @pl.when(t == pl.num_programs(T_AXIS) - 1)
        def _():
            bufnew_ref[...] = zbuf[:, pad - 2:pad, :]


def _mix_ac_work(u_ref, v_ref, b_ref, c_ref, hc_ref, wt_ref, bias_ref, wc_ref, ya_ref, yc_ref, vn_ref, zbuf,
                 pad, seq_len):
    bs, tt, _ = u_ref.shape
    rows = bs * tt

    def piece(g):
        sl = slice(g * DH_A, (g + 1) * DH_A)
        zc = c_ref[:, :, sl] * hc_ref[:, :, sl]
        zbuf[:, pad:pad + tt, sl] = zc
        y = zbuf[:, pad - 2:pad - 2 + tt, sl] * wc_ref[0:1, sl]
        y = y + zbuf[:, pad - 1:pad - 1 + tt, sl] * wc_ref[1:2, sl]
        y = y + zc * wc_ref[2:3, sl]
        yc_ref[:, sl] = (b_ref[:, :, sl] * y).reshape(rows, DH_A).astype(BF16)
        zbuf[:, pad - 2:pad, sl] = zbuf[:, pad + tt - 2:pad + tt, sl]

        gu = _gelu_tanh(u_ref[:, :, sl].reshape(rows, DH_A))
        vg = _gelu_tanh(v_ref[:, :, sl].reshape(rows, DH_A))
        dv = vg - jnp.mean(vg, axis=-1, keepdims=True)
        vn = dv * lax.rsqrt(jnp.mean(dv * dv, axis=-1, keepdims=True) + EPS)
        ri = lax.broadcasted_iota(jnp.int32, (rows, rows), 0)
        ci = lax.broadcasted_iota(jnp.int32, (rows, rows), 1)
        keep = ci <= ri
        if seq_len < rows:
            keep = keep & ((ri // seq_len) == (ci // seq_len))
        wm = jnp.where(keep, wt_ref[g], 0.0).astype(BF16)
        mixed = jnp.dot(wm, vn.astype(BF16), preferred_element_type=F32) + bias_ref[:, sl]
        ya_ref[:, sl] = (gu * mixed).astype(BF16)
        if vn_ref is not None:
            vn_ref[:, :, sl] = vn.reshape(bs, tt, DH_A)

    return [functools.partial(piece, g) for g in range(G_A)]


def _mix_ac_parts(z_head, z_tail, wt, bias_full, w_conv, layer, buf, emit_v):
    n_seq, t, _ = z_head.shape
    bs, tt = _row_tiling(t, ROW_TILE)
    nt = t // tt
    rows = n_seq * t
    zspec = lambda c: pl.BlockSpec((bs, tt, W_A), lambda i, j, *_: (i, j, c))
    y_spec = pl.BlockSpec((ROW_TILE, W_A), lambda i, j, *_: (i * nt + j, 0))
    out_shape = [jax.ShapeDtypeStruct((rows, W_A), BF16),
                 jax.ShapeDtypeStruct((rows, W_C), BF16),
                 jax.ShapeDtypeStruct((n_seq, CONV_W - 1, W_C), F32)]
    out_specs = [y_spec, y_spec, pl.BlockSpec((bs, CONV_W - 1, W_C), lambda i, j, *_: (i, 0, 0))]
    if emit_v:
        out_shape.append(jax.ShapeDtypeStruct((n_seq, t, W_A), F32))
        out_specs.append(pl.BlockSpec((bs, tt, W_A), lambda i, j, *_: (i, j, 0)))
    in_specs = [zspec(0), zspec(1), zspec(0), zspec(1), zspec(2),
                pl.BlockSpec((G_A, ROW_TILE, ROW_TILE), lambda *_: (0, 0, 0)),
                pl.BlockSpec((ROW_TILE, W_A), lambda *_: (0, 0)),
                pl.BlockSpec((None, CONV_W, W_C), lambda *_: (layer, 0, 0)),
                pl.BlockSpec((None, bs, CONV_W - 1, W_C), lambda i, *_: (layer, i, 0, 0))]
    return dict(args=[z_head, z_head, z_tail, z_tail, z_tail, wt, bias_full, w_conv, buf],
                in_specs=in_specs, out_shape=out_shape, out_specs=out_specs,
                scratch=[pltpu.VMEM((bs, tt + 8, W_C), F32)], grid=(n_seq // bs, nt),
                kernel=functools.partial(_mix_ac_kernel, seq_len=min(t, CHUNK)))


def _mix_ac(*operands):
    m = _mix_ac_parts(*operands)
    return pl.pallas_call(
        m["kernel"], out_shape=m["out_shape"], grid=m["grid"], in_specs=m["in_specs"],
        out_specs=m["out_specs"], scratch_shapes=m["scratch"],
        compiler_params=_cparams(("parallel", "arbitrary")), name="mix_ac")(*m["args"])


def _gla_tables(rows, seq_len):
    i = np.arange(rows)[:, None]
    t = np.arange(rows)[None, :]
    same = (i // seq_len) == (t // seq_len)
    slabs = [same & (t <= i)]
    masks = [i == t]
    s = seq_len // 2
    while s >= 1:
        second = (i // s) % 2 == 1
        start2 = (i // s) * s
        end1 = start2 + s - 1
        if s < MIN_REF_LEVEL:
            slabs.append(np.where(second, (t >= start2) & (t <= i), (t > i) & (t <= end1)))
        masks.append(((i // (2 * s)) == (t // (2 * s))) & second & ((t // s) % 2 == 0))
        s //= 2
    return (np.concatenate(slabs, 0).astype(np.float32), np.stack(masks).astype(np.float32))


def _block_ref_exponent(g, block, ref_row, flip_from):
    rows, dk = g.shape
    gr = g.reshape(rows // block, block, dk)
    d = gr - gr[:, ref_row:ref_row + 1, :]
    pos = lax.broadcasted_iota(jnp.int32, gr.shape, 1)
    return jnp.where(pos >= flip_from, d, -d).reshape(rows, dk)


def _gla_kernel(q_ref, k_ref, v_ref, r_ref, la_ref, mall_ref, mask_ref, g_ref, s0_ref, *rest, phases=PHASES,
                between=()):
    y_ref, snew_ref, s_scr = rest[-3:]
    t = pl.program_id(T_AXIS)

    if "init" in phases:
        @pl.when(t == 0)
        def _():
            s_scr[...] = s0_ref[...]

    if "body" in phases:
        _gla_body(q_ref, k_ref, v_ref, r_ref, la_ref, mall_ref, mask_ref, g_ref, y_ref, s_scr, between)

    if "final" in phases:
        @pl.when(t == pl.num_programs(T_AXIS) - 1)
        def _():
            snew_ref[...] = s_scr[...]


def _gla_body(q_ref, k_ref, v_ref, r_ref, la_ref, mall_ref, mask_ref, g_ref, y_ref, s_scr, between):
    bs, tt, _ = q_ref.shape
    hp = s_scr.shape[1]
    assert len(between) in (0, hp)
    rows = bs * tt
    n_lvl = mask_ref.shape[0] - 1
    ri = lax.broadcasted_iota(jnp.int32, (DK_B, DK_B), 0)
    ci = lax.broadcasted_iota(jnp.int32, (DK_B, DK_B), 1)
    eye = ri == ci
    nt_dims = (((1,), (1,)), ((), ()))
    tn_dims = (((0,), (0,)), ((), ()))

    for hh in range(hp):
        ks = slice(hh * DK_B, (hh + 1) * DK_B)
        vs = slice(hh * DV_B, (hh + 1) * DV_B)
        q = q_ref[:, :, ks].reshape(rows, DK_B) * (DK_B ** -0.5)
        k = k_ref[:, :, ks].reshape(rows, DK_B)
        v = v_ref[:, :, vs].reshape(rows, DV_B)
        vb = v.astype(BF16)
        la = la_ref[:, :, ks].reshape(rows, DK_B) * LOG2_E
        la_hi = la.astype(BF16)
        la_lo = (la - la_hi.astype(F32)).astype(BF16)
        x = jnp.dot(mall_ref[...], jnp.concatenate([la_hi, la_lo], axis=1), preferred_element_type=F32)
        x = x[:, :DK_B] + x[:, DK_B:]
        g = x[0:rows]

        a = mask_ref[0] * lax.dot_general(q.astype(BF16), k.astype(BF16), nt_dims,
                                          preferred_element_type=F32)
        n_slab = 1
        for lvl in range(n_lvl):
            s = tt >> (lvl + 1)
            if s >= MIN_REF_LEVEL:
                xl = _block_ref_exponent(g, 2 * s, s - 1, s)
            else:
                xl = x[n_slab * rows:(n_slab + 1) * rows]
                n_slab += 1
            el = jnp.exp2(xl)
            p = lax.dot_general((q * el).astype(BF16), (k * el).astype(BF16), nt_dims,
                                preferred_element_type=F32)
            a = a + mask_ref[lvl + 1] * p
        o_intra = jnp.dot(a.astype(BF16), vb, preferred_element_type=F32)
        e_g = jnp.exp2(g)
        qg = q * e_g
        kd = k * jnp.exp2(_block_ref_exponent(g, tt, tt - 1, tt))

        o_parts = []
        for b in range(bs):
            rs = slice(b * tt, (b + 1) * tt)
            s = s_scr[b, hh]
            o_parts.append(o_intra[rs] + jnp.dot(qg[rs].astype(BF16), s.astype(BF16),
                                                 preferred_element_type=F32))
            e_last = e_g[(b + 1) * tt - 1:(b + 1) * tt]
            e_col = jnp.sum(jnp.where(eye, jnp.broadcast_to(e_last, (DK_B, DK_B)), 0.0),
                            axis=1, keepdims=True)
            kv = lax.dot_general(kd[rs].astype(BF16), v[rs].astype(BF16), tn_dims,
                                 preferred_element_type=F32)
            s_scr[b, hh] = e_col * s + kv
        o = o_parts[0] if bs == 1 else jnp.concatenate(o_parts, axis=0)
        yn = o * lax.rsqrt(jnp.mean(o * o, axis=-1, keepdims=True) + EPS) * g_ref[...]
        y_ref[:, vs] = (yn * _silu(r_ref[:, :, vs].reshape(rows, DV_B))).astype(BF16)
        if between:
            between[hh]()


def _gla_parts(z_head, la, g_gla, s0, layer, s_out_prev):
    n_seq, t, _ = z_head.shape
    tile = ROW_TILE if t >= ROW_TILE else 2 * ROW_TILE
    bs, tt = _row_tiling(t, tile)
    hp = H_B if bs == 1 else 1
    nt = t // tt
    assert nt == 1 or hp == H_B
    rows = n_seq * t
    mall, masks = _gla_tables(tile, tt)
    mall = jnp.asarray(mall, BF16)
    masks = jnp.asarray(masks, F32)
    zs = lambda w, c0: pl.BlockSpec((bs, tt, hp * w), lambda i, j, h=0: (i, j, c0 // hp + h))
    s_spec = pl.BlockSpec((None, bs, hp, DK_B, DV_B), lambda i, j, h=0: (layer, i, h, 0, 0))
    in_specs = [zs(DK_B, ZQ0), zs(DK_B, ZK0), zs(DV_B, ZV0), zs(DV_B, ZR0),
                pl.BlockSpec((bs, tt, hp * DK_B), lambda i, j, h=0: (i, j, h)),
                pl.BlockSpec(mall.shape, lambda *_: (0, 0)),
                pl.BlockSpec(masks.shape, lambda *_: (0, 0, 0)),
                pl.BlockSpec((None, 1, DV_B), lambda *_: (layer, 0, 0)),
                s_spec]
    args = [z_head, z_head, z_head, z_head, la, mall, masks, g_gla, s0]
    if s_out_prev is not None:
        in_specs.append(pl.BlockSpec(memory_space=pl.ANY))
        args.append(s_out_prev)
    return dict(args=args, in_specs=in_specs, aliased=s_out_prev is not None,
                out_shape=[jax.ShapeDtypeStruct((rows, W_B), BF16),
                           jax.ShapeDtypeStruct((DEPTH, n_seq, H_B, DK_B, DV_B), F32)],
                out_specs=[pl.BlockSpec((tile, hp * DV_B), lambda i, j, h=0: (i * nt + j, h)), s_spec],
                scratch=[pltpu.VMEM((bs, hp, DK_B, DV_B), F32)], grid=(n_seq // bs, nt, H_B // hp))


def _gla(*operands):
    m = _gla_parts(*operands)
    aliases = {len(m["args"]) - 1: 1} if m["aliased"] else {}
    return pl.pallas_call(
        _gla_kernel, out_shape=m["out_shape"], grid=m["grid"], in_specs=m["in_specs"],
        out_specs=m["out_specs"], scratch_shapes=m["scratch"], input_output_aliases=aliases,
        compiler_params=_cparams(("parallel", "arbitrary", "parallel")), name="gla")(*m["args"])


def _mix_gla_kernel(*refs, n_in, n_out, mix_kernel):
    (mi, gi), (mo, go) = n_in, n_out
    o0 = mi + gi
    s0 = o0 + mo + go
    mix_refs = (*refs[:mi], *refs[o0:o0 + mo], refs[s0])
    gla_refs = (*refs[mi:o0], *refs[o0 + mo:s0], refs[s0 + 1])
    for phase in PHASES:
        pieces = []
        mix_kernel(*mix_refs, phases=(phase,), defer_to=pieces)
        _gla_kernel(*gla_refs, phases=(phase,), between=pieces)


def _mix_gla(mix_operands, gla_operands):
    m = _mix_ac_parts(*mix_operands)
    g = _gla_parts(*gla_operands)
    assert m["grid"] == g["grid"][:2] and g["grid"][2] == 1
    n_in = (len(m["args"]), len(g["args"]))
    n_out = (len(m["out_shape"]), len(g["out_shape"]))
    aliases = {sum(n_in) - 1: n_out[0] + 1} if g["aliased"] else {}
    res = pl.pallas_call(
        functools.partial(_mix_gla_kernel, n_in=n_in, n_out=n_out, mix_kernel=m["kernel"]),
        out_shape=m["out_shape"] + g["out_shape"], grid=m["grid"],
        in_specs=m["in_specs"] + g["in_specs"], out_specs=m["out_specs"] + g["out_specs"],
        scratch_shapes=m["scratch"] + g["scratch"], input_output_aliases=aliases,
        compiler_params=_cparams(("parallel", "arbitrary")), name="mix_gla")(*m["args"], *g["args"])
    return res[:n_out[0]], res[n_out[0]:]


def _layer(x, layer, mod3, s_gla, buf_conv, p, wts, s_out_prev, emit_v):
    n_seq, t, d = x.shape
    seq_len = min(t, CHUNK)
    reps = ROW_TILE // seq_len
    h, la = _norm_call(x, p["g_mix"], mod3, 1, 0, gate_w=(p["w_a"], p["w_a2"], p["b_a2"]))
    z_head, wb_head = _in_proj(h, wts["w_head"], layer, A_OFF, n_seq, t)
    if wts["w_tail"] is None:
        z_tail, wb_tail = _in_proj_tail_cast(h, wts["w_head"], layer, n_seq, t)
    else:
        z_tail, wb_tail = _in_proj(h, wts["w_tail"], layer, P_TAIL, n_seq, t)
    wt = jnp.tile(p["w_s"][:, :seq_len, :seq_len], (1, reps, reps))
    bias_full = jnp.repeat(jnp.tile(p["b_s"][:, :seq_len].T, (reps, 1)), DH_A, axis=1)
    mix_ops = (z_head, z_tail, wt, bias_full, p["w_conv"], layer, buf_conv, emit_v)
    gla_ops = (z_head, la, p["g_gla"], s_gla, layer, s_out_prev)
    if t >= ROW_TILE:
        mix, (yb, s_out) = _mix_gla(mix_ops, gla_ops)
    else:
        mix = _mix_ac(*mix_ops)
        yb, s_out = _gla(*gla_ops)
    ya, yc, buf_new = mix[:3]
    vn = mix[3] if emit_v else None
    x, wb_out = _out_proj(ya, yb, yc, wts["w_out"], layer, x, mod3, 2)
    h2 = _norm_call(x, p["g_ffn"], mod3, 4, 3)
    a, wb_gate, wb_up = _swiglu(h2, wts["w_gate"], wts["w_up"], layer)
    x, wb_down = _down_proj(a, wts["w_down"], layer, x, mod3, 5)
    wb = dict(w_head=wb_head, w_tail=wb_tail, w_out=wb_out, w_gate=wb_gate, w_up=wb_up, w_down=wb_down)
    return x, s_out, buf_new, vn, wb


def kernel(x_prompt, x_sample, state_gla, state_conv, c_prompt, c_sample, g_mix, g_ffn, w_mod, b_mod,
           w_in, w_s, b_s, w_a2, b_a2, g_gla, w_conv, w_out, w_gate, w_up, w_down, g_final):
    bp = x_prompt.shape[0]
    bd = x_sample.shape[0]
    n_c = bp + bd
    mc = -(-n_c // 16) * 16
    c_all = jnp.concatenate([c_prompt, c_sample, jnp.zeros((mc - n_c, D_MODEL), F32)], axis=0)
    mod = _modulation(c_all, w_mod, b_mod)
    gf = g_final.reshape(1, D_MODEL)
    g_gla3 = g_gla.reshape(DEPTH, 1, DV_B)
    gla0 = jnp.zeros((DEPTH, bp, H_B, DK_B, DV_B), F32)
    conv0 = jnp.zeros((DEPTH, bp, CONV_W - 1, W_C), F32)

    xs, xp = x_sample, x_prompt
    gla_s = gla_p = None
    conv_s, conv_p, v_s = [], [], []
    wt_in = jnp.swapaxes(w_in, 1, 2)
    for l in range(DEPTH):
        w_a = wt_in[l, A_OFF:A_OFF + GATE_RANK, :].T
        p = dict(
            g_mix=g_mix[l].reshape(1, D_MODEL), g_ffn=g_ffn[l].reshape(1, D_MODEL),
            w_a=jnp.pad(w_a, ((0, 0), (0, LANES - GATE_RANK))),
            w_a2=jnp.pad(w_a2[l], ((0, LANES - GATE_RANK), (0, 0))).astype(BF16),
            b_a2=b_a2[l].reshape(1, KW_B),
            w_s=w_s[l], b_s=b_s[l], g_gla=g_gla3, w_conv=w_conv)
        w_f32 = dict(w_head=wt_in, w_tail=None, w_out=w_out,
                     w_gate=w_gate, w_up=w_up, w_down=w_down)
        mod_s = mod[l, bp:n_c].reshape(bd, 1, N_MOD * D_MODEL)
        mod_p = mod[l, :bp].reshape(bp, 1, N_MOD * D_MODEL)
        xs, gla_s, buf_s, vn_s, w_bf16 = _layer(xs, l, mod_s, state_gla, state_conv, p, w_f32, gla_s, True)
        xp, gla_p, buf_p, _, _ = _layer(xp, l, mod_p, gla0, conv0, p, w_bf16, gla_p, False)
        conv_s.append(buf_s)
        conv_p.append(buf_p)
        v_s.append(vn_s)
    y_s = _final_norm(xs, gf)
    y_p = _final_norm(xp, gf)
    return (y_p, y_s, gla_p, jnp.stack(conv_p), gla_s, jnp.stack(conv_s), jnp.stack(v_s))
```

```python
import functools

import numpy as np
import jax
import jax.numpy as jnp
from jax import lax
from jax.experimental import pallas as pl
from jax.experimental.pallas import tpu as pltpu

F32 = jnp.float32
BF16 = jnp.bfloat16

D_MODEL = 4096
DEPTH = 2
EPS = 1e-6
CHUNK = 128
W_A = D_MODEL // 4
G_A = 8
DH_A = W_A // G_A
W_B = D_MODEL // 2
H_B = 8
DV_B = W_B // H_B
DK_B = DV_B // 2
KW_B = H_B * DK_B
GATE_RANK = 16
GATE_TAU = 16.0
W_C = D_MODEL // 4
CONV_W = 3
D_FF = -(-8 * D_MODEL // (3 * 256)) * 256
N_MOD = 6
A_OFF = 2 * W_A + 2 * KW_B + 2 * W_B
P_TAIL = 3 * W_C

LOG2_E = float(np.log2(np.e))
LANES = 128
ROW_TILE = 128
MIN_REF_LEVEL = 4
NORM_ROWS = 512
TM = 1024
ROW_SPLIT = 4
T_AXIS = 1
PHASES = ("init", "body", "final")
VMEM_LIMIT = 56 * 1024 * 1024
DOWN_PROJ_VMEM_LIMIT = 62 * 1024 * 1024

ZQ0 = (2 * W_A) // DK_B
ZK0 = (2 * W_A + KW_B) // DK_B
ZV0 = (2 * W_A + 2 * KW_B) // DV_B
ZR0 = (2 * W_A + 2 * KW_B + W_B) // DV_B


def _cparams(sem):
    return pltpu.CompilerParams(dimension_semantics=sem, vmem_limit_bytes=VMEM_LIMIT)


def _silu(x):
    return x / (1.0 + jnp.exp(-x))


def _gelu_tanh(x):
    c = np.float32(np.sqrt(2.0 / np.pi))
    return 0.5 * x * (1.0 + jnp.tanh(c * (x + 0.044715 * (x * x * x))))


def _row_tiling(t, rows):
    if t >= rows:
        return 1, rows
    return rows // t, t


def _w_spec(w, layer, k, tn, col_map):
    if w.ndim == 2:
        return pl.BlockSpec((k, tn), lambda *g: (0, col_map(*g)))
    return pl.BlockSpec((None, k, tn), lambda *g: (layer, 0, col_map(*g)))


def _mod_kernel(c_ref, w_ref, b_ref, o_ref):
    s = _silu(c_ref[...]).astype(BF16)
    o_ref[0] = jnp.dot(s, w_ref[0].astype(BF16), preferred_element_type=F32) + b_ref[0]


def _modulation(c_all, w_mod, b_mod):
    mc = c_all.shape[0]
    n = N_MOD * D_MODEL
    tn = 1024
    return pl.pallas_call(
        _mod_kernel,
        out_shape=jax.ShapeDtypeStruct((DEPTH, mc, n), F32),
        grid=(DEPTH, n // tn),
        in_specs=[
            pl.BlockSpec((mc, D_MODEL), lambda l, j: (0, 0)),
            pl.BlockSpec((1, D_MODEL, tn), lambda l, j: (l, 0, j)),
            pl.BlockSpec((1, 1, tn), lambda l, j: (l, 0, j)),
        ],
        out_specs=pl.BlockSpec((1, mc, tn), lambda l, j: (l, 0, j)),
        compiler_params=_cparams(("parallel", "parallel")),
        name="modulation",
    )(c_all, w_mod, b_mod.reshape(DEPTH, 1, n))


def _norm_mod(x, g, sc, sh):
    r = lax.rsqrt(jnp.mean(x * x, axis=-1, keepdims=True) + EPS)
    return (x * r) * (g * (1.0 + sc)) + sh


def _norm_gate_kernel(x_ref, g_ref, sc_ref, sh_ref, wa_ref, wa2_ref, ba2_ref, h_ref, la_ref):
    bs, tt, d = x_ref.shape
    h = _norm_mod(x_ref[...], g_ref[...], sc_ref[...], sh_ref[...]).reshape(bs * tt, d).astype(BF16)
    h_ref[...] = h
    a = jnp.dot(h, wa_ref[...].astype(BF16), preferred_element_type=F32)
    pre = jnp.dot(a.astype(BF16), wa2_ref[...], preferred_element_type=F32) + ba2_ref[...]
    log_sig = jnp.minimum(pre, 0.0) - jnp.log(1.0 + jnp.exp(-jnp.abs(pre)))
    la_ref[...] = (log_sig / GATE_TAU).reshape(bs, tt, KW_B)


def _norm_kernel(x_ref, g_ref, sc_ref, sh_ref, h_ref):
    bs, tt, d = x_ref.shape
    h_ref[...] = _norm_mod(x_ref[...], g_ref[...], sc_ref[...], sh_ref[...]).reshape(bs * tt, d).astype(BF16)


def _norm_call(x, g, mod3, sc_idx, sh_idx, gate_w=None):
    n_seq, t, d = x.shape
    bs, tt = _row_tiling(t, NORM_ROWS if t >= NORM_ROWS else NORM_ROWS // 2)
    rows = bs * tt
    nt = t // tt
    grid = (n_seq // bs, nt)
    x_spec = pl.BlockSpec((bs, tt, d), lambda i, j: (i, j, 0))
    g_spec = pl.BlockSpec((1, d), lambda i, j: (0, 0))
    mod_spec = lambda c: pl.BlockSpec((bs, 1, d), lambda i, j: (i, 0, c))
    h_spec = pl.BlockSpec((rows, d), lambda i, j: (i * nt + j, 0))
    h_shape = jax.ShapeDtypeStruct((n_seq * t, d), BF16)
    if gate_w is None:
        return pl.pallas_call(
            _norm_kernel, out_shape=h_shape, grid=grid,
            in_specs=[x_spec, g_spec, mod_spec(sc_idx), mod_spec(sh_idx)],
            out_specs=h_spec, compiler_params=_cparams(("parallel", "parallel")),
            name="norm_mod")(x, g, mod3, mod3)
    wa, wa2, ba2 = gate_w
    return pl.pallas_call(
        _norm_gate_kernel,
        out_shape=(h_shape, jax.ShapeDtypeStruct((n_seq, t, KW_B), F32)),
        grid=grid,
        in_specs=[x_spec, g_spec, mod_spec(sc_idx), mod_spec(sh_idx),
                  pl.BlockSpec((d, LANES), lambda i, j: (0, 0)),
                  pl.BlockSpec((LANES, KW_B), lambda i, j: (0, 0)),
                  pl.BlockSpec((1, KW_B), lambda i, j: (0, 0))],
        out_specs=(h_spec, pl.BlockSpec((bs, tt, KW_B), lambda i, j: (i, j, 0))),
        compiler_params=_cparams(("parallel", "parallel")),
        name="norm_mod_gate")(x, g, mod3, mod3, wa, wa2, ba2)


def _final_norm_kernel(x_ref, g_ref, o_ref):
    x = x_ref[...]
    o_ref[...] = x * lax.rsqrt(jnp.mean(x * x, axis=-1, keepdims=True) + EPS) * g_ref[...]


def _final_norm(x, g):
    n_seq, t, d = x.shape
    bs, tt = _row_tiling(t, NORM_ROWS if t >= NORM_ROWS else NORM_ROWS // 2)
    return pl.pallas_call(
        _final_norm_kernel, out_shape=jax.ShapeDtypeStruct(x.shape, F32),
        grid=(n_seq // bs, t // tt),
        in_specs=[pl.BlockSpec((bs, tt, d), lambda i, j: (i, j, 0)),
                  pl.BlockSpec((1, d), lambda i, j: (0, 0))],
        out_specs=pl.BlockSpec((bs, tt, d), lambda i, j: (i, j, 0)),
        compiler_params=_cparams(("parallel", "parallel")),
        name="final_norm")(x, g)


def _bf16_tile(w_ref, wb_ref):
    if wb_ref is None:
        return w_ref[...]
    wb = w_ref[...].astype(BF16)
    wb_ref[...] = wb
    return wb


def _in_proj_kernel(h_ref, wt_ref, z_ref, wtb_ref=None):
    acc = lax.dot_general(h_ref[...], _bf16_tile(wt_ref, wtb_ref), (((1,), (1,)), ((), ())),
                          preferred_element_type=F32)
    z_ref[...] = acc.reshape(z_ref.shape)


def _in_proj(h, wt, layer, n_cols, n_seq, t):
    rows, k = h.shape
    emit = wt.dtype != BF16
    bs, tt = _row_tiling(t, TM)
    nt = t // tt
    tn = 512 if emit else 1024
    if wt.ndim == 2:
        w_spec = pl.BlockSpec((tn, k), lambda i, j: (j, 0))
    else:
        w_spec = pl.BlockSpec((None, tn, k), lambda i, j: (layer, j, 0))
    out_shape = [jax.ShapeDtypeStruct((n_seq, t, n_cols), F32)]
    out_specs = [pl.BlockSpec((bs, tt, tn), lambda i, j: (i // nt, i % nt, j))]
    if emit:
        assert rows == TM
        out_shape.append(jax.ShapeDtypeStruct((n_cols, k), BF16))
        out_specs.append(pl.BlockSpec((tn, k), lambda i, j: (j, 0)))
    res = pl.pallas_call(
        _in_proj_kernel, out_shape=out_shape,
        grid=(rows // TM, n_cols // tn),
        in_specs=[pl.BlockSpec((TM, k), lambda i, j: (i, 0)), w_spec],
        out_specs=out_specs,
        compiler_params=_cparams(("parallel", "parallel")),
        name="in_proj_cast" if emit else "in_proj")(h, wt)
    return (res[0], res[1]) if emit else (res[0], None)


def _in_proj_tail_cast_kernel(h_ref, wa_ref, wb_ref, z_ref, wtb_ref):
    wt = jnp.concatenate([wa_ref[GATE_RANK:, :], wb_ref[...]], axis=0).astype(BF16)
    wtb_ref[...] = wt
    acc = lax.dot_general(h_ref[...], wt, (((1,), (1,)), ((), ())), preferred_element_type=F32)
    z_ref[...] = acc.reshape(z_ref.shape)


def _in_proj_tail_cast(h, wt_all, layer, n_seq, t):
    rows, k = h.shape
    assert rows == TM
    bs, tt = _row_tiling(t, TM)
    tn = 512
    return pl.pallas_call(
        _in_proj_tail_cast_kernel,
        out_shape=(jax.ShapeDtypeStruct((n_seq, t, P_TAIL), F32), jax.ShapeDtypeStruct((P_TAIL, k), BF16)),
        grid=(P_TAIL // tn,),
        in_specs=[pl.BlockSpec((TM, k), lambda j: (0, 0)),
                  pl.BlockSpec((None, tn, k), lambda j: (layer, A_OFF // tn + j, 0)),
                  pl.BlockSpec((None, GATE_RANK, k),
                               lambda j: (layer, (A_OFF + tn * (j + 1)) // GATE_RANK, 0))],
        out_specs=(pl.BlockSpec((bs, tt, tn), lambda j: (0, 0, j)),
                   pl.BlockSpec((tn, k), lambda j: (j, 0))),
        compiler_params=_cparams(("parallel",)),
        name="in_proj_tail_cast")(h, wt_all, wt_all)


def _out_proj_kernel(ya_ref, yb_ref, yc_ref, w_ref, x_ref, gt_ref, o_ref, wb_ref=None):
    if wb_ref is not None:
        wb_ref[...] = w_ref[...].astype(BF16)
        w_ref = wb_ref
    acc = jnp.dot(ya_ref[...], w_ref[0:W_A, :], preferred_element_type=F32)
    acc += jnp.dot(yb_ref[...], w_ref[W_A:W_A + W_B, :], preferred_element_type=F32)
    acc += jnp.dot(yc_ref[...], w_ref[W_A + W_B:, :], preferred_element_type=F32)
    o_ref[...] = x_ref[...] + gt_ref[...] * acc.reshape(o_ref.shape)


def _out_proj(ya, yb, yc, w, layer, x, mod3, gt_idx):
    n_seq, t, d = x.shape
    rows = n_seq * t
    emit = w.dtype != BF16
    bs, tt = _row_tiling(t, TM)
    nt = t // tt
    tn = 512 if emit else 1024
    xo_spec = pl.BlockSpec((bs, tt, tn), lambda i, j: (i // nt, i % nt, j))
    out_shape = [jax.ShapeDtypeStruct(x.shape, F32)]
    out_specs = [xo_spec]
    if emit:
        assert rows == TM
        out_shape.append(jax.ShapeDtypeStruct((d, d), BF16))
        out_specs.append(pl.BlockSpec((d, tn), lambda i, j: (0, j)))
    res = pl.pallas_call(
        _out_proj_kernel, out_shape=out_shape,
        grid=(rows // TM, d // tn),
        in_specs=[pl.BlockSpec((TM, W_A), lambda i, j: (i, 0)),
                  pl.BlockSpec((TM, W_B), lambda i, j: (i, 0)),
                  pl.BlockSpec((TM, W_C), lambda i, j: (i, 0)),
                  _w_spec(w, layer, d, tn, lambda i, j: j),
                  xo_spec,
                  pl.BlockSpec((bs, 1, tn), lambda i, j: (i // nt, 0, gt_idx * (d // tn) + j))],
        out_specs=out_specs,
        compiler_params=_cparams(("parallel", "parallel")),
        name="out_proj_cast" if emit else "out_proj")(ya, yb, yc, w, x, mod3)
    return (res[0], res[1]) if emit else (res[0], None)


def _swiglu_kernel(h_ref, wg_ref, wu_ref, a_ref, wgb_ref=None, wub_ref=None):
    wg = _bf16_tile(wg_ref, wgb_ref)
    wu = _bf16_tile(wu_ref, wub_ref)
    rows = h_ref.shape[0]
    part = rows // ROW_SPLIT
    for r0 in range(0, rows, part):
        h = h_ref[r0:r0 + part, :]
        g = jnp.dot(h, wg, preferred_element_type=F32)
        u = jnp.dot(h, wu, preferred_element_type=F32)
        a_ref[r0:r0 + part, :] = (_silu(g) * u).astype(BF16)


def _swiglu(h, wg, wu, layer):
    rows, k = h.shape
    n = wg.shape[-1]
    emit = wg.dtype != BF16
    tn = 256
    tm = TM if emit or rows % (2 * TM) else 2 * TM
    out_shape = [jax.ShapeDtypeStruct((rows, n), BF16)]
    out_specs = [pl.BlockSpec((tm, tn), lambda i, j: (i, j))]
    if emit:
        assert rows == TM
        out_shape += [jax.ShapeDtypeStruct((k, n), BF16)] * 2
        out_specs += [pl.BlockSpec((k, tn), lambda i, j: (0, j))] * 2
    res = pl.pallas_call(
        _swiglu_kernel, out_shape=out_shape,
        grid=(rows // tm, n // tn),
        in_specs=[pl.BlockSpec((tm, k), lambda i, j: (i, 0)),
                  _w_spec(wg, layer, k, tn, lambda i, j: j),
                  _w_spec(wu, layer, k, tn, lambda i, j: j)],
        out_specs=out_specs,
        compiler_params=_cparams(("parallel", "parallel")),
        name="swiglu_cast" if emit else "swiglu")(h, wg, wu)
    return (res[0], res[1], res[2]) if emit else (res[0], None, None)


def _down_proj_cast_kernel(a_ref, w_ref, x_ref, gt_ref, o_ref, wb_ref, acc_ref):
    kk = pl.program_id(0)
    j = pl.program_id(1)
    part = jnp.dot(a_ref[...], _bf16_tile(w_ref, wb_ref), preferred_element_type=F32)

    @pl.when(kk == 0)
    def _():
        acc_ref[j] = part

    @pl.when(kk == 1)
    def _():
        o_ref[...] = x_ref[...] + gt_ref[...] * (acc_ref[j] + part).reshape(o_ref.shape)


def _down_proj_kernel(a_ref, w_ref, x_ref, gt_ref, o_ref):
    acc = jnp.dot(a_ref[...], w_ref[...], preferred_element_type=F32)
    o_ref[...] = x_ref[...] + gt_ref[...] * acc.reshape(o_ref.shape)


def _down_proj(a, w, layer, x, mod3, gt_idx):
    n_seq, t, d = x.shape
    rows, k = a.shape
    bs, tt = _row_tiling(t, TM)
    nt = t // tt
    if w.dtype == BF16:
        tn = 256
        n_j = d // tn
        xo_spec = pl.BlockSpec((bs, tt, tn), lambda i, j: (i // nt, i % nt, j))
        out = pl.pallas_call(
            _down_proj_kernel, out_shape=jax.ShapeDtypeStruct(x.shape, F32),
            grid=(rows // TM, n_j),
            in_specs=[pl.BlockSpec((TM, k), lambda i, j: (i, 0)),
                      pl.BlockSpec((k, tn), lambda i, j: (0, j)),
                      xo_spec,
                      pl.BlockSpec((bs, 1, tn), lambda i, j: (i // nt, 0, gt_idx * n_j + j))],
            out_specs=xo_spec,
            compiler_params=pltpu.CompilerParams(dimension_semantics=("parallel", "parallel"),
                                                 vmem_limit_bytes=DOWN_PROJ_VMEM_LIMIT),
            name="down_proj")(a, w, x, mod3)
        return out, None
    assert rows == TM
    tn = 256
    n_j = d // tn
    tk = k // 2
    xo_spec = pl.BlockSpec((bs, tt, tn), lambda kk, j: (0, 0, j * kk))
    out, wb = pl.pallas_call(
        _down_proj_cast_kernel,
        out_shape=(jax.ShapeDtypeStruct(x.shape, F32), jax.ShapeDtypeStruct((k, d), BF16)),
        grid=(2, n_j),
        in_specs=[pl.BlockSpec((TM, tk), lambda kk, j: (0, kk), pipeline_mode=pl.Buffered(1)),
                  pl.BlockSpec((None, tk, tn), lambda kk, j: (layer, kk, j)),
                  xo_spec,
                  pl.BlockSpec((bs, 1, tn), lambda kk, j: (0, 0, gt_idx * n_j + j * kk))],
        out_specs=(xo_spec, pl.BlockSpec((tk, tn), lambda kk, j: (kk, j))),
        scratch_shapes=[pltpu.VMEM((n_j, TM, tn), F32)],
        compiler_params=_cparams(("arbitrary", "arbitrary")),
        name="down_proj_cast")(a, w, x, mod3)
    return out, wb


def _mix_ac_kernel(u_ref, v_ref, b_ref, c_ref, hc_ref, wt_ref, bias_ref, wc_ref, buf_ref,
                   ya_ref, yc_ref, bufnew_ref, *rest, seq_len, phases=PHASES, defer_to=None):
    zbuf = rest[-1]
    vn_ref = rest[0] if len(rest) == 2 else None
    bs, tt, w = u_ref.shape
    rows = bs * tt
    t = pl.program_id(T_AXIS)
    pad = 8

    if "init" in phases:
        @pl.when(t == 0)
        def _():
            zbuf[:, pad - 2:pad, :] = buf_ref[...]

    if "body" in phases:
        work = _mix_ac_work(u_ref, v_ref, b_ref, c_ref, hc_ref, wt_ref, bias_ref, wc_ref, ya_ref, yc_ref,
                            vn_ref, zbuf, pad, seq_len)
        if defer_to is None:
            for piece in work:
                piece()
        else:
            defer_to.extend(work)

    if "final" in phases:
        @pl.when(t == pl.num_programs(T_AXIS) - 1)
        def _():
            bufnew_ref[...] = zbuf[:, pad - 2:pad, :]


def _mix_ac_work(u_ref, v_ref, b_ref, c_ref, hc_ref, wt_ref, bias_ref, wc_ref, ya_ref, yc_ref, vn_ref, zbuf,
                 pad, seq_len):
    bs, tt, _ = u_ref.shape
    rows = bs * tt

    def piece(g):
        sl = slice(g * DH_A, (g + 1) * DH_A)
        zc = c_ref[:, :, sl] * hc_ref[:, :, sl]
        zbuf[:, pad:pad + tt, sl] = zc
        y = zbuf[:, pad - 2:pad - 2 + tt, sl] * wc_ref[0:1, sl]
        y = y + zbuf[:, pad - 1:pad - 1 + tt, sl] * wc_ref[1:2, sl]
        y = y + zc * wc_ref[2:3, sl]
        yc_ref[:, sl] = (b_ref[:, :, sl] * y).reshape(rows, DH_A).astype(BF16)
        zbuf[:, pad - 2:pad, sl] = zbuf[:, pad + tt - 2:pad + tt, sl]

        gu = _gelu_tanh(u_ref[:, :, sl].reshape(rows, DH_A))
        vg = _gelu_tanh(v_ref[:, :, sl].reshape(rows, DH_A))
        dv = vg - jnp.mean(vg, axis=-1, keepdims=True)
        vn = dv * lax.rsqrt(jnp.mean(dv * dv, axis=-1, keepdims=True) + EPS)
        ri = lax.broadcasted_iota(jnp.int32, (rows, rows), 0)
        ci = lax.broadcasted_iota(jnp.int32, (rows, rows), 1)
        keep = ci <= ri
        if seq_len < rows:
            keep = keep & ((ri // seq_len) == (ci // seq_len))
        wm = jnp.where(keep, wt_ref[g], 0.0).astype(BF16)
        mixed = jnp.dot(wm, vn.astype(BF16), preferred_element_type=F32) + bias_ref[:, sl]
        ya_ref[:, sl] = (gu * mixed).astype(BF16)
        if vn_ref is not None:
            vn_ref[:, :, sl] = vn.reshape(bs, tt, DH_A)

    return [functools.partial(piece, g) for g in range(G_A)]


def _mix_ac_parts(z_head, z_tail, wt, bias_full, w_conv, layer, buf, emit_v):
    n_seq, t, _ = z_head.shape
    bs, tt = _row_tiling(t, ROW_TILE)
    nt = t // tt
    rows = n_seq * t
    zspec = lambda c: pl.BlockSpec((bs, tt, W_A), lambda i, j, *_: (i, j, c))
    y_spec = pl.BlockSpec((ROW_TILE, W_A), lambda i, j, *_: (i * nt + j, 0))
    out_shape = [jax.ShapeDtypeStruct((rows, W_A), BF16),
                 jax.ShapeDtypeStruct((rows, W_C), BF16),
                 jax.ShapeDtypeStruct((n_seq, CONV_W - 1, W_C), F32)]
    out_specs = [y_spec, y_spec, pl.BlockSpec((bs, CONV_W - 1, W_C), lambda i, j, *_: (i, 0, 0))]
    if emit_v:
        out_shape.append(jax.ShapeDtypeStruct((n_seq, t, W_A), F32))
        out_specs.append(pl.BlockSpec((bs, tt, W_A), lambda i, j, *_: (i, j, 0)))
    in_specs = [zspec(0), zspec(1), zspec(0), zspec(1), zspec(2),
                pl.BlockSpec((G_A, ROW_TILE, ROW_TILE), lambda *_: (0, 0, 0)),
                pl.BlockSpec((ROW_TILE, W_A), lambda *_: (0, 0)),
                pl.BlockSpec((None, CONV_W, W_C), lambda *_: (layer, 0, 0)),
                pl.BlockSpec((None, bs, CONV_W - 1, W_C), lambda i, *_: (layer, i, 0, 0))]
    return dict(args=[z_head, z_head, z_tail, z_tail, z_tail, wt, bias_full, w_conv, buf],
                in_specs=in_specs, out_shape=out_shape, out_specs=out_specs,
                scratch=[pltpu.VMEM((bs, tt + 8, W_C), F32)], grid=(n_seq // bs, nt),
                kernel=functools.partial(_mix_ac_kernel, seq_len=min(t, CHUNK)))


def _mix_ac(*operands):
    m = _mix_ac_parts(*operands)
    return pl.pallas_call(
        m["kernel"], out_shape=m["out_shape"], grid=m["grid"], in_specs=m["in_specs"],
        out_specs=m["out_specs"], scratch_shapes=m["scratch"],
        compiler_params=_cparams(("parallel", "arbitrary")), name="mix_ac")(*m["args"])


def _gla_tables(rows, seq_len):
    i = np.arange(rows)[:, None]
    t = np.arange(rows)[None, :]
    same = (i // seq_len) == (t // seq_len)
    slabs = [same & (t <= i)]
    masks = [i == t]
    s = seq_len // 2
    while s >= 1:
        second = (i // s) % 2 == 1
        start2 = (i // s) * s
        end1 = start2 + s - 1
        if s < MIN_REF_LEVEL:
            slabs.append(np.where(second, (t >= start2) & (t <= i), (t > i) & (t <= end1)))
        masks.append(((i // (2 * s)) == (t // (2 * s))) & second & ((t // s) % 2 == 0))
        s //= 2
    return (np.concatenate(slabs, 0).astype(np.float32), np.stack(masks).astype(np.float32))


def _block_ref_exponent(g, block, ref_row, flip_from):
    rows, dk = g.shape
    gr = g.reshape(rows // block, block, dk)
    d = gr - gr[:, ref_row:ref_row + 1, :]
    pos = lax.broadcasted_iota(jnp.int32, gr.shape, 1)
    return jnp.where(pos >= flip_from, d, -d).reshape(rows, dk)


def _gla_kernel(q_ref, k_ref, v_ref, r_ref, la_ref, mall_ref, mask_ref, g_ref, s0_ref, *rest, phases=PHASES,
                between=()):
    y_ref, snew_ref, s_scr = rest[-3:]
    t = pl.program_id(T_AXIS)

    if "init" in phases:
        @pl.when(t == 0)
        def _():
            s_scr[...] = s0_ref[...]

    if "body" in phases:
        _gla_body(q_ref, k_ref, v_ref, r_ref, la_ref, mall_ref, mask_ref, g_ref, y_ref, s_scr, between)

    if "final" in phases:
        @pl.when(t == pl.num_programs(T_AXIS) - 1)
        def _():
            snew_ref[...] = s_scr[...]


def _gla_body(q_ref, k_ref, v_ref, r_ref, la_ref, mall_ref, mask_ref, g_ref, y_ref, s_scr, between):
    bs, tt, _ = q_ref.shape
    hp = s_scr.shape[1]
    assert len(between) in (0, hp)
    rows = bs * tt
    n_lvl = mask_ref.shape[0] - 1
    ri = lax.broadcasted_iota(jnp.int32, (DK_B, DK_B), 0)
    ci = lax.broadcasted_iota(jnp.int32, (DK_B, DK_B), 1)
    eye = ri == ci
    nt_dims = (((1,), (1,)), ((), ()))
    tn_dims = (((0,), (0,)), ((), ()))

    for hh in range(hp):
        ks = slice(hh * DK_B, (hh + 1) * DK_B)
        vs = slice(hh * DV_B, (hh + 1) * DV_B)
        q = q_ref[:, :, ks].reshape(rows, DK_B) * (DK_B ** -0.5)
        k = k_ref[:, :, ks].reshape(rows, DK_B)
        v = v_ref[:, :, vs].reshape(rows, DV_B)
        vb = v.astype(BF16)
        la = la_ref[:, :, ks].reshape(rows, DK_B) * LOG2_E
        la_hi = la.astype(BF16)
        la_lo = (la - la_hi.astype(F32)).astype(BF16)
        x = jnp.dot(mall_ref[...], jnp.concatenate([la_hi, la_lo], axis=1), preferred_element_type=F32)
        x = x[:, :DK_B] + x[:, DK_B:]
        g = x[0:rows]

        a = mask_ref[0] * lax.dot_general(q.astype(BF16), k.astype(BF16), nt_dims,
                                          preferred_element_type=F32)
        n_slab = 1
        for lvl in range(n_lvl):
            s = tt >> (lvl + 1)
            if s >= MIN_REF_LEVEL:
                xl = _block_ref_exponent(g, 2 * s, s - 1, s)
            else:
                xl = x[n_slab * rows:(n_slab + 1) * rows]
                n_slab += 1
            el = jnp.exp2(xl)
            p = lax.dot_general((q * el).astype(BF16), (k * el).astype(BF16), nt_dims,
                                preferred_element_type=F32)
            a = a + mask_ref[lvl + 1] * p
        o_intra = jnp.dot(a.astype(BF16), vb, preferred_element_type=F32)
        e_g = jnp.exp2(g)
        qg = q * e_g
        kd = k * jnp.exp2(_block_ref_exponent(g, tt, tt - 1, tt))

        o_parts = []
        for b in range(bs):
            rs = slice(b * tt, (b + 1) * tt)
            s = s_scr[b, hh]
            o_parts.append(o_intra[rs] + jnp.dot(qg[rs].astype(BF16), s.astype(BF16),
                                                 preferred_element_type=F32))
            e_last = e_g[(b + 1) * tt - 1:(b + 1) * tt]
            e_col = jnp.sum(jnp.where(eye, jnp.broadcast_to(e_last, (DK_B, DK_B)), 0.0),
                            axis=1, keepdims=True)
            kv = lax.dot_general(kd[rs].astype(BF16), v[rs].astype(BF16), tn_dims,
                                 preferred_element_type=F32)
            s_scr[b, hh] = e_col * s + kv
        o = o_parts[0] if bs == 1 else jnp.concatenate(o_parts, axis=0)
        yn = o * lax.rsqrt(jnp.mean(o * o, axis=-1, keepdims=True) + EPS) * g_ref[...]
        y_ref[:, vs] = (yn * _silu(r_ref[:, :, vs].reshape(rows, DV_B))).astype(BF16)
        if between:
            between[hh]()


def _gla_parts(z_head, la, g_gla, s0, layer, s_out_prev):
    n_seq, t, _ = z_head.shape
    tile = ROW_TILE if t >= ROW_TILE else 2 * ROW_TILE
    bs, tt = _row_tiling(t, tile)
    hp = H_B if bs == 1 else 1
    nt = t // tt
    assert nt == 1 or hp == H_B
    rows = n_seq * t
    mall, masks = _gla_tables(tile, tt)
    mall = jnp.asarray(mall, BF16)
    masks = jnp.asarray(masks, F32)
    zs = lambda w, c0: pl.BlockSpec((bs, tt, hp * w), lambda i, j, h=0: (i, j, c0 // hp + h))
    s_spec = pl.BlockSpec((None, bs, hp, DK_B, DV_B), lambda i, j, h=0: (layer, i, h, 0, 0))
    in_specs = [zs(DK_B, ZQ0), zs(DK_B, ZK0), zs(DV_B, ZV0), zs(DV_B, ZR0),
                pl.BlockSpec((bs, tt, hp * DK_B), lambda i, j, h=0: (i, j, h)),
                pl.BlockSpec(mall.shape, lambda *_: (0, 0)),
                pl.BlockSpec(masks.shape, lambda *_: (0, 0, 0)),
                pl.BlockSpec((None, 1, DV_B), lambda *_: (layer, 0, 0)),
                s_spec]
    args = [z_head, z_head, z_head, z_head, la, mall, masks, g_gla, s0]
    if s_out_prev is not None:
        in_specs.append(pl.BlockSpec(memory_space=pl.ANY))
        args.append(s_out_prev)
    return dict(args=args, in_specs=in_specs, aliased=s_out_prev is not None,
                out_shape=[jax.ShapeDtypeStruct((rows, W_B), BF16),
                           jax.ShapeDtypeStruct((DEPTH, n_seq, H_B, DK_B, DV_B), F32)],
                out_specs=[pl.BlockSpec((tile, hp * DV_B), lambda i, j, h=0: (i * nt + j, h)), s_spec],
                scratch=[pltpu.VMEM((bs, hp, DK_B, DV_B), F32)], grid=(n_seq // bs, nt, H_B // hp))


def _gla(*operands):
    m = _gla_parts(*operands)
    aliases = {len(m["args"]) - 1: 1} if m["aliased"] else {}
    return pl.pallas_call(
        _gla_kernel, out_shape=m["out_shape"], grid=m["grid"], in_specs=m["in_specs"],
        out_specs=m["out_specs"], scratch_shapes=m["scratch"], input_output_aliases=aliases,
        compiler_params=_cparams(("parallel", "arbitrary", "parallel")), name="gla")(*m["args"])


def _mix_gla_kernel(*refs, n_in, n_out, mix_kernel):
    (mi, gi), (mo, go) = n_in, n_out
    o0 = mi + gi
    s0 = o0 + mo + go
    mix_refs = (*refs[:mi], *refs[o0:o0 + mo], refs[s0])
    gla_refs = (*refs[mi:o0], *refs[o0 + mo:s0], refs[s0 + 1])
    for phase in PHASES:
        pieces = []
        mix_kernel(*mix_refs, phases=(phase,), defer_to=pieces)
        _gla_kernel(*gla_refs, phases=(phase,), between=pieces)


def _mix_gla(mix_operands, gla_operands):
    m = _mix_ac_parts(*mix_operands)
    g = _gla_parts(*gla_operands)
    assert m["grid"] == g["grid"][:2] and g["grid"][2] == 1
    n_in = (len(m["args"]), len(g["args"]))
    n_out = (len(m["out_shape"]), len(g["out_shape"]))
    aliases = {sum(n_in) - 1: n_out[0] + 1} if g["aliased"] else {}
    res = pl.pallas_call(
        functools.partial(_mix_gla_kernel, n_in=n_in, n_out=n_out, mix_kernel=m["kernel"]),
        out_shape=m["out_shape"] + g["out_shape"], grid=m["grid"],
        in_specs=m["in_specs"] + g["in_specs"], out_specs=m["out_specs"] + g["out_specs"],
        scratch_shapes=m["scratch"] + g["scratch"], input_output_aliases=aliases,
        compiler_params=_cparams(("parallel", "arbitrary")), name="mix_gla")(*m["args"], *g["args"])
    return res[:n_out[0]], res[n_out[0]:]


def _layer(x, layer, mod3, s_gla, buf_conv, p, wts, s_out_prev, emit_v):
    n_seq, t, d = x.shape
    seq_len = min(t, CHUNK)
    reps = ROW_TILE // seq_len
    h, la = _norm_call(x, p["g_mix"], mod3, 1, 0, gate_w=(p["w_a"], p["w_a2"], p["b_a2"]))
    z_head, wb_head = _in_proj(h, wts["w_head"], layer, A_OFF, n_seq, t)
    if wts["w_tail"] is None:
        z_tail, wb_tail = _in_proj_tail_cast(h, wts["w_head"], layer, n_seq, t)
    else:
        z_tail, wb_tail = _in_proj(h, wts["w_tail"], layer, P_TAIL, n_seq, t)
    wt = jnp.tile(p["w_s"][:, :seq_len, :seq_len], (1, reps, reps))
    bias_full = jnp.repeat(jnp.tile(p["b_s"][:, :seq_len].T, (reps, 1)), DH_A, axis=1)
    mix_ops = (z_head, z_tail, wt, bias_full, p["w_conv"], layer, buf_conv, emit_v)
    gla_ops = (z_head, la, p["g_gla"], s_gla, layer, s_out_prev)
    if t >= ROW_TILE:
        mix, (yb, s_out) = _mix_gla(mix_ops, gla_ops)
    else:
        mix = _mix_ac(*mix_ops)
        yb, s_out = _gla(*gla_ops)
    ya, yc, buf_new = mix[:3]
    vn = mix[3] if emit_v else None
    x, wb_out = _out_proj(ya, yb, yc, wts["w_out"], layer, x, mod3, 2)
    h2 = _norm_call(x, p["g_ffn"], mod3, 4, 3)
    a, wb_gate, wb_up = _swiglu(h2, wts["w_gate"], wts["w_up"], layer)
    x, wb_down = _down_proj(a, wts["w_down"], layer, x, mod3, 5)
    wb = dict(w_head=wb_head, w_tail=wb_tail, w_out=wb_out, w_gate=wb_gate, w_up=wb_up, w_down=wb_down)
    return x, s_out, buf_new, vn, wb


def kernel(x_prompt, x_sample, state_gla, state_conv, c_prompt, c_sample, g_mix, g_ffn, w_mod, b_mod,
           w_in, w_s, b_s, w_a2, b_a2, g_gla, w_conv, w_out, w_gate, w_up, w_down, g_final):
    bp = x_prompt.shape[0]
    bd = x_sample.shape[0]
    n_c = bp + bd
    mc = -(-n_c // 16) * 16
    c_all = jnp.concatenate([c_prompt, c_sample, jnp.zeros((mc - n_c, D_MODEL), F32)], axis=0)
    mod = _modulation(c_all, w_mod, b_mod)
    gf = g_final.reshape(1, D_MODEL)
    g_gla3 = g_gla.reshape(DEPTH, 1, DV_B)
    gla0 = jnp.zeros((DEPTH, bp, H_B, DK_B, DV_B), F32)
    conv0 = jnp.zeros((DEPTH, bp, CONV_W - 1, W_C), F32)

    xs, xp = x_sample, x_prompt
    gla_s = gla_p = None
    conv_s, conv_p, v_s = [], [], []
    wt_in = jnp.swapaxes(w_in, 1, 2)
    for l in range(DEPTH):
        w_a = wt_in[l, A_OFF:A_OFF + GATE_RANK, :].T
        p = dict(
            g_mix=g_mix[l].reshape(1, D_MODEL), g_ffn=g_ffn[l].reshape(1, D_MODEL),
            w_a=jnp.pad(w_a, ((0, 0), (0, LANES - GATE_RANK))),
            w_a2=jnp.pad(w_a2[l], ((0, LANES - GATE_RANK), (0, 0))).astype(BF16),
            b_a2=b_a2[l].reshape(1, KW_B),
            w_s=w_s[l], b_s=b_s[l], g_gla=g_gla3, w_conv=w_conv)
        w_f32 = dict(w_head=wt_in, w_tail=None, w_out=w_out,
                     w_gate=w_gate, w_up=w_up, w_down=w_down)
        mod_s = mod[l, bp:n_c].reshape(bd, 1, N_MOD * D_MODEL)
        mod_p = mod[l, :bp].reshape(bp, 1, N_MOD * D_MODEL)
        xs, gla_s, buf_s, vn_s, w_bf16 = _layer(xs, l, mod_s, state_gla, state_conv, p, w_f32, gla_s, True)
        xp, gla_p, buf_p, _, _ = _layer(xp, l, mod_p, gla0, conv0, p, w_bf16, gla_p, False)
        conv_s.append(buf_s)
        conv_p.append(buf_p)
        v_s.append(vn_s)
    y_s = _final_norm(xs, gf)
    y_p = _final_norm(xp, gf)
    return (y_p, y_s, gla_p, jnp.stack(conv_p), gla_s, jnp.stack(conv_s), jnp.stack(v_s))
```

```python
import functools

import numpy as np
import jax
import jax.numpy as jnp
from jax import lax
from jax.experimental import pallas as pl
from jax.experimental.pallas import tpu as pltpu

F32 = jnp.float32
BF16 = jnp.bfloat16

D_MODEL = 4096
DEPTH = 2
EPS = 1e-6
CHUNK = 128
W_A = D_MODEL // 4
G_A = 8
DH_A = W_A // G_A
W_B = D_MODEL // 2
H_B = 8
DV_B = W_B // H_B
DK_B = DV_B // 2
KW_B = H_B * DK_B
GATE_RANK = 16
GATE_TAU = 16.0
W_C = D_MODEL // 4
CONV_W = 3
D_FF = -(-8 * D_MODEL // (3 * 256)) * 256
N_MOD = 6
A_OFF = 2 * W_A + 2 * KW_B + 2 * W_B
P_TAIL = 3 * W_C

LOG2_E = float(np.log2(np.e))
LANES = 128
ROW_TILE = 128
MIN_REF_LEVEL = 4
TM = 1024
IN_TN = (512, 1024)
OUT_TN = (512, 1024)
FFN_TN = 256
DOWN_TN = 256
MOD_TN = 512
NORM_ROWS = 256
ROW_SPLIT = 2
T_AXIS = 1
PHASES = ("init", "body", "final")
VMEM_LIMIT = 56 * 1024 * 1024
DOWN_PROJ_VMEM_LIMIT = 62 * 1024 * 1024

ZQ0 = (2 * W_A) // DK_B
ZK0 = (2 * W_A + KW_B) // DK_B
ZV0 = (2 * W_A + 2 * KW_B) // DV_B
ZR0 = (2 * W_A + 2 * KW_B + W_B) // DV_B


def _cparams(sem):
    return pltpu.CompilerParams(dimension_semantics=sem, vmem_limit_bytes=VMEM_LIMIT)


def _silu(x):
    return x / (1.0 + jnp.exp(-x))


def _gelu_tanh(x):
    c = np.float32(np.sqrt(2.0 / np.pi))
    return 0.5 * x * (1.0 + jnp.tanh(c * (x + 0.044715 * (x * x * x))))


def _row_tiling(t, rows):
    if t >= rows:
        return 1, rows
    return rows // t, t


def _w_spec(w, layer, k, tn, col_map):
    if w.ndim == 2:
        return pl.BlockSpec((k, tn), lambda *g: (0, col_map(*g)))
    return pl.BlockSpec((None, k, tn), lambda *g: (layer, 0, col_map(*g)))


def _mod_kernel(c_ref, w_ref, b_ref, o_ref):
    s = _silu(c_ref[...]).astype(BF16)
    o_ref[0] = jnp.dot(s, w_ref[0].astype(BF16), preferred_element_type=F32) + b_ref[0]


def _modulation(c_all, w_mod, b_mod):
    mc = c_all.shape[0]
    n = N_MOD * D_MODEL
    tn = MOD_TN
    return pl.pallas_call(
        _mod_kernel,
        out_shape=jax.ShapeDtypeStruct((DEPTH, mc, n), F32),
        grid=(DEPTH, n // tn),
        in_specs=[
            pl.BlockSpec((mc, D_MODEL), lambda l, j: (0, 0)),
            pl.BlockSpec((1, D_MODEL, tn), lambda l, j: (l, 0, j)),
            pl.BlockSpec((1, 1, tn), lambda l, j: (l, 0, j)),
        ],
        out_specs=pl.BlockSpec((1, mc, tn), lambda l, j: (l, 0, j)),
        compiler_params=_cparams(("parallel", "parallel")),
        name="modulation",
    )(c_all, w_mod, b_mod.reshape(DEPTH, 1, n))


def _norm_mod(x, g, sc, sh):
    y = x * lax.rsqrt(jnp.mean(x * x, axis=-1, keepdims=True) + EPS) * g
    return y * (1.0 + sc) + sh


def _norm_gate_kernel(x_ref, g_ref, sc_ref, sh_ref, wa_ref, wa2_ref, ba2_ref, h_ref, la_ref):
    bs, tt, d = x_ref.shape
    h = _norm_mod(x_ref[...], g_ref[...], sc_ref[...], sh_ref[...]).reshape(bs * tt, d).astype(BF16)
    h_ref[...] = h
    a = jnp.dot(h, wa_ref[...].astype(BF16), preferred_element_type=F32)
    pre = jnp.dot(a.astype(BF16), wa2_ref[...], preferred_element_type=F32) + ba2_ref[...]
    log_sig = jnp.minimum(pre, 0.0) - jnp.log(1.0 + jnp.exp(-jnp.abs(pre)))
    la_ref[...] = (log_sig / GATE_TAU).reshape(bs, tt, KW_B)


def _norm_kernel(x_ref, g_ref, sc_ref, sh_ref, h_ref):
    bs, tt, d = x_ref.shape
    h_ref[...] = _norm_mod(x_ref[...], g_ref[...], sc_ref[...], sh_ref[...]).reshape(bs * tt, d).astype(BF16)


def _norm_call(x, g, mod3, sc_idx, sh_idx, gate_w=None):
    n_seq, t, d = x.shape
    bs, tt = _row_tiling(t, NORM_ROWS)
    rows = bs * tt
    nt = t // tt
    grid = (n_seq // bs, nt)
    x_spec = pl.BlockSpec((bs, tt, d), lambda i, j: (i, j, 0))
    g_spec = pl.BlockSpec((1, d), lambda i, j: (0, 0))
    mod_spec = lambda c: pl.BlockSpec((bs, 1, d), lambda i, j: (i, 0, c))
    h_spec = pl.BlockSpec((rows, d), lambda i, j: (i * nt + j, 0))
    h_shape = jax.ShapeDtypeStruct((n_seq * t, d), BF16)
    if gate_w is None:
        return pl.pallas_call(
            _norm_kernel, out_shape=h_shape, grid=grid,
            in_specs=[x_spec, g_spec, mod_spec(sc_idx), mod_spec(sh_idx)],
            out_specs=h_spec, compiler_params=_cparams(("parallel", "parallel")),
            name="norm_mod")(x, g, mod3, mod3)
    wa, wa2, ba2 = gate_w
    return pl.pallas_call(
        _norm_gate_kernel,
        out_shape=(h_shape, jax.ShapeDtypeStruct((n_seq, t, KW_B), F32)),
        grid=grid,
        in_specs=[x_spec, g_spec, mod_spec(sc_idx), mod_spec(sh_idx),
                  pl.BlockSpec((d, LANES), lambda i, j: (0, 0)),
                  pl.BlockSpec((LANES, KW_B), lambda i, j: (0, 0)),
                  pl.BlockSpec((1, KW_B), lambda i, j: (0, 0))],
        out_specs=(h_spec, pl.BlockSpec((bs, tt, KW_B), lambda i, j: (i, j, 0))),
        compiler_params=_cparams(("parallel", "parallel")),
        name="norm_mod_gate")(x, g, mod3, mod3, wa, wa2, ba2)


def _final_norm_kernel(x_ref, g_ref, o_ref):
    x = x_ref[...]
    o_ref[...] = x * lax.rsqrt(jnp.mean(x * x, axis=-1, keepdims=True) + EPS) * g_ref[...]


def _final_norm(x, g):
    n_seq, t, d = x.shape
    bs, tt = _row_tiling(t, NORM_ROWS)
    return pl.pallas_call(
        _final_norm_kernel, out_shape=jax.ShapeDtypeStruct(x.shape, F32),
        grid=(n_seq // bs, t // tt),
        in_specs=[pl.BlockSpec((bs, tt, d), lambda i, j: (i, j, 0)),
                  pl.BlockSpec((1, d), lambda i, j: (0, 0))],
        out_specs=pl.BlockSpec((bs, tt, d), lambda i, j: (i, j, 0)),
        compiler_params=_cparams(("parallel", "parallel")),
        name="final_norm")(x, g)


def _bf16_tile(w_ref, wb_ref):
    if wb_ref is None:
        return w_ref[...]
    wb = w_ref[...].astype(BF16)
    wb_ref[...] = wb
    return wb


def _in_proj_kernel(h_ref, wt_ref, z_ref, wtb_ref=None):
    acc = lax.dot_general(h_ref[...], _bf16_tile(wt_ref, wtb_ref), (((1,), (1,)), ((), ())),
                          preferred_element_type=F32)
    z_ref[...] = acc.reshape(z_ref.shape)


def _in_proj(h, wt, layer, n_cols, n_seq, t):
    rows, k = h.shape
    emit = wt.dtype != BF16
    bs, tt = _row_tiling(t, TM)
    nt = t // tt
    tn = IN_TN[0] if emit else IN_TN[1]
    if wt.ndim == 2:
        w_spec = pl.BlockSpec((tn, k), lambda i, j: (j, 0))
    else:
        w_spec = pl.BlockSpec((None, tn, k), lambda i, j: (layer, j, 0))
    out_shape = [jax.ShapeDtypeStruct((n_seq, t, n_cols), F32)]
    out_specs = [pl.BlockSpec((bs, tt, tn), lambda i, j: (i // nt, i % nt, j))]
    if emit:
        assert rows == TM
        out_shape.append(jax.ShapeDtypeStruct((n_cols, k), BF16))
        out_specs.append(pl.BlockSpec((tn, k), lambda i, j: (j, 0)))
    res = pl.pallas_call(
        _in_proj_kernel, out_shape=out_shape,
        grid=(rows // TM, n_cols // tn),
        in_specs=[pl.BlockSpec((TM, k), lambda i, j: (i, 0)), w_spec],
        out_specs=out_specs,
        compiler_params=_cparams(("parallel", "parallel")),
        name="in_proj_cast" if emit else "in_proj")(h, wt)
    return (res[0], res[1]) if emit else (res[0], None)


def _in_proj_tail_cast_kernel(h_ref, wa_ref, wb_ref, z_ref, wtb_ref):
    wt = jnp.concatenate([wa_ref[GATE_RANK:, :], wb_ref[...]], axis=0).astype(BF16)
    wtb_ref[...] = wt
    acc = lax.dot_general(h_ref[...], wt, (((1,), (1,)), ((), ())), preferred_element_type=F32)
    z_ref[...] = acc.reshape(z_ref.shape)


def _in_proj_tail_cast(h, wt_all, layer, n_seq, t):
    rows, k = h.shape
    assert rows == TM
    bs, tt = _row_tiling(t, TM)
    tn = IN_TN[0]
    return pl.pallas_call(
        _in_proj_tail_cast_kernel,
        out_shape=(jax.ShapeDtypeStruct((n_seq, t, P_TAIL), F32), jax.ShapeDtypeStruct((P_TAIL, k), BF16)),
        grid=(P_TAIL // tn,),
        in_specs=[pl.BlockSpec((TM, k), lambda j: (0, 0)),
                  pl.BlockSpec((None, tn, k), lambda j: (layer, A_OFF // tn + j, 0)),
                  pl.BlockSpec((None, GATE_RANK, k),
                               lambda j: (layer, (A_OFF + tn * (j + 1)) // GATE_RANK, 0))],
        out_specs=(pl.BlockSpec((bs, tt, tn), lambda j: (0, 0, j)),
                   pl.BlockSpec((tn, k), lambda j: (j, 0))),
        compiler_params=_cparams(("parallel",)),
        name="in_proj_tail_cast")(h, wt_all, wt_all)


def _out_proj_kernel(ya_ref, yb_ref, yc_ref, w_ref, x_ref, gt_ref, o_ref, wb_ref=None):
    if wb_ref is not None:
        wb_ref[...] = w_ref[...].astype(BF16)
        w_ref = wb_ref
    acc = jnp.dot(ya_ref[...], w_ref[0:W_A, :], preferred_element_type=F32)
    acc += jnp.dot(yb_ref[...], w_ref[W_A:W_A + W_B, :], preferred_element_type=F32)
    acc += jnp.dot(yc_ref[...], w_ref[W_A + W_B:, :], preferred_element_type=F32)
    o_ref[...] = x_ref[...] + gt_ref[...] * acc.reshape(o_ref.shape)


def _out_proj(ya, yb, yc, w, layer, x, mod3, gt_idx):
    n_seq, t, d = x.shape
    rows = n_seq * t
    emit = w.dtype != BF16
    bs, tt = _row_tiling(t, TM)
    nt = t // tt
    tn = OUT_TN[0] if emit else OUT_TN[1]
    xo_spec = pl.BlockSpec((bs, tt, tn), lambda i, j: (i // nt, i % nt, j))
    out_shape = [jax.ShapeDtypeStruct(x.shape, F32)]
    out_specs = [xo_spec]
    if emit:
        assert rows == TM
        out_shape.append(jax.ShapeDtypeStruct((d, d), BF16))
        out_specs.append(pl.BlockSpec((d, tn), lambda i, j: (0, j)))
    res = pl.pallas_call(
        _out_proj_kernel, out_shape=out_shape,
        grid=(rows // TM, d // tn),
        in_specs=[pl.BlockSpec((TM, W_A), lambda i, j: (i, 0)),
                  pl.BlockSpec((TM, W_B), lambda i, j: (i, 0)),
                  pl.BlockSpec((TM, W_C), lambda i, j: (i, 0)),
                  _w_spec(w, layer, d, tn, lambda i, j: j),
                  xo_spec,
                  pl.BlockSpec((bs, 1, tn), lambda i, j: (i // nt, 0, gt_idx * (d // tn) + j))],
        out_specs=out_specs,
        compiler_params=_cparams(("parallel", "parallel")),
        name="out_proj_cast" if emit else "out_proj")(ya, yb, yc, w, x, mod3)
    return (res[0], res[1]) if emit else (res[0], None)


def _swiglu_kernel(h_ref, wg_ref, wu_ref, a_ref, wgb_ref=None, wub_ref=None):
    wg = _bf16_tile(wg_ref, wgb_ref)
    wu = _bf16_tile(wu_ref, wub_ref)
    rows = h_ref.shape[0]
    part = rows // ROW_SPLIT
    for r0 in range(0, rows, part):
        h = h_ref[r0:r0 + part, :]
        g = jnp.dot(h, wg, preferred_element_type=F32)
        u = jnp.dot(h, wu, preferred_element_type=F32)
        a_ref[r0:r0 + part, :] = (_silu(g) * u).astype(BF16)


def _swiglu(h, wg, wu, layer):
    rows, k = h.shape
    n = wg.shape[-1]
    emit = wg.dtype != BF16
    tn = FFN_TN
    tm = TM if emit or rows % (2 * TM) else 2 * TM
    out_shape = [jax.ShapeDtypeStruct((rows, n), BF16)]
    out_specs = [pl.BlockSpec((tm, tn), lambda i, j: (i, j))]
    if emit:
        assert rows == TM
        out_shape += [jax.ShapeDtypeStruct((k, n), BF16)] * 2
        out_specs += [pl.BlockSpec((k, tn), lambda i, j: (0, j))] * 2
    res = pl.pallas_call(
        _swiglu_kernel, out_shape=out_shape,
        grid=(rows // tm, n // tn),
        in_specs=[pl.BlockSpec((tm, k), lambda i, j: (i, 0)),
                  _w_spec(wg, layer, k, tn, lambda i, j: j),
                  _w_spec(wu, layer, k, tn, lambda i, j: j)],
        out_specs=out_specs,
        compiler_params=_cparams(("parallel", "parallel")),
        name="swiglu_cast" if emit else "swiglu")(h, wg, wu)
    return (res[0], res[1], res[2]) if emit else (res[0], None, None)


def _down_proj_cast_kernel(a_ref, w_ref, x_ref, gt_ref, o_ref, wb_ref, acc_ref):
    kk = pl.program_id(0)
    j = pl.program_id(1)
    part = jnp.dot(a_ref[...], _bf16_tile(w_ref, wb_ref), preferred_element_type=F32)

    @pl.when(kk == 0)
    def _():
        acc_ref[j] = part

    @pl.when(kk == 1)
    def _():
        o_ref[...] = x_ref[...] + gt_ref[...] * (acc_ref[j] + part).reshape(o_ref.shape)


def _down_proj_kernel(a_ref, w_ref, x_ref, gt_ref, o_ref):
    acc = jnp.dot(a_ref[...], w_ref[...], preferred_element_type=F32)
    o_ref[...] = x_ref[...] + gt_ref[...] * acc.reshape(o_ref.shape)


def _down_proj(a, w, layer, x, mod3, gt_idx):
    n_seq, t, d = x.shape
    rows, k = a.shape
    bs, tt = _row_tiling(t, TM)
    nt = t // tt
    tn = DOWN_TN
    n_j = d // tn
    if w.dtype == BF16:
        xo_spec = pl.BlockSpec((bs, tt, tn), lambda i, j: (i // nt, i % nt, j))
        out = pl.pallas_call(
            _down_proj_kernel, out_shape=jax.ShapeDtypeStruct(x.shape, F32),
            grid=(rows // TM, n_j),
            in_specs=[pl.BlockSpec((TM, k), lambda i, j: (i, 0)),
                      pl.BlockSpec((k, tn), lambda i, j: (0, j)),
                      xo_spec,
                      pl.BlockSpec((bs, 1, tn), lambda i, j: (i // nt, 0, gt_idx * n_j + j))],
            out_specs=xo_spec,
            compiler_params=pltpu.CompilerParams(dimension_semantics=("parallel", "parallel"),
                                                 vmem_limit_bytes=DOWN_PROJ_VMEM_LIMIT),
            name="down_proj")(a, w, x, mod3)
        return out, None
    assert rows == TM
    tk = k // 2
    xo_spec = pl.BlockSpec((bs, tt, tn), lambda kk, j: (0, 0, j * kk))
    out, wb = pl.pallas_call(
        _down_proj_cast_kernel,
        out_shape=(jax.ShapeDtypeStruct(x.shape, F32), jax.ShapeDtypeStruct((k, d), BF16)),
        grid=(2, n_j),
        in_specs=[pl.BlockSpec((TM, tk), lambda kk, j: (0, kk), pipeline_mode=pl.Buffered(1)),
                  pl.BlockSpec((None, tk, tn), lambda kk, j: (layer, kk, j)),
                  xo_spec,
                  pl.BlockSpec((bs, 1, tn), lambda kk, j: (0, 0, gt_idx * n_j + j * kk))],
        out_specs=(xo_spec, pl.BlockSpec((tk, tn), lambda kk, j: (kk, j))),
        scratch_shapes=[pltpu.VMEM((n_j, TM, tn), F32)],
        compiler_params=_cparams(("arbitrary", "arbitrary")),
        name="down_proj_cast")(a, w, x, mod3)
    return out, wb


def _mix_ac_kernel(u_ref, v_ref, b_ref, c_ref, hc_ref, wt_ref, bias_ref, wc_ref, buf_ref,
                   ya_ref, yc_ref, bufnew_ref, *rest, seq_len, phases=PHASES, defer_to=None):
    zbuf = rest[-1]
    vn_ref = rest[0] if len(rest) == 2 else None
    bs, tt, w = u_ref.shape
    rows = bs * tt
    t = pl.program_id(T_AXIS)
    pad = 8

    if "init" in phases:
        @pl.when(t == 0)
        def _():
            zbuf[:, pad - 2:pad, :] = buf_ref[...]

    if "body" in phases:
        work = _mix_ac_work(u_ref, v_ref, b_ref, c_ref, hc_ref, wt_ref, bias_ref, wc_ref, ya_ref, yc_ref,
                            vn_ref, zbuf, pad, seq_len)
        if defer_to is None:
            for piece in work:
                piece()
        else:
            defer_to.extend(work)

    if "final" in phases:
        @pl.when(t == pl.num_programs(T_AXIS) - 1)
        def _():
            bufnew_ref[...] = zbuf[:, pad - 2:pad, :]


def _mix_ac_work(u_ref, v_ref, b_ref, c_ref, hc_ref, wt_ref, bias_ref, wc_ref, ya_ref, yc_ref, vn_ref, zbuf,
                 pad, seq_len):
    bs, tt, _ = u_ref.shape
    rows = bs * tt

    def piece(g):
        sl = slice(g * DH_A, (g + 1) * DH_A)
        zc = c_ref[:, :, sl] * hc_ref[:, :, sl]
        zbuf[:, pad:pad + tt, sl] = zc
        y = zbuf[:, pad - 2:pad - 2 + tt, sl] * wc_ref[0:1, sl]
        y = y + zbuf[:, pad - 1:pad - 1 + tt, sl] * wc_ref[1:2, sl]
        y = y + zc * wc_ref[2:3, sl]
        yc_ref[:, sl] = (b_ref[:, :, sl] * y).reshape(rows, DH_A).astype(BF16)
        zbuf[:, pad - 2:pad, sl] = zbuf[:, pad + tt - 2:pad + tt, sl]

        gu = _gelu_tanh(u_ref[:, :, sl].reshape(rows, DH_A))
        vg = _gelu_tanh(v_ref[:, :, sl].reshape(rows, DH_A))
        dv = vg - jnp.mean(vg, axis=-1, keepdims=True)
        vn = dv * lax.rsqrt(jnp.mean(dv * dv, axis=-1, keepdims=True) + EPS)
        ri = lax.broadcasted_iota(jnp.int32, (rows, rows), 0)
        ci = lax.broadcasted_iota(jnp.int32, (rows, rows), 1)
        keep = ci <= ri
        if seq_len < rows:
            keep = keep & ((ri // seq_len) == (ci // seq_len))
        wm = jnp.where(keep, wt_ref[g], 0.0).astype(BF16)
        mixed = jnp.dot(wm, vn.astype(BF16), preferred_element_type=F32) + bias_ref[:, sl]
        ya_ref[:, sl] = (gu * mixed).astype(BF16)
        if vn_ref is not None:
            vn_ref[:, :, sl] = vn.reshape(bs, tt, DH_A)

    return [functools.partial(piece, g) for g in range(G_A)]


def _mix_ac_parts(z_head, z_tail, wt, bias_full, w_conv, layer, buf, emit_v):
    n_seq, t, _ = z_head.shape
    bs, tt = _row_tiling(t, ROW_TILE)
    nt = t // tt
    rows = n_seq * t
    zspec = lambda c: pl.BlockSpec((bs, tt, W_A), lambda i, j, *_: (i, j, c))
    y_spec = pl.BlockSpec((ROW_TILE, W_A), lambda i, j, *_: (i * nt + j, 0))
    out_shape = [jax.ShapeDtypeStruct((rows, W_A), BF16),
                 jax.ShapeDtypeStruct((rows, W_C), BF16),
                 jax.ShapeDtypeStruct((n_seq, CONV_W - 1, W_C), F32)]
    out_specs = [y_spec, y_spec, pl.BlockSpec((bs, CONV_W - 1, W_C), lambda i, j, *_: (i, 0, 0))]
    if emit_v:
        out_shape.append(jax.ShapeDtypeStruct((n_seq, t, W_A), F32))
        out_specs.append(pl.BlockSpec((bs, tt, W_A), lambda i, j, *_: (i, j, 0)))
    in_specs = [zspec(0), zspec(1), zspec(0), zspec(1), zspec(2),
                pl.BlockSpec((G_A, ROW_TILE, ROW_TILE), lambda *_: (0, 0, 0)),
                pl.BlockSpec((ROW_TILE, W_A), lambda *_: (0, 0)),
                pl.BlockSpec((None, CONV_W, W_C), lambda *_: (layer, 0, 0)),
                pl.BlockSpec((None, bs, CONV_W - 1, W_C), lambda i, *_: (layer, i, 0, 0))]
    return dict(args=[z_head, z_head, z_tail, z_tail, z_tail, wt, bias_full, w_conv, buf],
                in_specs=in_specs, out_shape=out_shape, out_specs=out_specs,
                scratch=[pltpu.VMEM((bs, tt + 8, W_C), F32)], grid=(n_seq // bs, nt),
                kernel=functools.partial(_mix_ac_kernel, seq_len=min(t, CHUNK)))


def _mix_ac(*operands):
    m = _mix_ac_parts(*operands)
    return pl.pallas_call(
        m["kernel"], out_shape=m["out_shape"], grid=m["grid"], in_specs=m["in_specs"],
        out_specs=m["out_specs"], scratch_shapes=m["scratch"],
        compiler_params=_cparams(("parallel", "arbitrary")), name="mix_ac")(*m["args"])


def _gla_tables(rows, seq_len):
    i = np.arange(rows)[:, None]
    t = np.arange(rows)[None, :]
    same = (i // seq_len) == (t // seq_len)
    slabs = [same & (t <= i)]
    masks = [i == t]
    s = seq_len // 2
    while s >= 1:
        second = (i // s) % 2 == 1
        start2 = (i // s) * s
        end1 = start2 + s - 1
        if s < MIN_REF_LEVEL:
            slabs.append(np.where(second, (t >= start2) & (t <= i), (t > i) & (t <= end1)))
        masks.append(((i // (2 * s)) == (t // (2 * s))) & second & ((t // s) % 2 == 0))
        s //= 2
    return (np.concatenate(slabs, 0).astype(np.float32), np.stack(masks).astype(np.float32))


def _block_ref_exponent(g, block, ref_row, flip_from):
    rows, dk = g.shape
    gr = g.reshape(rows // block, block, dk)
    d = gr - gr[:, ref_row:ref_row + 1, :]
    pos = lax.broadcasted_iota(jnp.int32, gr.shape, 1)
    return jnp.where(pos >= flip_from, d, -d).reshape(rows, dk)


def _gla_kernel(q_ref, k_ref, v_ref, r_ref, la_ref, mall_ref, mask_ref, g_ref, s0_ref, *rest, phases=PHASES,
                between=()):
    y_ref, snew_ref, s_scr = rest[-3:]
    t = pl.program_id(T_AXIS)

    if "init" in phases:
        @pl.when(t == 0)
        def _():
            s_scr[...] = s0_ref[...]

    if "body" in phases:
        _gla_body(q_ref, k_ref, v_ref, r_ref, la_ref, mall_ref, mask_ref, g_ref, y_ref, s_scr, between)

    if "final" in phases:
        @pl.when(t == pl.num_programs(T_AXIS) - 1)
        def _():
            snew_ref[...] = s_scr[...]


def _gla_body(q_ref, k_ref, v_ref, r_ref, la_ref, mall_ref, mask_ref, g_ref, y_ref, s_scr, between):
    bs, tt, _ = q_ref.shape
    hp = s_scr.shape[1]
    assert len(between) in (0, hp)
    rows = bs * tt
    n_lvl = mask_ref.shape[0] - 1
    ri = lax.broadcasted_iota(jnp.int32, (DK_B, DK_B), 0)
    ci = lax.broadcasted_iota(jnp.int32, (DK_B, DK_B), 1)
    eye = ri == ci
    nt_dims = (((1,), (1,)), ((), ()))
    tn_dims = (((0,), (0,)), ((), ()))

    for hh in range(hp):
        ks = slice(hh * DK_B, (hh + 1) * DK_B)
        vs = slice(hh * DV_B, (hh + 1) * DV_B)
        q = q_ref[:, :, ks].reshape(rows, DK_B) * (DK_B ** -0.5)
        k = k_ref[:, :, ks].reshape(rows, DK_B)
        v = v_ref[:, :, vs].reshape(rows, DV_B)
        vb = v.astype(BF16)
        la = la_ref[:, :, ks].reshape(rows, DK_B) * LOG2_E
        la_hi = la.astype(BF16)
        la_lo = (la - la_hi.astype(F32)).astype(BF16)
        x = jnp.dot(mall_ref[...], jnp.concatenate([la_hi, la_lo], axis=1), preferred_element_type=F32)
        x = x[:, :DK_B] + x[:, DK_B:]
        g = x[0:rows]

        a = mask_ref[0] * lax.dot_general(q.astype(BF16), k.astype(BF16), nt_dims,
                                          preferred_element_type=F32)
        n_slab = 1
        for lvl in range(n_lvl):
            s = tt >> (lvl + 1)
            if s >= MIN_REF_LEVEL:
                xl = _block_ref_exponent(g, 2 * s, s - 1, s)
            else:
                xl = x[n_slab * rows:(n_slab + 1) * rows]
                n_slab += 1
            el = jnp.exp2(xl)
            p = lax.dot_general((q * el).astype(BF16), (k * el).astype(BF16), nt_dims,
                                preferred_element_type=F32)
            a = a + mask_ref[lvl + 1] * p
        o_intra = jnp.dot(a.astype(BF16), vb, preferred_element_type=F32)
        e_g = jnp.exp2(g)
        qg = q * e_g
        kd = k * jnp.exp2(_block_ref_exponent(g, tt, tt - 1, tt))

        o_parts = []
        for b in range(bs):
            rs = slice(b * tt, (b + 1) * tt)
            s = s_scr[b, hh]
            o_parts.append(o_intra[rs] + jnp.dot(qg[rs].astype(BF16), s.astype(BF16),
                                                 preferred_element_type=F32))
            e_last = e_g[(b + 1) * tt - 1:(b + 1) * tt]
            e_col = jnp.sum(jnp.where(eye, jnp.broadcast_to(e_last, (DK_B, DK_B)), 0.0),
                            axis=1, keepdims=True)
            kv = lax.dot_general(kd[rs].astype(BF16), v[rs].astype(BF16), tn_dims,
                                 preferred_element_type=F32)
            s_scr[b, hh] = e_col * s + kv
        o = o_parts[0] if bs == 1 else jnp.concatenate(o_parts, axis=0)
        yn = o * lax.rsqrt(jnp.mean(o * o, axis=-1, keepdims=True) + EPS) * g_ref[...]
        y_ref[:, vs] = (yn * _silu(r_ref[:, :, vs].reshape(rows, DV_B))).astype(BF16)
        if between:
            between[hh]()


def _gla_parts(z_head, la, g_gla, s0, layer, s_out_prev):
    n_seq, t, _ = z_head.shape
    tile = ROW_TILE if t >= ROW_TILE else 2 * ROW_TILE
    bs, tt = _row_tiling(t, tile)
    hp = H_B if bs == 1 else 1
    nt = t // tt
    assert nt == 1 or hp == H_B
    rows = n_seq * t
    mall, masks = _gla_tables(tile, tt)
    mall = jnp.asarray(mall, BF16)
    masks = jnp.asarray(masks, F32)
    zs = lambda w, c0: pl.BlockSpec((bs, tt, hp * w), lambda i, j, h=0: (i, j, c0 // hp + h))
    s_spec = pl.BlockSpec((None, bs, hp, DK_B, DV_B), lambda i, j, h=0: (layer, i, h, 0, 0))
    in_specs = [zs(DK_B, ZQ0), zs(DK_B, ZK0), zs(DV_B, ZV0), zs(DV_B, ZR0),
                pl.BlockSpec((bs, tt, hp * DK_B), lambda i, j, h=0: (i, j, h)),
                pl.BlockSpec(mall.shape, lambda *_: (0, 0)),
                pl.BlockSpec(masks.shape, lambda *_: (0, 0, 0)),
                pl.BlockSpec((None, 1, DV_B), lambda *_: (layer, 0, 0)),
                s_spec]
    args = [z_head, z_head, z_head, z_head, la, mall, masks, g_gla, s0]
    if s_out_prev is not None:
        in_specs.append(pl.BlockSpec(memory_space=pl.ANY))
        args.append(s_out_prev)
    return dict(args=args, in_specs=in_specs, aliased=s_out_prev is not None,
                out_shape=[jax.ShapeDtypeStruct((rows, W_B), BF16),
                           jax.ShapeDtypeStruct((DEPTH, n_seq, H_B, DK_B, DV_B), F32)],
                out_specs=[pl.BlockSpec((tile, hp * DV_B), lambda i, j, h=0: (i * nt + j, h)), s_spec],
                scratch=[pltpu.VMEM((bs, hp, DK_B, DV_B), F32)], grid=(n_seq // bs, nt, H_B // hp))


def _gla(*operands):
    m = _gla_parts(*operands)
    aliases = {len(m["args"]) - 1: 1} if m["aliased"] else {}
    return pl.pallas_call(
        _gla_kernel, out_shape=m["out_shape"], grid=m["grid"], in_specs=m["in_specs"],
        out_specs=m["out_specs"], scratch_shapes=m["scratch"], input_output_aliases=aliases,
        compiler_params=_cparams(("parallel", "arbitrary", "parallel")), name="gla")(*m["args"])


def _mix_gla_kernel(*refs, n_in, n_out, mix_kernel):
    (mi, gi), (mo, go) = n_in, n_out
    o0 = mi + gi
    s0 = o0 + mo + go
    mix_refs = (*refs[:mi], *refs[o0:o0 + mo], refs[s0])
    gla_refs = (*refs[mi:o0], *refs[o0 + mo:s0], refs[s0 + 1])
    for phase in PHASES:
        pieces = []
        mix_kernel(*mix_refs, phases=(phase,), defer_to=pieces)
        _gla_kernel(*gla_refs, phases=(phase,), between=pieces)


def _mix_gla(mix_operands, gla_operands):
    m = _mix_ac_parts(*mix_operands)
    g = _gla_parts(*gla_operands)
    assert m["grid"] == g["grid"][:2] and g["grid"][2] == 1
    n_in = (len(m["args"]), len(g["args"]))
    n_out = (len(m["out_shape"]), len(g["out_shape"]))
    aliases = {sum(n_in) - 1: n_out[0] + 1} if g["aliased"] else {}
    res = pl.pallas_call(
        functools.partial(_mix_gla_kernel, n_in=n_in, n_out=n_out, mix_kernel=m["kernel"]),
        out_shape=m["out_shape"] + g["out_shape"], grid=m["grid"],
        in_specs=m["in_specs"] + g["in_specs"], out_specs=m["out_specs"] + g["out_specs"],
        scratch_shapes=m["scratch"] + g["scratch"], input_output_aliases=aliases,
        compiler_params=_cparams(("parallel", "arbitrary")), name="mix_gla")(*m["args"], *g["args"])
    return res[:n_out[0]], res[n_out[0]:]


def _layer(x, layer, mod3, s_gla, buf_conv, p, wts, s_out_prev, emit_v):
    n_seq, t, d = x.shape
    seq_len = min(t, CHUNK)
    reps = ROW_TILE // seq_len
    h, la = _norm_call(x, p["g_mix"], mod3, 1, 0, gate_w=(p["w_a"], p["w_a2"], p["b_a2"]))
    z_head, wb_head = _in_proj(h, wts["w_head"], layer, A_OFF, n_seq, t)
    if wts["w_tail"] is None:
        z_tail, wb_tail = _in_proj_tail_cast(h, wts["w_head"], layer, n_seq, t)
    else:
        z_tail, wb_tail = _in_proj(h, wts["w_tail"], layer, P_TAIL, n_seq, t)
    wt = jnp.tile(p["w_s"][:, :seq_len, :seq_len], (1, reps, reps))
    bias_full = jnp.repeat(jnp.tile(p["b_s"][:, :seq_len].T, (reps, 1)), DH_A, axis=1)
    mix_ops = (z_head, z_tail, wt, bias_full, p["w_conv"], layer, buf_conv, emit_v)
    gla_ops = (z_head, la, p["g_gla"], s_gla, layer, s_out_prev)
    if t >= ROW_TILE:
        mix, (yb, s_out) = _mix_gla(mix_ops, gla_ops)
    else:
        mix = _mix_ac(*mix_ops)
        yb, s_out = _gla(*gla_ops)
    ya, yc, buf_new = mix[:3]
    vn = mix[3] if emit_v else None
    x, wb_out = _out_proj(ya, yb, yc, wts["w_out"], layer, x, mod3, 2)
    h2 = _norm_call(x, p["g_ffn"], mod3, 4, 3)
    a, wb_gate, wb_up = _swiglu(h2, wts["w_gate"], wts["w_up"], layer)
    x, wb_down = _down_proj(a, wts["w_down"], layer, x, mod3, 5)
    wb = dict(w_head=wb_head, w_tail=wb_tail, w_out=wb_out, w_gate=wb_gate, w_up=wb_up, w_down=wb_down)
    return x, s_out, buf_new, vn, wb


def kernel(x_prompt, x_sample, state_gla, state_conv, c_prompt, c_sample, g_mix, g_ffn, w_mod, b_mod,
           w_in, w_s, b_s, w_a2, b_a2, g_gla, w_conv, w_out, w_gate, w_up, w_down, g_final):
    bp = x_prompt.shape[0]
    bd = x_sample.shape[0]
    n_c = bp + bd
    mc = -(-n_c // 16) * 16
    c_all = jnp.concatenate([c_prompt, c_sample, jnp.zeros((mc - n_c, D_MODEL), F32)], axis=0)
    mod = _modulation(c_all, w_mod, b_mod)
    gf = g_final.reshape(1, D_MODEL)
    g_gla3 = g_gla.reshape(DEPTH, 1, DV_B)
    gla0 = jnp.zeros((DEPTH, bp, H_B, DK_B, DV_B), F32)
    conv0 = jnp.zeros((DEPTH, bp, CONV_W - 1, W_C), F32)

    xs, xp = x_sample, x_prompt
    gla_s = gla_p = None
    conv_s, conv_p, v_s = [], [], []
    wt_in = jnp.swapaxes(w_in, 1, 2)
    for l in range(DEPTH):
        w_a = wt_in[l, A_OFF:A_OFF + GATE_RANK, :].T
        p = dict(
            g_mix=g_mix[l].reshape(1, D_MODEL), g_ffn=g_ffn[l].reshape(1, D_MODEL),
            w_a=jnp.pad(w_a, ((0, 0), (0, LANES - GATE_RANK))),
            w_a2=jnp.pad(w_a2[l], ((0, LANES - GATE_RANK), (0, 0))).astype(BF16),
            b_a2=b_a2[l].reshape(1, KW_B),
            w_s=w_s[l], b_s=b_s[l], g_gla=g_gla3, w_conv=w_conv)
        w_f32 = dict(w_head=wt_in, w_tail=None, w_out=w_out,
                     w_gate=w_gate, w_up=w_up, w_down=w_down)
        mod_s = mod[l, bp:n_c].reshape(bd, 1, N_MOD * D_MODEL)
        mod_p = mod[l, :bp].reshape(bp, 1, N_MOD * D_MODEL)
        xs, gla_s, buf_s, vn_s, w_bf16 = _layer(xs, l, mod_s, state_gla, state_conv, p, w_f32, gla_s, True)
        xp, gla_p, buf_p, _, _ = _layer(xp, l, mod_p, gla0, conv0, p, w_bf16, gla_p, False)
        conv_s.append(buf_s)
        conv_p.append(buf_p)
        v_s.append(vn_s)
    y_s = _final_norm(xs, gf)
    y_p = _final_norm(xp, gf)
    return (y_p, y_s, gla_p, jnp.stack(conv_p), gla_s, jnp.stack(conv_s), jnp.stack(v_s))
```

```python
import functools

import numpy as np
import jax
import jax.numpy as jnp
from jax import lax
from jax.experimental import pallas as pl
from jax.experimental.pallas import tpu as pltpu

F32 = jnp.float32
BF16 = jnp.bfloat16

D_MODEL = 4096
DEPTH = 2
EPS = 1e-6
CHUNK = 128
W_A = D_MODEL // 4
G_A = 8
DH_A = W_A // G_A
W_B = D_MODEL // 2
H_B = 8
DV_B = W_B // H_B
DK_B = DV_B // 2
KW_B = H_B * DK_B
GATE_RANK = 16
GATE_TAU = 16.0
W_C = D_MODEL // 4
CONV_W = 3
D_FF = -(-8 * D_MODEL // (3 * 256)) * 256
N_MOD = 6
A_OFF = 2 * W_A + 2 * KW_B + 2 * W_B
P_TAIL = 3 * W_C

LOG2_E = float(np.log2(np.e))
LANES = 128
ROW_TILE = 128
MIN_REF_LEVEL = 4
TM = 1024
IN_TN = (512, 1024)
OUT_TN = (512, 1024)
FFN_TN = 256
DOWN_TN = 256
MOD_TN = 512
NORM_ROWS = 256
ROW_SPLIT = 2
T_AXIS = 1
PHASES = ("init", "body", "final")
VMEM_LIMIT = 56 * 1024 * 1024
DOWN_PROJ_VMEM_LIMIT = 62 * 1024 * 1024

ZQ0 = (2 * W_A) // DK_B
ZK0 = (2 * W_A + KW_B) // DK_B
ZV0 = (2 * W_A + 2 * KW_B) // DV_B
ZR0 = (2 * W_A + 2 * KW_B + W_B) // DV_B


def _cparams(sem):
    return pltpu.CompilerParams(dimension_semantics=sem, vmem_limit_bytes=VMEM_LIMIT)


def _silu(x):
    return x / (1.0 + jnp.exp(-x))


def _gelu_tanh(x):
    c = np.float32(np.sqrt(2.0 / np.pi))
    return 0.5 * x * (1.0 + jnp.tanh(c * (x + 0.044715 * (x * x * x))))


def _row_tiling(t, rows):
    if t >= rows:
        return 1, rows
    return rows // t, t


def _w_spec(w, layer, k, tn, col_map):
    if w.ndim == 2:
        return pl.BlockSpec((k, tn), lambda *g: (0, col_map(*g)))
    return pl.BlockSpec((None, k, tn), lambda *g: (layer, 0, col_map(*g)))


def _mod_kernel(c_ref, w_ref, b_ref, o_ref):
    s = _silu(c_ref[...]).astype(BF16)
    o_ref[0] = jnp.dot(s, w_ref[0].astype(BF16), preferred_element_type=F32) + b_ref[0]


def _modulation(c_all, w_mod, b_mod):
    mc = c_all.shape[0]
    n = N_MOD * D_MODEL
    tn = MOD_TN
    return pl.pallas_call(
        _mod_kernel,
        out_shape=jax.ShapeDtypeStruct((DEPTH, mc, n), F32),
        grid=(DEPTH, n // tn),
        in_specs=[
            pl.BlockSpec((mc, D_MODEL), lambda l, j: (0, 0)),
            pl.BlockSpec((1, D_MODEL, tn), lambda l, j: (l, 0, j)),
            pl.BlockSpec((1, 1, tn), lambda l, j: (l, 0, j)),
        ],
        out_specs=pl.BlockSpec((1, mc, tn), lambda l, j: (l, 0, j)),
        compiler_params=_cparams(("parallel", "parallel")),
        name="modulation",
    )(c_all, w_mod, b_mod.reshape(DEPTH, 1, n))


def _norm_mod(x, g, sc, sh):
    y = x * lax.rsqrt(jnp.mean(x * x, axis=-1, keepdims=True) + EPS) * g
    return y * (1.0 + sc) + sh


def _norm_gate_kernel(x_ref, g_ref, sc_ref, sh_ref, wa_ref, wa2_ref, ba2_ref, h_ref, la_ref):
    bs, tt, d = x_ref.shape
    h = _norm_mod(x_ref[...], g_ref[...], sc_ref[...], sh_ref[...]).reshape(bs * tt, d).astype(BF16)
    h_ref[...] = h
    a = jnp.dot(h, wa_ref[...].astype(BF16), preferred_element_type=F32)
    pre = jnp.dot(a.astype(BF16), wa2_ref[...], preferred_element_type=F32) + ba2_ref[...]
    log_sig = jnp.minimum(pre, 0.0) - jnp.log(1.0 + jnp.exp(-jnp.abs(pre)))
    la_ref[...] = (log_sig / GATE_TAU).reshape(bs, tt, KW_B)


def _norm_kernel(x_ref, g_ref, sc_ref, sh_ref, h_ref):
    bs, tt, d = x_ref.shape
    h_ref[...] = _norm_mod(x_ref[...], g_ref[...], sc_ref[...], sh_ref[...]).reshape(bs * tt, d).astype(BF16)


def _norm_call(x, g, mod3, sc_idx, sh_idx, gate_w=None):
    n_seq, t, d = x.shape
    bs, tt = _row_tiling(t, NORM_ROWS)
    rows = bs * tt
    nt = t // tt
    grid = (n_seq // bs, nt)
    x_spec = pl.BlockSpec((bs, tt, d), lambda i, j: (i, j, 0))
    g_spec = pl.BlockSpec((1, d), lambda i, j: (0, 0))
    mod_spec = lambda c: pl.BlockSpec((bs, 1, d), lambda i, j: (i, 0, c))
    h_spec = pl.BlockSpec((rows, d), lambda i, j: (i * nt + j, 0))
    h_shape = jax.ShapeDtypeStruct((n_seq * t, d), BF16)
    if gate_w is None:
        return pl.pallas_call(
            _norm_kernel, out_shape=h_shape, grid=grid,
            in_specs=[x_spec, g_spec, mod_spec(sc_idx), mod_spec(sh_idx)],
            out_specs=h_spec, compiler_params=_cparams(("parallel", "parallel")),
            name="norm_mod")(x, g, mod3, mod3)
    wa, wa2, ba2 = gate_w
    return pl.pallas_call(
        _norm_gate_kernel,
        out_shape=(h_shape, jax.ShapeDtypeStruct((n_seq, t, KW_B), F32)),
        grid=grid,
        in_specs=[x_spec, g_spec, mod_spec(sc_idx), mod_spec(sh_idx),
                  pl.BlockSpec((d, LANES), lambda i, j: (0, 0)),
                  pl.BlockSpec((LANES, KW_B), lambda i, j: (0, 0)),
                  pl.BlockSpec((1, KW_B), lambda i, j: (0, 0))],
        out_specs=(h_spec, pl.BlockSpec((bs, tt, KW_B), lambda i, j: (i, j, 0))),
        compiler_params=_cparams(("parallel", "parallel")),
        name="norm_mod_gate")(x, g, mod3, mod3, wa, wa2, ba2)


def _final_norm_kernel(x_ref, g_ref, o_ref):
    x = x_ref[...]
    o_ref[...] = x * lax.rsqrt(jnp.mean(x * x, axis=-1, keepdims=True) + EPS) * g_ref[...]


def _final_norm(x, g):
    n_seq, t, d = x.shape
    bs, tt = _row_tiling(t, NORM_ROWS)
    return pl.pallas_call(
        _final_norm_kernel, out_shape=jax.ShapeDtypeStruct(x.shape, F32),
        grid=(n_seq // bs, t // tt),
        in_specs=[pl.BlockSpec((bs, tt, d), lambda i, j: (i, j, 0)),
                  pl.BlockSpec((1, d), lambda i, j: (0, 0))],
        out_specs=pl.BlockSpec((bs, tt, d), lambda i, j: (i, j, 0)),
        compiler_params=_cparams(("parallel", "parallel")),
        name="final_norm")(x, g)


def _bf16_tile(w_ref, wb_ref):
    if wb_ref is None:
        return w_ref[...]
    wb = w_ref[...].astype(BF16)
    wb_ref[...] = wb
    return wb


def _in_proj_kernel(h_ref, wt_ref, z_ref, wtb_ref=None):
    acc = lax.dot_general(h_ref[...], _bf16_tile(wt_ref, wtb_ref), (((1,), (1,)), ((), ())),
                          preferred_element_type=F32)
    z_ref[...] = acc.reshape(z_ref.shape)


def _in_proj(h, wt, layer, n_cols, n_seq, t):
    rows, k = h.shape
    emit = wt.dtype != BF16
    bs, tt = _row_tiling(t, TM)
    nt = t // tt
    tn = IN_TN[0] if emit else IN_TN[1]
    if wt.ndim == 2:
        w_spec = pl.BlockSpec((tn, k), lambda i, j: (j, 0))
    else:
        w_spec = pl.BlockSpec((None, tn, k), lambda i, j: (layer, j, 0))
    out_shape = [jax.ShapeDtypeStruct((n_seq, t, n_cols), F32)]
    out_specs = [pl.BlockSpec((bs, tt, tn), lambda i, j: (i // nt, i % nt, j))]
    if emit:
        assert rows == TM
        out_shape.append(jax.ShapeDtypeStruct((n_cols, k), BF16))
        out_specs.append(pl.BlockSpec((tn, k), lambda i, j: (j, 0)))
    res = pl.pallas_call(
        _in_proj_kernel, out_shape=out_shape,
        grid=(rows // TM, n_cols // tn),
        in_specs=[pl.BlockSpec((TM, k), lambda i, j: (i, 0)), w_spec],
        out_specs=out_specs,
        compiler_params=_cparams(("parallel", "parallel")),
        name="in_proj_cast" if emit else "in_proj")(h, wt)
    return (res[0], res[1]) if emit else (res[0], None)


def _in_proj_tail_cast_kernel(h_ref, wa_ref, wb_ref, z_ref, wtb_ref):
    wt = jnp.concatenate([wa_ref[GATE_RANK:, :], wb_ref[...]], axis=0).astype(BF16)
    wtb_ref[...] = wt
    acc = lax.dot_general(h_ref[...], wt, (((1,), (1,)), ((), ())), preferred_element_type=F32)
    z_ref[...] = acc.reshape(z_ref.shape)


def _in_proj_tail_cast(h, wt_all, layer, n_seq, t):
    rows, k = h.shape
    assert rows == TM
    bs, tt = _row_tiling(t, TM)
    tn = IN_TN[0]
    return pl.pallas_call(
        _in_proj_tail_cast_kernel,
        out_shape=(jax.ShapeDtypeStruct((n_seq, t, P_TAIL), F32), jax.ShapeDtypeStruct((P_TAIL, k), BF16)),
        grid=(P_TAIL // tn,),
        in_specs=[pl.BlockSpec((TM, k), lambda j: (0, 0)),
                  pl.BlockSpec((None, tn, k), lambda j: (layer, A_OFF // tn + j, 0)),
                  pl.BlockSpec((None, GATE_RANK, k),
                               lambda j: (layer, (A_OFF + tn * (j + 1)) // GATE_RANK, 0))],
        out_specs=(pl.BlockSpec((bs, tt, tn), lambda j: (0, 0, j)),
                   pl.BlockSpec((tn, k), lambda j: (j, 0))),
        compiler_params=_cparams(("parallel",)),
        name="in_proj_tail_cast")(h, wt_all, wt_all)


def _out_proj_kernel(ya_ref, yb_ref, yc_ref, w_ref, x_ref, gt_ref, o_ref, wb_ref=None):
    if wb_ref is not None:
        wb_ref[...] = w_ref[...].astype(BF16)
        w_ref = wb_ref
    acc = jnp.dot(ya_ref[...], w_ref[0:W_A, :], preferred_element_type=F32)
    acc += jnp.dot(yb_ref[...], w_ref[W_A:W_A + W_B, :], preferred_element_type=F32)
    acc += jnp.dot(yc_ref[...], w_ref[W_A + W_B:, :], preferred_element_type=F32)
    o_ref[...] = x_ref[...] + gt_ref[...] * acc.reshape(o_ref.shape)


def _out_proj(ya, yb, yc, w, layer, x, mod3, gt_idx):
    n_seq, t, d = x.shape
    rows = n_seq * t
    emit = w.dtype != BF16
    bs, tt = _row_tiling(t, TM)
    nt = t // tt
    tn = OUT_TN[0] if emit else OUT_TN[1]
    xo_spec = pl.BlockSpec((bs, tt, tn), lambda i, j: (i // nt, i % nt, j))
    out_shape = [jax.ShapeDtypeStruct(x.shape, F32)]
    out_specs = [xo_spec]
    if emit:
        assert rows == TM
        out_shape.append(jax.ShapeDtypeStruct((d, d), BF16))
        out_specs.append(pl.BlockSpec((d, tn), lambda i, j: (0, j)))
    res = pl.pallas_call(
        _out_proj_kernel, out_shape=out_shape,
        grid=(rows // TM, d // tn),
        in_specs=[pl.BlockSpec((TM, W_A), lambda i, j: (i, 0)),
                  pl.BlockSpec((TM, W_B), lambda i, j: (i, 0)),
                  pl.BlockSpec((TM, W_C), lambda i, j: (i, 0)),
                  _w_spec(w, layer, d, tn, lambda i, j: j),
                  xo_spec,
                  pl.BlockSpec((bs, 1, tn), lambda i, j: (i // nt, 0, gt_idx * (d // tn) + j))],
        out_specs=out_specs,
        compiler_params=_cparams(("parallel", "parallel")),
        name="out_proj_cast" if emit else "out_proj")(ya, yb, yc, w, x, mod3)
    return (res[0], res[1]) if emit else (res[0], None)


def _swiglu_rows(h_ref, wgu, a_ref):
    rows = h_ref.shape[0]
    tn = wgu.shape[1] // 2
    part = rows // ROW_SPLIT
    for r0 in range(0, rows, part):
        gu = jnp.dot(h_ref[r0:r0 + part, :], wgu, preferred_element_type=F32)
        a_ref[r0:r0 + part, :] = (_silu(gu[:, :tn]) * gu[:, tn:]).astype(BF16)


def _swiglu_cast_kernel(h_ref, wg_ref, wu_ref, a_ref, wgu_ref):
    tn = wg_ref.shape[1]
    wgu_ref[:, :tn] = wg_ref[...].astype(BF16)
    wgu_ref[:, tn:] = wu_ref[...].astype(BF16)
    _swiglu_rows(h_ref, wgu_ref[...], a_ref)


def _swiglu_kernel(h_ref, wgu_ref, a_ref):
    _swiglu_rows(h_ref, wgu_ref[...], a_ref)


def _swiglu(h, w, layer):
    rows, k = h.shape
    tn = FFN_TN
    a_shape = jax.ShapeDtypeStruct((rows, D_FF), BF16)
    if isinstance(w, tuple):
        assert rows == TM
        wg, wu = w
        w_spec = pl.BlockSpec((None, k, tn), lambda j: (layer, 0, j))
        return pl.pallas_call(
            _swiglu_cast_kernel, out_shape=(a_shape, jax.ShapeDtypeStruct((k, 2 * D_FF), BF16)),
            grid=(D_FF // tn,),
            in_specs=[pl.BlockSpec((TM, k), lambda j: (0, 0)), w_spec, w_spec],
            out_specs=(pl.BlockSpec((TM, tn), lambda j: (0, j)), pl.BlockSpec((k, 2 * tn), lambda j: (0, j))),
            compiler_params=_cparams(("parallel",)),
            name="swiglu_cast")(h, wg, wu)
    tm = TM if rows % (2 * TM) else 2 * TM
    a = pl.pallas_call(
        _swiglu_kernel, out_shape=a_shape,
        grid=(rows // tm, D_FF // tn),
        in_specs=[pl.BlockSpec((tm, k), lambda i, j: (i, 0)),
                  pl.BlockSpec((k, 2 * tn), lambda i, j: (0, j))],
        out_specs=pl.BlockSpec((tm, tn), lambda i, j: (i, j)),
        compiler_params=_cparams(("parallel", "parallel")),
        name="swiglu")(h, w)
    return a, None


def _down_proj_cast_kernel(a_ref, w_ref, x_ref, gt_ref, o_ref, wb_ref, acc_ref):
    kk = pl.program_id(0)
    j = pl.program_id(1)
    part = jnp.dot(a_ref[...], _bf16_tile(w_ref, wb_ref), preferred_element_type=F32)

    @pl.when(kk == 0)
    def _():
        acc_ref[j] = part

    @pl.when(kk == 1)
    def _():
        o_ref[...] = x_ref[...] + gt_ref[...] * (acc_ref[j] + part).reshape(o_ref.shape)


def _down_proj_kernel(a_ref, w_ref, x_ref, gt_ref, o_ref):
    acc = jnp.dot(a_ref[...], w_ref[...], preferred_element_type=F32)
    o_ref[...] = x_ref[...] + gt_ref[...] * acc.reshape(o_ref.shape)


def _down_proj(a, w, layer, x, mod3, gt_idx):
    n_seq, t, d = x.shape
    rows, k = a.shape
    bs, tt = _row_tiling(t, TM)
    nt = t // tt
    tn = DOWN_TN
    n_j = d // tn
    if w.dtype == BF16:
        xo_spec = pl.BlockSpec((bs, tt, tn), lambda i, j: (i // nt, i % nt, j))
        out = pl.pallas_call(
            _down_proj_kernel, out_shape=jax.ShapeDtypeStruct(x.shape, F32),
            grid=(rows // TM, n_j),
            in_specs=[pl.BlockSpec((TM, k), lambda i, j: (i, 0)),
                      pl.BlockSpec((k, tn), lambda i, j: (0, j)),
                      xo_spec,
                      pl.BlockSpec((bs, 1, tn), lambda i, j: (i // nt, 0, gt_idx * n_j + j))],
            out_specs=xo_spec,
            compiler_params=pltpu.CompilerParams(dimension_semantics=("parallel", "parallel"),
                                                 vmem_limit_bytes=DOWN_PROJ_VMEM_LIMIT),
            name="down_proj")(a, w, x, mod3)
        return out, None
    assert rows == TM
    tk = k // 2
    xo_spec = pl.BlockSpec((bs, tt, tn), lambda kk, j: (0, 0, j * kk))
    out, wb = pl.pallas_call(
        _down_proj_cast_kernel,
        out_shape=(jax.ShapeDtypeStruct(x.shape, F32), jax.ShapeDtypeStruct((k, d), BF16)),
        grid=(2, n_j),
        in_specs=[pl.BlockSpec((TM, tk), lambda kk, j: (0, kk), pipeline_mode=pl.Buffered(1)),
                  pl.BlockSpec((None, tk, tn), lambda kk, j: (layer, kk, j)),
                  xo_spec,
                  pl.BlockSpec((bs, 1, tn), lambda kk, j: (0, 0, gt_idx * n_j + j * kk))],
        out_specs=(xo_spec, pl.BlockSpec((tk, tn), lambda kk, j: (kk, j))),
        scratch_shapes=[pltpu.VMEM((n_j, TM, tn), F32)],
        compiler_params=_cparams(("arbitrary", "arbitrary")),
        name="down_proj_cast")(a, w, x, mod3)
    return out, wb


def _mix_ac_kernel(u_ref, v_ref, b_ref, c_ref, hc_ref, wt_ref, bias_ref, wc_ref, buf_ref,
                   ya_ref, yc_ref, bufnew_ref, *rest, seq_len, phases=PHASES, defer_to=None):
    zbuf = rest[-1]
    vn_ref = rest[0] if len(rest) == 2 else None
    bs, tt, w = u_ref.shape
    rows = bs * tt
    t = pl.program_id(T_AXIS)
    pad = 8

    if "init" in phases:
        @pl.when(t == 0)
        def _():
            zbuf[:, pad - 2:pad, :] = buf_ref[...]

    if "body" in phases:
        work = _mix_ac_work(u_ref, v_ref, b_ref, c_ref, hc_ref, wt_ref, bias_ref, wc_ref, ya_ref, yc_ref,
                            vn_ref, zbuf, pad, seq_len)
        if defer_to is None:
            for piece in work:
                piece()
        else:
            defer_to.extend(work)

    if "final" in phases:
        @pl.when(t == pl.num_programs(T_AXIS) - 1)
        def _():
            bufnew_ref[...] = zbuf[:, pad - 2:pad, :]


def _mix_ac_work(u_ref, v_ref, b_ref, c_ref, hc_ref, wt_ref, bias_ref, wc_ref, ya_ref, yc_ref, vn_ref, zbuf,
                 pad, seq_len):
    bs, tt, _ = u_ref.shape
    rows = bs * tt

    def piece(g):
        sl = slice(g * DH_A, (g + 1) * DH_A)
        zc = c_ref[:, :, sl] * hc_ref[:, :, sl]
        zbuf[:, pad:pad + tt, sl] = zc
        y = zbuf[:, pad - 2:pad - 2 + tt, sl] * wc_ref[0:1, sl]
        y = y + zbuf[:, pad - 1:pad - 1 + tt, sl] * wc_ref[1:2, sl]
        y = y + zc * wc_ref[2:3, sl]
        yc_ref[:, sl] = (b_ref[:, :, sl] * y).reshape(rows, DH_A).astype(BF16)
        zbuf[:, pad - 2:pad, sl] = zbuf[:, pad + tt - 2:pad + tt, sl]

        gu = _gelu_tanh(u_ref[:, :, sl].reshape(rows, DH_A))
        vg = _gelu_tanh(v_ref[:, :, sl].reshape(rows, DH_A))
        dv = vg - jnp.mean(vg, axis=-1, keepdims=True)
        vn = dv * lax.rsqrt(jnp.mean(dv * dv, axis=-1, keepdims=True) + EPS)
        ri = lax.broadcasted_iota(jnp.int32, (rows, rows), 0)
        ci = lax.broadcasted_iota(jnp.int32, (rows, rows), 1)
        keep = ci <= ri
        if seq_len < rows:
            keep = keep & ((ri // seq_len) == (ci // seq_len))
        wm = jnp.where(keep, wt_ref[g], 0.0).astype(BF16)
        mixed = jnp.dot(wm, vn.astype(BF16), preferred_element_type=F32) + bias_ref[:, sl]
        ya_ref[:, sl] = (gu * mixed).astype(BF16)
        if vn_ref is not None:
            vn_ref[:, :, sl] = vn.reshape(bs, tt, DH_A)

    return [functools.partial(piece, g) for g in range(G_A)]


def _mix_ac_parts(z_head, z_tail, wt, bias_full, w_conv, layer, buf, emit_v):
    n_seq, t, _ = z_head.shape
    bs, tt = _row_tiling(t, ROW_TILE)
    nt = t // tt
    rows = n_seq * t
    zspec = lambda c: pl.BlockSpec((bs, tt, W_A), lambda i, j, *_: (i, j, c))
    y_spec = pl.BlockSpec((ROW_TILE, W_A), lambda i, j, *_: (i * nt + j, 0))
    out_shape = [jax.ShapeDtypeStruct((rows, W_A), BF16),
                 jax.ShapeDtypeStruct((rows, W_C), BF16),
                 jax.ShapeDtypeStruct((n_seq, CONV_W - 1, W_C), F32)]
    out_specs = [y_spec, y_spec, pl.BlockSpec((bs, CONV_W - 1, W_C), lambda i, j, *_: (i, 0, 0))]
    if emit_v:
        out_shape.append(jax.ShapeDtypeStruct((n_seq, t, W_A), F32))
        out_specs.append(pl.BlockSpec((bs, tt, W_A), lambda i, j, *_: (i, j, 0)))
    in_specs = [zspec(0), zspec(1), zspec(0), zspec(1), zspec(2),
                pl.BlockSpec((G_A, ROW_TILE, ROW_TILE), lambda *_: (0, 0, 0)),
                pl.BlockSpec((ROW_TILE, W_A), lambda *_: (0, 0)),
                pl.BlockSpec((None, CONV_W, W_C), lambda *_: (layer, 0, 0)),
                pl.BlockSpec((None, bs, CONV_W - 1, W_C), lambda i, *_: (layer, i, 0, 0))]
    return dict(args=[z_head, z_head, z_tail, z_tail, z_tail, wt, bias_full, w_conv, buf],
                in_specs=in_specs, out_shape=out_shape, out_specs=out_specs,
                scratch=[pltpu.VMEM((bs, tt + 8, W_C), F32)], grid=(n_seq // bs, nt),
                kernel=functools.partial(_mix_ac_kernel, seq_len=min(t, CHUNK)))


def _mix_ac(*operands):
    m = _mix_ac_parts(*operands)
    return pl.pallas_call(
        m["kernel"], out_shape=m["out_shape"], grid=m["grid"], in_specs=m["in_specs"],
        out_specs=m["out_specs"], scratch_shapes=m["scratch"],
        compiler_params=_cparams(("parallel", "arbitrary")), name="mix_ac")(*m["args"])


def _gla_tables(rows, seq_len):
    i = np.arange(rows)[:, None]
    t = np.arange(rows)[None, :]
    same = (i // seq_len) == (t // seq_len)
    slabs = [same & (t <= i)]
    masks = [i == t]
    s = seq_len // 2
    while s >= 1:
        second = (i // s) % 2 == 1
        start2 = (i // s) * s
        end1 = start2 + s - 1
        if s < MIN_REF_LEVEL:
            slabs.append(np.where(second, (t >= start2) & (t <= i), (t > i) & (t <= end1)))
        masks.append(((i // (2 * s)) == (t // (2 * s))) & second & ((t // s) % 2 == 0))
        s //= 2
    return (np.concatenate(slabs, 0).astype(np.float32), np.stack(masks).astype(np.float32))


def _block_ref_exponent(g, block, ref_row, flip_from):
    rows, dk = g.shape
    gr = g.reshape(rows // block, block, dk)
    d = gr - gr[:, ref_row:ref_row + 1, :]
    pos = lax.broadcasted_iota(jnp.int32, gr.shape, 1)
    return jnp.where(pos >= flip_from, d, -d).reshape(rows, dk)


def _gla_kernel(q_ref, k_ref, v_ref, r_ref, la_ref, mall_ref, mask_ref, g_ref, s0_ref, *rest, phases=PHASES,
                between=()):
    y_ref, snew_ref, s_scr = rest[-3:]
    t = pl.program_id(T_AXIS)

    if "init" in phases:
        @pl.when(t == 0)
        def _():
            s_scr[...] = s0_ref[...]

    if "body" in phases:
        _gla_body(q_ref, k_ref, v_ref, r_ref, la_ref, mall_ref, mask_ref, g_ref, y_ref, s_scr, between)

    if "final" in phases:
        @pl.when(t == pl.num_programs(T_AXIS) - 1)
        def _():
            snew_ref[...] = s_scr[...]


def _gla_body(q_ref, k_ref, v_ref, r_ref, la_ref, mall_ref, mask_ref, g_ref, y_ref, s_scr, between):
    bs, tt, _ = q_ref.shape
    hp = s_scr.shape[1]
    assert len(between) in (0, hp)
    rows = bs * tt
    n_lvl = mask_ref.shape[0] - 1
    ri = lax.broadcasted_iota(jnp.int32, (DK_B, DK_B), 0)
    ci = lax.broadcasted_iota(jnp.int32, (DK_B, DK_B), 1)
    eye = ri == ci
    nt_dims = (((1,), (1,)), ((), ()))
    tn_dims = (((0,), (0,)), ((), ()))

    for hh in range(hp):
        ks = slice(hh * DK_B, (hh + 1) * DK_B)
        vs = slice(hh * DV_B, (hh + 1) * DV_B)
        q = q_ref[:, :, ks].reshape(rows, DK_B) * (DK_B ** -0.5)
        k = k_ref[:, :, ks].reshape(rows, DK_B)
        v = v_ref[:, :, vs].reshape(rows, DV_B)
        vb = v.astype(BF16)
        la = la_ref[:, :, ks].reshape(rows, DK_B) * LOG2_E
        la_hi = la.astype(BF16)
        la_lo = (la - la_hi.astype(F32)).astype(BF16)
        x = jnp.dot(mall_ref[...], jnp.concatenate([la_hi, la_lo], axis=1), preferred_element_type=F32)
        x = x[:, :DK_B] + x[:, DK_B:]
        g = x[0:rows]

        a = mask_ref[0] * lax.dot_general(q.astype(BF16), k.astype(BF16), nt_dims,
                                          preferred_element_type=F32)
        n_slab = 1
        for lvl in range(n_lvl):
            s = tt >> (lvl + 1)
            if s >= MIN_REF_LEVEL:
                xl = _block_ref_exponent(g, 2 * s, s - 1, s)
            else:
                xl = x[n_slab * rows:(n_slab + 1) * rows]
                n_slab += 1
            el = jnp.exp2(xl)
            p = lax.dot_general((q * el).astype(BF16), (k * el).astype(BF16), nt_dims,
                                preferred_element_type=F32)
            a = a + mask_ref[lvl + 1] * p
        o_intra = jnp.dot(a.astype(BF16), vb, preferred_element_type=F32)
        e_g = jnp.exp2(g)
        qg = q * e_g
        kd = k * jnp.exp2(_block_ref_exponent(g, tt, tt - 1, tt))

        o_parts = []
        for b in range(bs):
            rs = slice(b * tt, (b + 1) * tt)
            s = s_scr[b, hh]
            o_parts.append(o_intra[rs] + jnp.dot(qg[rs].astype(BF16), s.astype(BF16),
                                                 preferred_element_type=F32))
            e_last = e_g[(b + 1) * tt - 1:(b + 1) * tt]
            e_col = jnp.sum(jnp.where(eye, jnp.broadcast_to(e_last, (DK_B, DK_B)), 0.0),
                            axis=1, keepdims=True)
            kv = lax.dot_general(kd[rs].astype(BF16), v[rs].astype(BF16), tn_dims,
                                 preferred_element_type=F32)
            s_scr[b, hh] = e_col * s + kv
        o = o_parts[0] if bs == 1 else jnp.concatenate(o_parts, axis=0)
        yn = o * lax.rsqrt(jnp.mean(o * o, axis=-1, keepdims=True) + EPS) * g_ref[...]
        y_ref[:, vs] = (yn * _silu(r_ref[:, :, vs].reshape(rows, DV_B))).astype(BF16)
        if between:
            between[hh]()


def _gla_parts(z_head, la, g_gla, s0, layer, s_out_prev):
    n_seq, t, _ = z_head.shape
    tile = ROW_TILE if t >= ROW_TILE else 2 * ROW_TILE
    bs, tt = _row_tiling(t, tile)
    hp = H_B if bs == 1 else 1
    nt = t // tt
    assert nt == 1 or hp == H_B
    rows = n_seq * t
    mall, masks = _gla_tables(tile, tt)
    mall = jnp.asarray(mall, BF16)
    masks = jnp.asarray(masks, F32)
    zs = lambda w, c0: pl.BlockSpec((bs, tt, hp * w), lambda i, j, h=0: (i, j, c0 // hp + h))
    s_spec = pl.BlockSpec((None, bs, hp, DK_B, DV_B), lambda i, j, h=0: (layer, i, h, 0, 0))
    in_specs = [zs(DK_B, ZQ0), zs(DK_B, ZK0), zs(DV_B, ZV0), zs(DV_B, ZR0),
                pl.BlockSpec((bs, tt, hp * DK_B), lambda i, j, h=0: (i, j, h)),
                pl.BlockSpec(mall.shape, lambda *_: (0, 0)),
                pl.BlockSpec(masks.shape, lambda *_: (0, 0, 0)),
                pl.BlockSpec((None, 1, DV_B), lambda *_: (layer, 0, 0)),
                s_spec]
    args = [z_head, z_head, z_head, z_head, la, mall, masks, g_gla, s0]
    if s_out_prev is not None:
        in_specs.append(pl.BlockSpec(memory_space=pl.ANY))
        args.append(s_out_prev)
    return dict(args=args, in_specs=in_specs, aliased=s_out_prev is not None,
                out_shape=[jax.ShapeDtypeStruct((rows, W_B), BF16),
                           jax.ShapeDtypeStruct((DEPTH, n_seq, H_B, DK_B, DV_B), F32)],
                out_specs=[pl.BlockSpec((tile, hp * DV_B), lambda i, j, h=0: (i * nt + j, h)), s_spec],
                scratch=[pltpu.VMEM((bs, hp, DK_B, DV_B), F32)], grid=(n_seq // bs, nt, H_B // hp))


def _gla(*operands):
    m = _gla_parts(*operands)
    aliases = {len(m["args"]) - 1: 1} if m["aliased"] else {}
    return pl.pallas_call(
        _gla_kernel, out_shape=m["out_shape"], grid=m["grid"], in_specs=m["in_specs"],
        out_specs=m["out_specs"], scratch_shapes=m["scratch"], input_output_aliases=aliases,
        compiler_params=_cparams(("parallel", "arbitrary", "parallel")), name="gla")(*m["args"])


def _mix_gla_kernel(*refs, n_in, n_out, mix_kernel):
    (mi, gi), (mo, go) = n_in, n_out
    o0 = mi + gi
    s0 = o0 + mo + go
    mix_refs = (*refs[:mi], *refs[o0:o0 + mo], refs[s0])
    gla_refs = (*refs[mi:o0], *refs[o0 + mo:s0], refs[s0 + 1])
    for phase in PHASES:
        pieces = []
        mix_kernel(*mix_refs, phases=(phase,), defer_to=pieces)
        _gla_kernel(*gla_refs, phases=(phase,), between=pieces)


def _mix_gla(mix_operands, gla_operands):
    m = _mix_ac_parts(*mix_operands)
    g = _gla_parts(*gla_operands)
    assert m["grid"] == g["grid"][:2] and g["grid"][2] == 1
    n_in = (len(m["args"]), len(g["args"]))
    n_out = (len(m["out_shape"]), len(g["out_shape"]))
    aliases = {sum(n_in) - 1: n_out[0] + 1} if g["aliased"] else {}
    res = pl.pallas_call(
        functools.partial(_mix_gla_kernel, n_in=n_in, n_out=n_out, mix_kernel=m["kernel"]),
        out_shape=m["out_shape"] + g["out_shape"], grid=m["grid"],
        in_specs=m["in_specs"] + g["in_specs"], out_specs=m["out_specs"] + g["out_specs"],
        scratch_shapes=m["scratch"] + g["scratch"], input_output_aliases=aliases,
        compiler_params=_cparams(("parallel", "arbitrary")), name="mix_gla")(*m["args"], *g["args"])
    return res[:n_out[0]], res[n_out[0]:]


def _layer(x, layer, mod3, s_gla, buf_conv, p, wts, s_out_prev, emit_v):
    n_seq, t, d = x.shape
    seq_len = min(t, CHUNK)
    reps = ROW_TILE // seq_len
    h, la = _norm_call(x, p["g_mix"], mod3, 1, 0, gate_w=(p["w_a"], p["w_a2"], p["b_a2"]))
    z_head, wb_head = _in_proj(h, wts["w_head"], layer, A_OFF, n_seq, t)
    if wts["w_tail"] is None:
        z_tail, wb_tail = _in_proj_tail_cast(h, wts["w_head"], layer, n_seq, t)
    else:
        z_tail, wb_tail = _in_proj(h, wts["w_tail"], layer, P_TAIL, n_seq, t)
    wt = jnp.tile(p["w_s"][:, :seq_len, :seq_len], (1, reps, reps))
    bias_full = jnp.repeat(jnp.tile(p["b_s"][:, :seq_len].T, (reps, 1)), DH_A, axis=1)
    mix_ops = (z_head, z_tail, wt, bias_full, p["w_conv"], layer, buf_conv, emit_v)
    gla_ops = (z_head, la, p["g_gla"], s_gla, layer, s_out_prev)
    if t >= ROW_TILE:
        mix, (yb, s_out) = _mix_gla(mix_ops, gla_ops)
    else:
        mix = _mix_ac(*mix_ops)
        yb, s_out = _gla(*gla_ops)
    ya, yc, buf_new = mix[:3]
    vn = mix[3] if emit_v else None
    x, wb_out = _out_proj(ya, yb, yc, wts["w_out"], layer, x, mod3, 2)
    h2 = _norm_call(x, p["g_ffn"], mod3, 4, 3)
    a, wb_ffn = _swiglu(h2, wts["w_ffn"], layer)
    x, wb_down = _down_proj(a, wts["w_down"], layer, x, mod3, 5)
    wb = dict(w_head=wb_head, w_tail=wb_tail, w_out=wb_out, w_ffn=wb_ffn, w_down=wb_down)
    return x, s_out, buf_new, vn, wb


def kernel(x_prompt, x_sample, state_gla, state_conv, c_prompt, c_sample, g_mix, g_ffn, w_mod, b_mod,
           w_in, w_s, b_s, w_a2, b_a2, g_gla, w_conv, w_out, w_gate, w_up, w_down, g_final):
    bp = x_prompt.shape[0]
    bd = x_sample.shape[0]
    n_c = bp + bd
    mc = -(-n_c // 16) * 16
    c_all = jnp.concatenate([c_prompt, c_sample, jnp.zeros((mc - n_c, D_MODEL), F32)], axis=0)
    mod = _modulation(c_all, w_mod, b_mod)
    gf = g_final.reshape(1, D_MODEL)
    g_gla3 = g_gla.reshape(DEPTH, 1, DV_B)
    gla0 = jnp.zeros((DEPTH, bp, H_B, DK_B, DV_B), F32)
    conv0 = jnp.zeros((DEPTH, bp, CONV_W - 1, W_C), F32)

    xs, xp = x_sample, x_prompt
    gla_s = gla_p = None
    conv_s, conv_p, v_s = [], [], []
    wt_in = jnp.swapaxes(w_in, 1, 2)
    for l in range(DEPTH):
        w_a = wt_in[l, A_OFF:A_OFF + GATE_RANK, :].T
        p = dict(
            g_mix=g_mix[l].reshape(1, D_MODEL), g_ffn=g_ffn[l].reshape(1, D_MODEL),
            w_a=jnp.pad(w_a, ((0, 0), (0, LANES - GATE_RANK))),
            w_a2=jnp.pad(w_a2[l], ((0, LANES - GATE_RANK), (0, 0))).astype(BF16),
            b_a2=b_a2[l].reshape(1, KW_B),
            w_s=w_s[l], b_s=b_s[l], g_gla=g_gla3, w_conv=w_conv)
        w_f32 = dict(w_head=wt_in, w_tail=None, w_out=w_out, w_ffn=(w_gate, w_up), w_down=w_down)
        mod_s = mod[l, bp:n_c].reshape(bd, 1, N_MOD * D_MODEL)
        mod_p = mod[l, :bp].reshape(bp, 1, N_MOD * D_MODEL)
        xs, gla_s, buf_s, vn_s, w_bf16 = _layer(xs, l, mod_s, state_gla, state_conv, p, w_f32, gla_s, True)
        xp, gla_p, buf_p, _, _ = _layer(xp, l, mod_p, gla0, conv0, p, w_bf16, gla_p, False)
        conv_s.append(buf_s)
        conv_p.append(buf_p)
        v_s.append(vn_s)
    y_s = _final_norm(xs, gf)
    y_p = _final_norm(xp, gf)
    return (y_p, y_s, gla_p, jnp.stack(conv_p), gla_s, jnp.stack(conv_s), jnp.stack(v_s))
```

```python
import functools

import numpy as np
import jax
import jax.numpy as jnp
from jax import lax
from jax.experimental import pallas as pl
from jax.experimental.pallas import tpu as pltpu

F32 = jnp.float32
BF16 = jnp.bfloat16

D_MODEL = 4096
DEPTH = 2
EPS = 1e-6
CHUNK = 128
W_A = D_MODEL // 4
G_A = 8
DH_A = W_A // G_A
W_B = D_MODEL // 2
H_B = 8
DV_B = W_B // H_B
DK_B = DV_B // 2
KW_B = H_B * DK_B
GATE_RANK = 16
GATE_TAU = 16.0
W_C = D_MODEL // 4
CONV_W = 3
D_FF = -(-8 * D_MODEL // (3 * 256)) * 256
N_MOD = 6
A_OFF = 2 * W_A + 2 * KW_B + 2 * W_B
P_TAIL = 3 * W_C

LOG2_E = float(np.log2(np.e))
LANES = 128
ROW_TILE = 128
MIN_REF_LEVEL = 4
N_SMALL_LEVELS = MIN_REF_LEVEL.bit_length() - 1
GLA_WINDOW = 32
TM = 1024
IN_TN = (512, 1024)
OUT_TN = (512, 1024)
FFN_TN = 256
DOWN_TN = 256
MOD_TN = 512
NORM_ROWS = 256
ROW_SPLIT = 2
T_AXIS = 1
PHASES = ("init", "body", "final")
VMEM_LIMIT = 56 * 1024 * 1024
DOWN_PROJ_VMEM_LIMIT = 62 * 1024 * 1024

ZQ0 = (2 * W_A) // DK_B
ZK0 = (2 * W_A + KW_B) // DK_B
ZV0 = (2 * W_A + 2 * KW_B) // DV_B
ZR0 = (2 * W_A + 2 * KW_B + W_B) // DV_B


def _cparams(sem):
    return pltpu.CompilerParams(dimension_semantics=sem, vmem_limit_bytes=VMEM_LIMIT)


def _silu(x):
    return x / (1.0 + jnp.exp(-x))


def _gelu_tanh(x):
    c = np.float32(np.sqrt(2.0 / np.pi))
    return 0.5 * x * (1.0 + jnp.tanh(c * (x + 0.044715 * (x * x * x))))


def _row_tiling(t, rows):
    if t >= rows:
        return 1, rows
    return rows // t, t


def _w_spec(w, layer, k, tn, col_map):
    if w.ndim == 2:
        return pl.BlockSpec((k, tn), lambda *g: (0, col_map(*g)))
    return pl.BlockSpec((None, k, tn), lambda *g: (layer, 0, col_map(*g)))


def _mod_kernel(c_ref, w_ref, b_ref, o_ref):
    s = _silu(c_ref[...]).astype(BF16)
    o_ref[0] = jnp.dot(s, w_ref[0].astype(BF16), preferred_element_type=F32) + b_ref[0]


def _modulation(c_all, w_mod, b_mod):
    mc = c_all.shape[0]
    n = N_MOD * D_MODEL
    tn = MOD_TN
    return pl.pallas_call(
        _mod_kernel,
        out_shape=jax.ShapeDtypeStruct((DEPTH, mc, n), F32),
        grid=(DEPTH, n // tn),
        in_specs=[
            pl.BlockSpec((mc, D_MODEL), lambda l, j: (0, 0)),
            pl.BlockSpec((1, D_MODEL, tn), lambda l, j: (l, 0, j)),
            pl.BlockSpec((1, 1, tn), lambda l, j: (l, 0, j)),
        ],
        out_specs=pl.BlockSpec((1, mc, tn), lambda l, j: (l, 0, j)),
        compiler_params=_cparams(("parallel", "parallel")),
        name="modulation",
    )(c_all, w_mod, b_mod.reshape(DEPTH, 1, n))


def _norm_mod(x, g, sc, sh):
    y = x * lax.rsqrt(jnp.mean(x * x, axis=-1, keepdims=True) + EPS) * g
    return y * (1.0 + sc) + sh


def _norm_gate_kernel(x_ref, g_ref, sc_ref, sh_ref, wa_ref, wa2_ref, ba2_ref, h_ref, la_ref):
    bs, tt, d = x_ref.shape
    h = _norm_mod(x_ref[...], g_ref[...], sc_ref[...], sh_ref[...]).reshape(bs * tt, d).astype(BF16)
    h_ref[...] = h
    a = jnp.dot(h, wa_ref[...].astype(BF16), preferred_element_type=F32)
    pre = jnp.dot(a.astype(BF16), wa2_ref[...], preferred_element_type=F32) + ba2_ref[...]
    log_sig = jnp.minimum(pre, 0.0) - jnp.log(1.0 + jnp.exp(-jnp.abs(pre)))
    la_ref[...] = (log_sig / GATE_TAU).reshape(bs, tt, KW_B)


def _norm_kernel(x_ref, g_ref, sc_ref, sh_ref, h_ref):
    bs, tt, d = x_ref.shape
    h_ref[...] = _norm_mod(x_ref[...], g_ref[...], sc_ref[...], sh_ref[...]).reshape(bs * tt, d).astype(BF16)


def _norm_call(x, g, mod3, sc_idx, sh_idx, gate_w=None):
    n_seq, t, d = x.shape
    bs, tt = _row_tiling(t, NORM_ROWS)
    rows = bs * tt
    nt = t // tt
    grid = (n_seq // bs, nt)
    x_spec = pl.BlockSpec((bs, tt, d), lambda i, j: (i, j, 0))
    g_spec = pl.BlockSpec((1, d), lambda i, j: (0, 0))
    mod_spec = lambda c: pl.BlockSpec((bs, 1, d), lambda i, j: (i, 0, c))
    h_spec = pl.BlockSpec((rows, d), lambda i, j: (i * nt + j, 0))
    h_shape = jax.ShapeDtypeStruct((n_seq * t, d), BF16)
    if gate_w is None:
        return pl.pallas_call(
            _norm_kernel, out_shape=h_shape, grid=grid,
            in_specs=[x_spec, g_spec, mod_spec(sc_idx), mod_spec(sh_idx)],
            out_specs=h_spec, compiler_params=_cparams(("parallel", "parallel")),
            name="norm_mod")(x, g, mod3, mod3)
    wa, wa2, ba2 = gate_w
    return pl.pallas_call(
        _norm_gate_kernel,
        out_shape=(h_shape, jax.ShapeDtypeStruct((n_seq, t, KW_B), F32)),
        grid=grid,
        in_specs=[x_spec, g_spec, mod_spec(sc_idx), mod_spec(sh_idx),
                  pl.BlockSpec((d, LANES), lambda i, j: (0, 0)),
                  pl.BlockSpec((LANES, KW_B), lambda i, j: (0, 0)),
                  pl.BlockSpec((1, KW_B), lambda i, j: (0, 0))],
        out_specs=(h_spec, pl.BlockSpec((bs, tt, KW_B), lambda i, j: (i, j, 0))),
        compiler_params=_cparams(("parallel", "parallel")),
        name="norm_mod_gate")(x, g, mod3, mod3, wa, wa2, ba2)


def _final_norm_kernel(x_ref, g_ref, o_ref):
    x = x_ref[...]
    o_ref[...] = x * lax.rsqrt(jnp.mean(x * x, axis=-1, keepdims=True) + EPS) * g_ref[...]


def _final_norm(x, g):
    n_seq, t, d = x.shape
    bs, tt = _row_tiling(t, NORM_ROWS)
    return pl.pallas_call(
        _final_norm_kernel, out_shape=jax.ShapeDtypeStruct(x.shape, F32),
        grid=(n_seq // bs, t // tt),
        in_specs=[pl.BlockSpec((bs, tt, d), lambda i, j: (i, j, 0)),
                  pl.BlockSpec((1, d), lambda i, j: (0, 0))],
        out_specs=pl.BlockSpec((bs, tt, d), lambda i, j: (i, j, 0)),
        compiler_params=_cparams(("parallel", "parallel")),
        name="final_norm")(x, g)


def _bf16_tile(w_ref, wb_ref):
    if wb_ref is None:
        return w_ref[...]
    wb = w_ref[...].astype(BF16)
    wb_ref[...] = wb
    return wb


def _in_proj_kernel(h_ref, wt_ref, z_ref, wtb_ref=None):
    acc = lax.dot_general(h_ref[...], _bf16_tile(wt_ref, wtb_ref), (((1,), (1,)), ((), ())),
                          preferred_element_type=F32)
    z_ref[...] = acc.reshape(z_ref.shape)


def _in_proj(h, wt, layer, n_cols, n_seq, t):
    rows, k = h.shape
    emit = wt.dtype != BF16
    bs, tt = _row_tiling(t, TM)
    nt = t // tt
    tn = IN_TN[0] if emit else IN_TN[1]
    if wt.ndim == 2:
        w_spec = pl.BlockSpec((tn, k), lambda i, j: (j, 0))
    else:
        w_spec = pl.BlockSpec((None, tn, k), lambda i, j: (layer, j, 0))
    out_shape = [jax.ShapeDtypeStruct((n_seq, t, n_cols), F32)]
    out_specs = [pl.BlockSpec((bs, tt, tn), lambda i, j: (i // nt, i % nt, j))]
    if emit:
        assert rows == TM
        out_shape.append(jax.ShapeDtypeStruct((n_cols, k), BF16))
        out_specs.append(pl.BlockSpec((tn, k), lambda i, j: (j, 0)))
    res = pl.pallas_call(
        _in_proj_kernel, out_shape=out_shape,
        grid=(rows // TM, n_cols // tn),
        in_specs=[pl.BlockSpec((TM, k), lambda i, j: (i, 0)), w_spec],
        out_specs=out_specs,
        compiler_params=_cparams(("parallel", "parallel")),
        name="in_proj_cast" if emit else "in_proj")(h, wt)
    return (res[0], res[1]) if emit else (res[0], None)


def _in_proj_tail_cast_kernel(h_ref, wa_ref, wb_ref, z_ref, wtb_ref):
    wt = jnp.concatenate([wa_ref[GATE_RANK:, :], wb_ref[...]], axis=0).astype(BF16)
    wtb_ref[...] = wt
    acc = lax.dot_general(h_ref[...], wt, (((1,), (1,)), ((), ())), preferred_element_type=F32)
    z_ref[...] = acc.reshape(z_ref.shape)


def _in_proj_tail_cast(h, wt_all, layer, n_seq, t):
    rows, k = h.shape
    assert rows == TM
    bs, tt = _row_tiling(t, TM)
    tn = IN_TN[0]
    return pl.pallas_call(
        _in_proj_tail_cast_kernel,
        out_shape=(jax.ShapeDtypeStruct((n_seq, t, P_TAIL), F32), jax.ShapeDtypeStruct((P_TAIL, k), BF16)),
        grid=(P_TAIL // tn,),
        in_specs=[pl.BlockSpec((TM, k), lambda j: (0, 0)),
                  pl.BlockSpec((None, tn, k), lambda j: (layer, A_OFF // tn + j, 0)),
                  pl.BlockSpec((None, GATE_RANK, k),
                               lambda j: (layer, (A_OFF + tn * (j + 1)) // GATE_RANK, 0))],
        out_specs=(pl.BlockSpec((bs, tt, tn), lambda j: (0, 0, j)),
                   pl.BlockSpec((tn, k), lambda j: (j, 0))),
        compiler_params=_cparams(("parallel",)),
        name="in_proj_tail_cast")(h, wt_all, wt_all)


def _out_proj_kernel(ya_ref, yb_ref, yc_ref, w_ref, x_ref, gt_ref, o_ref, wb_ref=None):
    if wb_ref is not None:
        wb_ref[...] = w_ref[...].astype(BF16)
        w_ref = wb_ref
    acc = jnp.dot(ya_ref[...], w_ref[0:W_A, :], preferred_element_type=F32)
    acc += jnp.dot(yb_ref[...], w_ref[W_A:W_A + W_B, :], preferred_element_type=F32)
    acc += jnp.dot(yc_ref[...], w_ref[W_A + W_B:, :], preferred_element_type=F32)
    o_ref[...] = x_ref[...] + gt_ref[...] * acc.reshape(o_ref.shape)


def _out_proj(ya, yb, yc, w, layer, x, mod3, gt_idx):
    n_seq, t, d = x.shape
    rows = n_seq * t
    emit = w.dtype != BF16
    bs, tt = _row_tiling(t, TM)
    nt = t // tt
    tn = OUT_TN[0] if emit else OUT_TN[1]
    xo_spec = pl.BlockSpec((bs, tt, tn), lambda i, j: (i // nt, i % nt, j))
    out_shape = [jax.ShapeDtypeStruct(x.shape, F32)]
    out_specs = [xo_spec]
    if emit:
        assert rows == TM
        out_shape.append(jax.ShapeDtypeStruct((d, d), BF16))
        out_specs.append(pl.BlockSpec((d, tn), lambda i, j: (0, j)))
    res = pl.pallas_call(
        _out_proj_kernel, out_shape=out_shape,
        grid=(rows // TM, d // tn),
        in_specs=[pl.BlockSpec((TM, W_A), lambda i, j: (i, 0)),
                  pl.BlockSpec((TM, W_B), lambda i, j: (i, 0)),
                  pl.BlockSpec((TM, W_C), lambda i, j: (i, 0)),
                  _w_spec(w, layer, d, tn, lambda i, j: j),
                  xo_spec,
                  pl.BlockSpec((bs, 1, tn), lambda i, j: (i // nt, 0, gt_idx * (d // tn) + j))],
        out_specs=out_specs,
        compiler_params=_cparams(("parallel", "parallel")),
        name="out_proj_cast" if emit else "out_proj")(ya, yb, yc, w, x, mod3)
    return (res[0], res[1]) if emit else (res[0], None)


def _swiglu_kernel(h_ref, wg_ref, wu_ref, a_ref, wgb_ref=None, wub_ref=None):
    wg = _bf16_tile(wg_ref, wgb_ref)
    wu = _bf16_tile(wu_ref, wub_ref)
    rows = h_ref.shape[0]
    part = rows // ROW_SPLIT
    for r0 in range(0, rows, part):
        h = h_ref[r0:r0 + part, :]
        g = jnp.dot(h, wg, preferred_element_type=F32)
        u = jnp.dot(h, wu, preferred_element_type=F32)
        a_ref[r0:r0 + part, :] = (_silu(g) * u).astype(BF16)


def _swiglu(h, wg, wu, layer):
    rows, k = h.shape
    n = wg.shape[-1]
    emit = wg.dtype != BF16
    tn = FFN_TN
    tm = TM if emit or rows % (2 * TM) else 2 * TM
    out_shape = [jax.ShapeDtypeStruct((rows, n), BF16)]
    out_specs = [pl.BlockSpec((tm, tn), lambda i, j: (i, j))]
    if emit:
        assert rows == TM
        out_shape += [jax.ShapeDtypeStruct((k, n), BF16)] * 2
        out_specs += [pl.BlockSpec((k, tn), lambda i, j: (0, j))] * 2
    res = pl.pallas_call(
        _swiglu_kernel, out_shape=out_shape,
        grid=(rows // tm, n // tn),
        in_specs=[pl.BlockSpec((tm, k), lambda i, j: (i, 0)),
                  _w_spec(wg, layer, k, tn, lambda i, j: j),
                  _w_spec(wu, layer, k, tn, lambda i, j: j)],
        out_specs=out_specs,
        compiler_params=_cparams(("parallel", "parallel")),
        name="swiglu_cast" if emit else "swiglu")(h, wg, wu)
    return (res[0], res[1], res[2]) if emit else (res[0], None, None)


def _down_proj_cast_kernel(a_ref, w_ref, x_ref, gt_ref, o_ref, wb_ref, acc_ref):
    kk = pl.program_id(0)
    j = pl.program_id(1)
    part = jnp.dot(a_ref[...], _bf16_tile(w_ref, wb_ref), preferred_element_type=F32)

    @pl.when(kk == 0)
    def _():
        acc_ref[j] = part

    @pl.when(kk == 1)
    def _():
        o_ref[...] = x_ref[...] + gt_ref[...] * (acc_ref[j] + part).reshape(o_ref.shape)


def _down_proj_kernel(a_ref, w_ref, x_ref, gt_ref, o_ref):
    acc = jnp.dot(a_ref[...], w_ref[...], preferred_element_type=F32)
    o_ref[...] = x_ref[...] + gt_ref[...] * acc.reshape(o_ref.shape)


def _down_proj(a, w, layer, x, mod3, gt_idx):
    n_seq, t, d = x.shape
    rows, k = a.shape
    bs, tt = _row_tiling(t, TM)
    nt = t // tt
    tn = DOWN_TN
    n_j = d // tn
    if w.dtype == BF16:
        xo_spec = pl.BlockSpec((bs, tt, tn), lambda i, j: (i // nt, i % nt, j))
        out = pl.pallas_call(
            _down_proj_kernel, out_shape=jax.ShapeDtypeStruct(x.shape, F32),
            grid=(rows // TM, n_j),
            in_specs=[pl.BlockSpec((TM, k), lambda i, j: (i, 0)),
                      pl.BlockSpec((k, tn), lambda i, j: (0, j)),
                      xo_spec,
                      pl.BlockSpec((bs, 1, tn), lambda i, j: (i // nt, 0, gt_idx * n_j + j))],
            out_specs=xo_spec,
            compiler_params=pltpu.CompilerParams(dimension_semantics=("parallel", "parallel"),
                                                 vmem_limit_bytes=DOWN_PROJ_VMEM_LIMIT),
            name="down_proj")(a, w, x, mod3)
        return out, None
    assert rows == TM
    tk = k // 2
    xo_spec = pl.BlockSpec((bs, tt, tn), lambda kk, j: (0, 0, j * kk))
    out, wb = pl.pallas_call(
        _down_proj_cast_kernel,
        out_shape=(jax.ShapeDtypeStruct(x.shape, F32), jax.ShapeDtypeStruct((k, d), BF16)),
        grid=(2, n_j),
        in_specs=[pl.BlockSpec((TM, tk), lambda kk, j: (0, kk), pipeline_mode=pl.Buffered(1)),
                  pl.BlockSpec((None, tk, tn), lambda kk, j: (layer, kk, j)),
                  xo_spec,
                  pl.BlockSpec((bs, 1, tn), lambda kk, j: (0, 0, gt_idx * n_j + j * kk))],
        out_specs=(xo_spec, pl.BlockSpec((tk, tn), lambda kk, j: (kk, j))),
        scratch_shapes=[pltpu.VMEM((n_j, TM, tn), F32)],
        compiler_params=_cparams(("arbitrary", "arbitrary")),
        name="down_proj_cast")(a, w, x, mod3)
    return out, wb


def _mix_ac_kernel(u_ref, v_ref, b_ref, c_ref, hc_ref, wt_ref, bias_ref, wc_ref, buf_ref,
                   ya_ref, yc_ref, bufnew_ref, *rest, seq_len, phases=PHASES, defer_to=None):
    zbuf = rest[-1]
    vn_ref = rest[0] if len(rest) == 2 else None
    bs, tt, w = u_ref.shape
    rows = bs * tt
    t = pl.program_id(T_AXIS)
    pad = 8

    if "init" in phases:
        @pl.when(t == 0)
        def _():
            zbuf[:, pad - 2:pad, :] = buf_ref[...]

    if "body" in phases:
        work = _mix_ac_work(u_ref, v_ref, b_ref, c_ref, hc_ref, wt_ref, bias_ref, wc_ref, ya_ref, yc_ref,
                            vn_ref, zbuf, pad, seq_len)
        if defer_to is None:
            for piece in work:
                piece()
        else:
            defer_to.extend(work)

    if "final" in phases:
        @pl.when(t == pl.num_programs(T_AXIS) - 1)
        def _():
            bufnew_ref[...] = zbuf[:, pad - 2:pad, :]


def _mix_ac_work(u_ref, v_ref, b_ref, c_ref, hc_ref, wt_ref, bias_ref, wc_ref, ya_ref, yc_ref, vn_ref, zbuf,
                 pad, seq_len):
    bs, tt, _ = u_ref.shape
    rows = bs * tt

    def piece(g):
        sl = slice(g * DH_A, (g + 1) * DH_A)
        zc = c_ref[:, :, sl] * hc_ref[:, :, sl]
        zbuf[:, pad:pad + tt, sl] = zc
        y = zbuf[:, pad - 2:pad - 2 + tt, sl] * wc_ref[0:1, sl]
        y = y + zbuf[:, pad - 1:pad - 1 + tt, sl] * wc_ref[1:2, sl]
        y = y + zc * wc_ref[2:3, sl]
        yc_ref[:, sl] = (b_ref[:, :, sl] * y).reshape(rows, DH_A).astype(BF16)
        zbuf[:, pad - 2:pad, sl] = zbuf[:, pad + tt - 2:pad + tt, sl]

        gu = _gelu_tanh(u_ref[:, :, sl].reshape(rows, DH_A))
        vg = _gelu_tanh(v_ref[:, :, sl].reshape(rows, DH_A))
        dv = vg - jnp.mean(vg, axis=-1, keepdims=True)
        vn = dv * lax.rsqrt(jnp.mean(dv * dv, axis=-1, keepdims=True) + EPS)
        ri = lax.broadcasted_iota(jnp.int32, (rows, rows), 0)
        ci = lax.broadcasted_iota(jnp.int32, (rows, rows), 1)
        keep = ci <= ri
        if seq_len < rows:
            keep = keep & ((ri // seq_len) == (ci // seq_len))
        wm = jnp.where(keep, wt_ref[g], 0.0).astype(BF16)
        mixed = jnp.dot(wm, vn.astype(BF16), preferred_element_type=F32) + bias_ref[:, sl]
        ya_ref[:, sl] = (gu * mixed).astype(BF16)
        if vn_ref is not None:
            vn_ref[:, :, sl] = vn.reshape(bs, tt, DH_A)

    return [functools.partial(piece, g) for g in range(G_A)]


def _mix_ac_parts(z_head, z_tail, wt, bias_full, w_conv, layer, buf, emit_v):
    n_seq, t, _ = z_head.shape
    bs, tt = _row_tiling(t, ROW_TILE)
    nt = t // tt
    rows = n_seq * t
    zspec = lambda c: pl.BlockSpec((bs, tt, W_A), lambda i, j, *_: (i, j, c))
    y_spec = pl.BlockSpec((ROW_TILE, W_A), lambda i, j, *_: (i * nt + j, 0))
    out_shape = [jax.ShapeDtypeStruct((rows, W_A), BF16),
                 jax.ShapeDtypeStruct((rows, W_C), BF16),
                 jax.ShapeDtypeStruct((n_seq, CONV_W - 1, W_C), F32)]
    out_specs = [y_spec, y_spec, pl.BlockSpec((bs, CONV_W - 1, W_C), lambda i, j, *_: (i, 0, 0))]
    if emit_v:
        out_shape.append(jax.ShapeDtypeStruct((n_seq, t, W_A), F32))
        out_specs.append(pl.BlockSpec((bs, tt, W_A), lambda i, j, *_: (i, j, 0)))
    in_specs = [zspec(0), zspec(1), zspec(0), zspec(1), zspec(2),
                pl.BlockSpec((G_A, ROW_TILE, ROW_TILE), lambda *_: (0, 0, 0)),
                pl.BlockSpec((ROW_TILE, W_A), lambda *_: (0, 0)),
                pl.BlockSpec((None, CONV_W, W_C), lambda *_: (layer, 0, 0)),
                pl.BlockSpec((None, bs, CONV_W - 1, W_C), lambda i, *_: (layer, i, 0, 0))]
    return dict(args=[z_head, z_head, z_tail, z_tail, z_tail, wt, bias_full, w_conv, buf],
                in_specs=in_specs, out_shape=out_shape, out_specs=out_specs,
                scratch=[pltpu.VMEM((bs, tt + 8, W_C), F32)], grid=(n_seq // bs, nt),
                kernel=functools.partial(_mix_ac_kernel, seq_len=min(t, CHUNK)))


def _mix_ac(*operands):
    m = _mix_ac_parts(*operands)
    return pl.pallas_call(
        m["kernel"], out_shape=m["out_shape"], grid=m["grid"], in_specs=m["in_specs"],
        out_specs=m["out_specs"], scratch_shapes=m["scratch"],
        compiler_params=_cparams(("parallel", "arbitrary")), name="mix_ac")(*m["args"])


def _gla_tables(rows, seq_len):
    i = np.arange(rows)[:, None]
    t = np.arange(rows)[None, :]
    window = min(seq_len, GLA_WINDOW)
    slabs = [((i // window) == (t // window)) & (t <= i)]
    masks = [i == t]
    s = seq_len // 2
    while s >= 1:
        second = (i // s) % 2 == 1
        start2 = (i // s) * s
        end1 = start2 + s - 1
        if s < MIN_REF_LEVEL:
            slabs.append(np.where(second, (t >= start2) & (t <= i), (t > i) & (t <= end1)))
        masks.append(((i // (2 * s)) == (t // (2 * s))) & second & ((t // s) % 2 == 0))
        s //= 2
    return (np.concatenate(slabs, 0).astype(np.float32), np.stack(masks).astype(np.float32))


def _block_ref_exponent(g, block, ref_row, flip_from):
    rows, dk = g.shape
    gr = g.reshape(rows // block, block, dk)
    d = gr - gr[:, ref_row:ref_row + 1, :]
    pos = lax.broadcasted_iota(jnp.int32, gr.shape, 1)
    return jnp.where(pos >= flip_from, d, -d).reshape(rows, dk)


def _double_restart(pre, suf, s):
    rows, dk = pre.shape
    pr = pre.reshape(rows // (2 * s), 2 * s, dk)
    sr = suf.reshape(rows // (2 * s), 2 * s, dk)
    second = lax.broadcasted_iota(jnp.int32, pr.shape, 1) >= s
    x = jnp.where(second, pr, sr)
    pre2 = jnp.where(second, pr + pr[:, s - 1:s, :], pr)
    suf2 = jnp.where(second, sr, sr + pr[:, 2 * s - 1:2 * s, :])
    return x.reshape(rows, dk), pre2.reshape(rows, dk), suf2.reshape(rows, dk)


def _gla_kernel(q_ref, k_ref, v_ref, r_ref, la_ref, mall_ref, mask_ref, g_ref, s0_ref, *rest, phases=PHASES,
                between=()):
    y_ref, snew_ref, s_scr = rest[-3:]
    t = pl.program_id(T_AXIS)

    if "init" in phases:
        @pl.when(t == 0)
        def _():
            s_scr[...] = s0_ref[...]

    if "body" in phases:
        _gla_body(q_ref, k_ref, v_ref, r_ref, la_ref, mall_ref, mask_ref, g_ref, y_ref, s_scr, between)

    if "final" in phases:
        @pl.when(t == pl.num_programs(T_AXIS) - 1)
        def _():
            snew_ref[...] = s_scr[...]


def _gla_body(q_ref, k_ref, v_ref, r_ref, la_ref, mall_ref, mask_ref, g_ref, y_ref, s_scr, between):
    bs, tt, _ = q_ref.shape
    hp = s_scr.shape[1]
    assert len(between) in (0, hp)
    rows = bs * tt
    n_lvl = mask_ref.shape[0] - 1
    ri = lax.broadcasted_iota(jnp.int32, (DK_B, DK_B), 0)
    ci = lax.broadcasted_iota(jnp.int32, (DK_B, DK_B), 1)
    eye = ri == ci
    nt_dims = (((1,), (1,)), ((), ()))
    tn_dims = (((0,), (0,)), ((), ()))

    for hh in range(hp):
        ks = slice(hh * DK_B, (hh + 1) * DK_B)
        vs = slice(hh * DV_B, (hh + 1) * DV_B)
        q = q_ref[:, :, ks].reshape(rows, DK_B) * (DK_B ** -0.5)
        k = k_ref[:, :, ks].reshape(rows, DK_B)
        v = v_ref[:, :, vs].reshape(rows, DV_B)
        vb = v.astype(BF16)
        la = la_ref[:, :, ks].reshape(rows, DK_B) * LOG2_E
        la_hi = la.astype(BF16)
        la_lo = (la - la_hi.astype(F32)).astype(BF16)
        x = jnp.dot(mall_ref[...], jnp.concatenate([la_hi, la_lo], axis=1), preferred_element_type=F32)
        x = x[:, :DK_B] + x[:, DK_B:]
        window = min(tt, GLA_WINDOW)
        pre = x[0:rows]
        suf = _block_ref_exponent(pre, window, window - 1, window)

        a = mask_ref[0] * lax.dot_general(q.astype(BF16), k.astype(BF16), nt_dims,
                                          preferred_element_type=F32)
        for lvl in reversed(range(n_lvl)):
            s = tt >> (lvl + 1)
            if s >= window:
                xl, pre, suf = _double_restart(pre, suf, s)
            elif s >= MIN_REF_LEVEL:
                xl = _block_ref_exponent(pre, 2 * s, s - 1, s)
            else:
                n_slab = 1 + lvl - (n_lvl - N_SMALL_LEVELS)
                xl = x[n_slab * rows:(n_slab + 1) * rows]
            el = jnp.exp2(xl)
            p = lax.dot_general((q * el).astype(BF16), (k * el).astype(BF16), nt_dims,
                                preferred_element_type=F32)
            a = a + mask_ref[lvl + 1] * p
        o_intra = jnp.dot(a.astype(BF16), vb, preferred_element_type=F32)
        e_g = jnp.exp2(pre)
        qg = q * e_g
        kd = k * jnp.exp2(suf)

        o_parts = []
        for b in range(bs):
            rs = slice(b * tt, (b + 1) * tt)
            s = s_scr[b, hh]
            o_parts.append(o_intra[rs] + jnp.dot(qg[rs].astype(BF16), s.astype(BF16),
                                                 preferred_element_type=F32))
            e_last = e_g[(b + 1) * tt - 1:(b + 1) * tt]
            e_col = jnp.sum(jnp.where(eye, jnp.broadcast_to(e_last, (DK_B, DK_B)), 0.0),
                            axis=1, keepdims=True)
            kv = lax.dot_general(kd[rs].astype(BF16), v[rs].astype(BF16), tn_dims,
                                 preferred_element_type=F32)
            s_scr[b, hh] = e_col * s + kv
        o = o_parts[0] if bs == 1 else jnp.concatenate(o_parts, axis=0)
        yn = o * lax.rsqrt(jnp.mean(o * o, axis=-1, keepdims=True) + EPS) * g_ref[...]
        y_ref[:, vs] = (yn * _silu(r_ref[:, :, vs].reshape(rows, DV_B))).astype(BF16)
        if between:
            between[hh]()


def _gla_parts(z_head, la, g_gla, s0, layer, s_out_prev):
    n_seq, t, _ = z_head.shape
    tile = ROW_TILE if t >= ROW_TILE else 2 * ROW_TILE
    bs, tt = _row_tiling(t, tile)
    hp = H_B if bs == 1 else 1
    nt = t // tt
    assert nt == 1 or hp == H_B
    rows = n_seq * t
    mall, masks = _gla_tables(tile, tt)
    mall = jnp.asarray(mall, BF16)
    masks = jnp.asarray(masks, F32)
    zs = lambda w, c0: pl.BlockSpec((bs, tt, hp * w), lambda i, j, h=0: (i, j, c0 // hp + h))
    s_spec = pl.BlockSpec((None, bs, hp, DK_B, DV_B), lambda i, j, h=0: (layer, i, h, 0, 0))
    in_specs = [zs(DK_B, ZQ0), zs(DK_B, ZK0), zs(DV_B, ZV0), zs(DV_B, ZR0),
                pl.BlockSpec((bs, tt, hp * DK_B), lambda i, j, h=0: (i, j, h)),
                pl.BlockSpec(mall.shape, lambda *_: (0, 0)),
                pl.BlockSpec(masks.shape, lambda *_: (0, 0, 0)),
                pl.BlockSpec((None, 1, DV_B), lambda *_: (layer, 0, 0)),
                s_spec]
    args = [z_head, z_head, z_head, z_head, la, mall, masks, g_gla, s0]
    if s_out_prev is not None:
        in_specs.append(pl.BlockSpec(memory_space=pl.ANY))
        args.append(s_out_prev)
    return dict(args=args, in_specs=in_specs, aliased=s_out_prev is not None,
                out_shape=[jax.ShapeDtypeStruct((rows, W_B), BF16),
                           jax.ShapeDtypeStruct((DEPTH, n_seq, H_B, DK_B, DV_B), F32)],
                out_specs=[pl.BlockSpec((tile, hp * DV_B), lambda i, j, h=0: (i * nt + j, h)), s_spec],
                scratch=[pltpu.VMEM((bs, hp, DK_B, DV_B), F32)], grid=(n_seq // bs, nt, H_B // hp))


def _gla(*operands):
    m = _gla_parts(*operands)
    aliases = {len(m["args"]) - 1: 1} if m["aliased"] else {}
    return pl.pallas_call(
        _gla_kernel, out_shape=m["out_shape"], grid=m["grid"], in_specs=m["in_specs"],
        out_specs=m["out_specs"], scratch_shapes=m["scratch"], input_output_aliases=aliases,
        compiler_params=_cparams(("parallel", "arbitrary", "parallel")), name="gla")(*m["args"])


def _mix_gla_kernel(*refs, n_in, n_out, mix_kernel):
    (mi, gi), (mo, go) = n_in, n_out
    o0 = mi + gi
    s0 = o0 + mo + go
    mix_refs = (*refs[:mi], *refs[o0:o0 + mo], refs[s0])
    gla_refs = (*refs[mi:o0], *refs[o0 + mo:s0], refs[s0 + 1])
    for phase in PHASES:
        pieces = []
        mix_kernel(*mix_refs, phases=(phase,), defer_to=pieces)
        _gla_kernel(*gla_refs, phases=(phase,), between=pieces)


def _mix_gla(mix_operands, gla_operands):
    m = _mix_ac_parts(*mix_operands)
    g = _gla_parts(*gla_operands)
    assert m["grid"] == g["grid"][:2] and g["grid"][2] == 1
    n_in = (len(m["args"]), len(g["args"]))
    n_out = (len(m["out_shape"]), len(g["out_shape"]))
    aliases = {sum(n_in) - 1: n_out[0] + 1} if g["aliased"] else {}
    res = pl.pallas_call(
        functools.partial(_mix_gla_kernel, n_in=n_in, n_out=n_out, mix_kernel=m["kernel"]),
        out_shape=m["out_shape"] + g["out_shape"], grid=m["grid"],
        in_specs=m["in_specs"] + g["in_specs"], out_specs=m["out_specs"] + g["out_specs"],
        scratch_shapes=m["scratch"] + g["scratch"], input_output_aliases=aliases,
        compiler_params=_cparams(("parallel", "arbitrary")), name="mix_gla")(*m["args"], *g["args"])
    return res[:n_out[0]], res[n_out[0]:]


def _layer(x, layer, mod3, s_gla, buf_conv, p, wts, s_out_prev, emit_v):
    n_seq, t, d = x.shape
    seq_len = min(t, CHUNK)
    reps = ROW_TILE // seq_len
    h, la = _norm_call(x, p["g_mix"], mod3, 1, 0, gate_w=(p["w_a"], p["w_a2"], p["b_a2"]))
    z_head, wb_head = _in_proj(h, wts["w_head"], layer, A_OFF, n_seq, t)
    if wts["w_tail"] is None:
        z_tail, wb_tail = _in_proj_tail_cast(h, wts["w_head"], layer, n_seq, t)
    else:
        z_tail, wb_tail = _in_proj(h, wts["w_tail"], layer, P_TAIL, n_seq, t)
    wt = jnp.tile(p["w_s"][:, :seq_len, :seq_len], (1, reps, reps))
    bias_full = jnp.repeat(jnp.tile(p["b_s"][:, :seq_len].T, (reps, 1)), DH_A, axis=1)
    mix_ops = (z_head, z_tail, wt, bias_full, p["w_conv"], layer, buf_conv, emit_v)
    gla_ops = (z_head, la, p["g_gla"], s_gla, layer, s_out_prev)
    if t >= ROW_TILE:
        mix, (yb, s_out) = _mix_gla(mix_ops, gla_ops)
    else:
        mix = _mix_ac(*mix_ops)
        yb, s_out = _gla(*gla_ops)
    ya, yc, buf_new = mix[:3]
    vn = mix[3] if emit_v else None
    x, wb_out = _out_proj(ya, yb, yc, wts["w_out"], layer, x, mod3, 2)
    h2 = _norm_call(x, p["g_ffn"], mod3, 4, 3)
    a, wb_gate, wb_up = _swiglu(h2, wts["w_gate"], wts["w_up"], layer)
    x, wb_down = _down_proj(a, wts["w_down"], layer, x, mod3, 5)
    wb = dict(w_head=wb_head, w_tail=wb_tail, w_out=wb_out, w_gate=wb_gate, w_up=wb_up, w_down=wb_down)
    return x, s_out, buf_new, vn, wb


def kernel(x_prompt, x_sample, state_gla, state_conv, c_prompt, c_sample, g_mix, g_ffn, w_mod, b_mod,
           w_in, w_s, b_s, w_a2, b_a2, g_gla, w_conv, w_out, w_gate, w_up, w_down, g_final):
    bp = x_prompt.shape[0]
    bd = x_sample.shape[0]
    n_c = bp + bd
    mc = -(-n_c // 16) * 16
    c_all = jnp.concatenate([c_prompt, c_sample, jnp.zeros((mc - n_c, D_MODEL), F32)], axis=0)
    mod = _modulation(c_all, w_mod, b_mod)
    gf = g_final.reshape(1, D_MODEL)
    g_gla3 = g_gla.reshape(DEPTH, 1, DV_B)
    gla0 = jnp.zeros((DEPTH, bp, H_B, DK_B, DV_B), F32)
    conv0 = jnp.zeros((DEPTH, bp, CONV_W - 1, W_C), F32)

    xs, xp = x_sample, x_prompt
    gla_s = gla_p = None
    conv_s, conv_p, v_s = [], [], []
    wt_in = jnp.swapaxes(w_in, 1, 2)
    for l in range(DEPTH):
        w_a = wt_in[l, A_OFF:A_OFF + GATE_RANK, :].T
        p = dict(
            g_mix=g_mix[l].reshape(1, D_MODEL), g_ffn=g_ffn[l].reshape(1, D_MODEL),
            w_a=jnp.pad(w_a, ((0, 0), (0, LANES - GATE_RANK))),
            w_a2=jnp.pad(w_a2[l], ((0, LANES - GATE_RANK), (0, 0))).astype(BF16),
            b_a2=b_a2[l].reshape(1, KW_B),
            w_s=w_s[l], b_s=b_s[l], g_gla=g_gla3, w_conv=w_conv)
        w_f32 = dict(w_head=wt_in, w_tail=None, w_out=w_out,
                     w_gate=w_gate, w_up=w_up, w_down=w_down)
        mod_s = mod[l, bp:n_c].reshape(bd, 1, N_MOD * D_MODEL)
        mod_p = mod[l, :bp].reshape(bp, 1, N_MOD * D_MODEL)
        xs, gla_s, buf_s, vn_s, w_bf16 = _layer(xs, l, mod_s, state_gla, state_conv, p, w_f32, gla_s, True)
        xp, gla_p, buf_p, _, _ = _layer(xp, l, mod_p, gla0, conv0, p, w_bf16, gla_p, False)
        conv_s.append(buf_s)
        conv_p.append(buf_p)
        v_s.append(vn_s)
    y_s = _final_norm(xs, gf)
    y_p = _final_norm(xp, gf)
    return (y_p, y_s, gla_p, jnp.stack(conv_p), gla_s, jnp.stack(conv_s), jnp.stack(v_s))
```

```python
import functools

import numpy as np
import jax
import jax.numpy as jnp
from jax import lax
from jax.experimental import pallas as pl
from jax.experimental.pallas import tpu as pltpu

F32 = jnp.float32
BF16 = jnp.bfloat16

D_MODEL = 4096
DEPTH = 2
EPS = 1e-6
CHUNK = 128
W_A = D_MODEL // 4
G_A = 8
DH_A = W_A // G_A
W_B = D_MODEL // 2
H_B = 8
DV_B = W_B // H_B
DK_B = DV_B // 2
KW_B = H_B * DK_B
GATE_RANK = 16
GATE_TAU = 16.0
W_C = D_MODEL // 4
CONV_W = 3
D_FF = -(-8 * D_MODEL // (3 * 256)) * 256
N_MOD = 6
A_OFF = 2 * W_A + 2 * KW_B + 2 * W_B
P_TAIL = 3 * W_C

LOG2_E = float(np.log2(np.e))
LANES = 128
ROW_TILE = 128
MIN_REF_LEVEL = 4
N_SMALL_LEVELS = MIN_REF_LEVEL.bit_length() - 1
GLA_WINDOW = 32
TM = 1024
IN_TN = (512, 1024)
OUT_TN = (512, 1024)
FFN_TN = 256
DOWN_TN = 256
MOD_TN = 512
NORM_ROWS = (256, 512)
SWIGLU_PART = 512
T_AXIS = 1
PHASES = ("init", "body", "final")
VMEM_LIMIT = 56 * 1024 * 1024
DOWN_PROJ_VMEM_LIMIT = 62 * 1024 * 1024

ZQ0 = (2 * W_A) // DK_B
ZK0 = (2 * W_A + KW_B) // DK_B
ZV0 = (2 * W_A + 2 * KW_B) // DV_B
ZR0 = (2 * W_A + 2 * KW_B + W_B) // DV_B


def _cparams(sem):
    return pltpu.CompilerParams(dimension_semantics=sem, vmem_limit_bytes=VMEM_LIMIT)


def _silu(x):
    return x / (1.0 + jnp.exp(-x))


def _gelu_tanh(x):
    c = np.float32(np.sqrt(2.0 / np.pi))
    return 0.5 * x * (1.0 + jnp.tanh(c * (x + 0.044715 * (x * x * x))))


def _row_tiling(t, rows):
    if t >= rows:
        return 1, rows
    return rows // t, t


def _w_spec(w, layer, k, tn, col_map):
    if w.ndim == 2:
        return pl.BlockSpec((k, tn), lambda *g: (0, col_map(*g)))
    return pl.BlockSpec((None, k, tn), lambda *g: (layer, 0, col_map(*g)))


def _mod_kernel(c_ref, w_ref, b_ref, o_ref):
    s = _silu(c_ref[...]).astype(BF16)
    o_ref[0] = jnp.dot(s, w_ref[0].astype(BF16), preferred_element_type=F32) + b_ref[0]


def _modulation(c_all, w_mod, b_mod):
    mc = c_all.shape[0]
    n = N_MOD * D_MODEL
    tn = MOD_TN
    return pl.pallas_call(
        _mod_kernel,
        out_shape=jax.ShapeDtypeStruct((DEPTH, mc, n), F32),
        grid=(DEPTH, n // tn),
        in_specs=[
            pl.BlockSpec((mc, D_MODEL), lambda l, j: (0, 0)),
            pl.BlockSpec((1, D_MODEL, tn), lambda l, j: (l, 0, j)),
            pl.BlockSpec((1, 1, tn), lambda l, j: (l, 0, j)),
        ],
        out_specs=pl.BlockSpec((1, mc, tn), lambda l, j: (l, 0, j)),
        compiler_params=_cparams(("parallel", "parallel")),
        name="modulation",
    )(c_all, w_mod, b_mod.reshape(DEPTH, 1, n))


def _norm_mod(x, g, sc, sh):
    y = x * lax.rsqrt(jnp.mean(x * x, axis=-1, keepdims=True) + EPS) * g
    return y * (1.0 + sc) + sh


def _norm_gate_kernel(x_ref, g_ref, sc_ref, sh_ref, wa_ref, wa2_ref, ba2_ref, h_ref, la_ref):
    bs, tt, d = x_ref.shape
    h = _norm_mod(x_ref[...], g_ref[...], sc_ref[...], sh_ref[...]).reshape(bs * tt, d).astype(BF16)
    h_ref[...] = h
    a = jnp.dot(h, wa_ref[...].astype(BF16), preferred_element_type=F32)
    pre = jnp.dot(a.astype(BF16), wa2_ref[...], preferred_element_type=F32) + ba2_ref[...]
    log_sig = jnp.minimum(pre, 0.0) - jnp.log(1.0 + jnp.exp(-jnp.abs(pre)))
    la_ref[...] = (log_sig / GATE_TAU).reshape(bs, tt, KW_B)


def _norm_kernel(x_ref, g_ref, sc_ref, sh_ref, h_ref):
    bs, tt, d = x_ref.shape
    h_ref[...] = _norm_mod(x_ref[...], g_ref[...], sc_ref[...], sh_ref[...]).reshape(bs * tt, d).astype(BF16)


def _norm_call(x, g, mod3, sc_idx, sh_idx, gate_w=None):
    n_seq, t, d = x.shape
    bs, tt = _row_tiling(t, NORM_ROWS[t >= NORM_ROWS[1]])
    rows = bs * tt
    nt = t // tt
    grid = (n_seq // bs, nt)
    x_spec = pl.BlockSpec((bs, tt, d), lambda i, j: (i, j, 0))
    g_spec = pl.BlockSpec((1, d), lambda i, j: (0, 0))
    mod_spec = lambda c: pl.BlockSpec((bs, 1, d), lambda i, j: (i, 0, c))
    h_spec = pl.BlockSpec((rows, d), lambda i, j: (i * nt + j, 0))
    h_shape = jax.ShapeDtypeStruct((n_seq * t, d), BF16)
    if gate_w is None:
        return pl.pallas_call(
            _norm_kernel, out_shape=h_shape, grid=grid,
            in_specs=[x_spec, g_spec, mod_spec(sc_idx), mod_spec(sh_idx)],
            out_specs=h_spec, compiler_params=_cparams(("parallel", "parallel")),
            name="norm_mod")(x, g, mod3, mod3)
    wa, wa2, ba2 = gate_w
    return pl.pallas_call(
        _norm_gate_kernel,
        out_shape=(h_shape, jax.ShapeDtypeStruct((n_seq, t, KW_B), F32)),
        grid=grid,
        in_specs=[x_spec, g_spec, mod_spec(sc_idx), mod_spec(sh_idx),
                  pl.BlockSpec((d, LANES), lambda i, j: (0, 0)),
                  pl.BlockSpec((LANES, KW_B), lambda i, j: (0, 0)),
                  pl.BlockSpec((1, KW_B), lambda i, j: (0, 0))],
        out_specs=(h_spec, pl.BlockSpec((bs, tt, KW_B), lambda i, j: (i, j, 0))),
        compiler_params=_cparams(("parallel", "parallel")),
        name="norm_mod_gate")(x, g, mod3, mod3, wa, wa2, ba2)


def _final_norm_kernel(x_ref, g_ref, o_ref):
    x = x_ref[...]
    o_ref[...] = x * lax.rsqrt(jnp.mean(x * x, axis=-1, keepdims=True) + EPS) * g_ref[...]


def _final_norm(x, g):
    n_seq, t, d = x.shape
    bs, tt = _row_tiling(t, NORM_ROWS[t >= NORM_ROWS[1]])
    return pl.pallas_call(
        _final_norm_kernel, out_shape=jax.ShapeDtypeStruct(x.shape, F32),
        grid=(n_seq // bs, t // tt),
        in_specs=[pl.BlockSpec((bs, tt, d), lambda i, j: (i, j, 0)),
                  pl.BlockSpec((1, d), lambda i, j: (0, 0))],
        out_specs=pl.BlockSpec((bs, tt, d), lambda i, j: (i, j, 0)),
        compiler_params=_cparams(("parallel", "parallel")),
        name="final_norm")(x, g)


def _bf16_tile(w_ref, wb_ref):
    if wb_ref is None:
        return w_ref[...]
    wb = w_ref[...].astype(BF16)
    wb_ref[...] = wb
    return wb


def _in_proj_kernel(h_ref, wt_ref, z_ref, wtb_ref=None):
    acc = lax.dot_general(h_ref[...], _bf16_tile(wt_ref, wtb_ref), (((1,), (1,)), ((), ())),
                          preferred_element_type=F32)
    z_ref[...] = acc.reshape(z_ref.shape)


def _in_proj(h, wt, layer, n_cols, n_seq, t):
    rows, k = h.shape
    emit = wt.dtype != BF16
    bs, tt = _row_tiling(t, TM)
    nt = t // tt
    tn = IN_TN[0] if emit else IN_TN[1]
    if wt.ndim == 2:
        w_spec = pl.BlockSpec((tn, k), lambda i, j: (j, 0))
    else:
        w_spec = pl.BlockSpec((None, tn, k), lambda i, j: (layer, j, 0))
    out_shape = [jax.ShapeDtypeStruct((n_seq, t, n_cols), F32)]
    out_specs = [pl.BlockSpec((bs, tt, tn), lambda i, j: (i // nt, i % nt, j))]
    if emit:
        assert rows == TM
        out_shape.append(jax.ShapeDtypeStruct((n_cols, k), BF16))
        out_specs.append(pl.BlockSpec((tn, k), lambda i, j: (j, 0)))
    res = pl.pallas_call(
        _in_proj_kernel, out_shape=out_shape,
        grid=(rows // TM, n_cols // tn),
        in_specs=[pl.BlockSpec((TM, k), lambda i, j: (i, 0)), w_spec],
        out_specs=out_specs,
        compiler_params=_cparams(("parallel", "parallel")),
        name="in_proj_cast" if emit else "in_proj")(h, wt)
    return (res[0], res[1]) if emit else (res[0], None)


def _in_proj_tail_cast_kernel(h_ref, wa_ref, wb_ref, z_ref, wtb_ref):
    wt = jnp.concatenate([wa_ref[GATE_RANK:, :], wb_ref[...]], axis=0).astype(BF16)
    wtb_ref[...] = wt
    acc = lax.dot_general(h_ref[...], wt, (((1,), (1,)), ((), ())), preferred_element_type=F32)
    z_ref[...] = acc.reshape(z_ref.shape)


def _in_proj_tail_cast(h, wt_all, layer, n_seq, t):
    rows, k = h.shape
    assert rows == TM
    bs, tt = _row_tiling(t, TM)
    tn = IN_TN[0]
    return pl.pallas_call(
        _in_proj_tail_cast_kernel,
        out_shape=(jax.ShapeDtypeStruct((n_seq, t, P_TAIL), F32), jax.ShapeDtypeStruct((P_TAIL, k), BF16)),
        grid=(P_TAIL // tn,),
        in_specs=[pl.BlockSpec((TM, k), lambda j: (0, 0)),
                  pl.BlockSpec((None, tn, k), lambda j: (layer, A_OFF // tn + j, 0)),
                  pl.BlockSpec((None, GATE_RANK, k),
                               lambda j: (layer, (A_OFF + tn * (j + 1)) // GATE_RANK, 0))],
        out_specs=(pl.BlockSpec((bs, tt, tn), lambda j: (0, 0, j)),
                   pl.BlockSpec((tn, k), lambda j: (j, 0))),
        compiler_params=_cparams(("parallel",)),
        name="in_proj_tail_cast")(h, wt_all, wt_all)


def _out_proj_kernel(ya_ref, yb_ref, yc_ref, w_ref, x_ref, gt_ref, o_ref, wb_ref=None):
    if wb_ref is not None:
        wb_ref[...] = w_ref[...].astype(BF16)
        w_ref = wb_ref
    acc = jnp.dot(ya_ref[...], w_ref[0:W_A, :], preferred_element_type=F32)
    acc += jnp.dot(yb_ref[...], w_ref[W_A:W_A + W_B, :], preferred_element_type=F32)
    acc += jnp.dot(yc_ref[...], w_ref[W_A + W_B:, :], preferred_element_type=F32)
    o_ref[...] = x_ref[...] + gt_ref[...] * acc.reshape(o_ref.shape)


def _out_proj(ya, yb, yc, w, layer, x, mod3, gt_idx):
    n_seq, t, d = x.shape
    rows = n_seq * t
    emit = w.dtype != BF16
    bs, tt = _row_tiling(t, TM)
    nt = t // tt
    tn = OUT_TN[0] if emit else OUT_TN[1]
    xo_spec = pl.BlockSpec((bs, tt, tn), lambda i, j: (i // nt, i % nt, j))
    out_shape = [jax.ShapeDtypeStruct(x.shape, F32)]
    out_specs = [xo_spec]
    if emit:
        assert rows == TM
        out_shape.append(jax.ShapeDtypeStruct((d, d), BF16))
        out_specs.append(pl.BlockSpec((d, tn), lambda i, j: (0, j)))
    res = pl.pallas_call(
        _out_proj_kernel, out_shape=out_shape,
        grid=(rows // TM, d // tn),
        in_specs=[pl.BlockSpec((TM, W_A), lambda i, j: (i, 0)),
                  pl.BlockSpec((TM, W_B), lambda i, j: (i, 0)),
                  pl.BlockSpec((TM, W_C), lambda i, j: (i, 0)),
                  _w_spec(w, layer, d, tn, lambda i, j: j),
                  xo_spec,
                  pl.BlockSpec((bs, 1, tn), lambda i, j: (i // nt, 0, gt_idx * (d // tn) + j))],
        out_specs=out_specs,
        compiler_params=_cparams(("parallel", "parallel")),
        name="out_proj_cast" if emit else "out_proj")(ya, yb, yc, w, x, mod3)
    return (res[0], res[1]) if emit else (res[0], None)


def _swiglu_kernel(h_ref, wg_ref, wu_ref, a_ref, wgb_ref=None, wub_ref=None):
    wg = _bf16_tile(wg_ref, wgb_ref)
    wu = _bf16_tile(wu_ref, wub_ref)
    rows = h_ref.shape[0]
    part = SWIGLU_PART
    for r0 in range(0, rows, part):
        h = h_ref[r0:r0 + part, :]
        g = jnp.dot(h, wg, preferred_element_type=F32)
        u = jnp.dot(h, wu, preferred_element_type=F32)
        a_ref[r0:r0 + part, :] = (_silu(g) * u).astype(BF16)


def _swiglu(h, wg, wu, layer):
    rows, k = h.shape
    n = wg.shape[-1]
    emit = wg.dtype != BF16
    tn = FFN_TN
    tm = TM if emit or rows % (2 * TM) else 2 * TM
    out_shape = [jax.ShapeDtypeStruct((rows, n), BF16)]
    out_specs = [pl.BlockSpec((tm, tn), lambda i, j: (i, j))]
    if emit:
        assert rows == TM
        out_shape += [jax.ShapeDtypeStruct((k, n), BF16)] * 2
        out_specs += [pl.BlockSpec((k, tn), lambda i, j: (0, j))] * 2
    res = pl.pallas_call(
        _swiglu_kernel, out_shape=out_shape,
        grid=(rows // tm, n // tn),
        in_specs=[pl.BlockSpec((tm, k), lambda i, j: (i, 0)),
                  _w_spec(wg, layer, k, tn, lambda i, j: j),
                  _w_spec(wu, layer, k, tn, lambda i, j: j)],
        out_specs=out_specs,
        compiler_params=_cparams(("parallel", "parallel")),
        name="swiglu_cast" if emit else "swiglu")(h, wg, wu)
    return (res[0], res[1], res[2]) if emit else (res[0], None, None)


def _down_proj_cast_kernel(a_ref, w_ref, x_ref, gt_ref, o_ref, wb_ref, acc_ref):
    kk = pl.program_id(0)
    j = pl.program_id(1)
    part = jnp.dot(a_ref[...], _bf16_tile(w_ref, wb_ref), preferred_element_type=F32)

    @pl.when(kk == 0)
    def _():
        acc_ref[j] = part

    @pl.when(kk == 1)
    def _():
        o_ref[...] = x_ref[...] + gt_ref[...] * (acc_ref[j] + part).reshape(o_ref.shape)


def _down_proj_kernel(a_ref, w_ref, x_ref, gt_ref, o_ref):
    acc = jnp.dot(a_ref[...], w_ref[...], preferred_element_type=F32)
    o_ref[...] = x_ref[...] + gt_ref[...] * acc.reshape(o_ref.shape)


def _down_proj(a, w, layer, x, mod3, gt_idx):
    n_seq, t, d = x.shape
    rows, k = a.shape
    bs, tt = _row_tiling(t, TM)
    nt = t // tt
    tn = DOWN_TN
    n_j = d // tn
    if w.dtype == BF16:
        xo_spec = pl.BlockSpec((bs, tt, tn), lambda i, j: (i // nt, i % nt, j))
        out = pl.pallas_call(
            _down_proj_kernel, out_shape=jax.ShapeDtypeStruct(x.shape, F32),
            grid=(rows // TM, n_j),
            in_specs=[pl.BlockSpec((TM, k), lambda i, j: (i, 0)),
                      pl.BlockSpec((k, tn), lambda i, j: (0, j)),
                      xo_spec,
                      pl.BlockSpec((bs, 1, tn), lambda i, j: (i // nt, 0, gt_idx * n_j + j))],
            out_specs=xo_spec,
            compiler_params=pltpu.CompilerParams(dimension_semantics=("parallel", "parallel"),
                                                 vmem_limit_bytes=DOWN_PROJ_VMEM_LIMIT),
            name="down_proj")(a, w, x, mod3)
        return out, None
    assert rows == TM
    tk = k // 2
    xo_spec = pl.BlockSpec((bs, tt, tn), lambda kk, j: (0, 0, j * kk))
    out, wb = pl.pallas_call(
        _down_proj_cast_kernel,
        out_shape=(jax.ShapeDtypeStruct(x.shape, F32), jax.ShapeDtypeStruct((k, d), BF16)),
        grid=(2, n_j),
        in_specs=[pl.BlockSpec((TM, tk), lambda kk, j: (0, kk), pipeline_mode=pl.Buffered(1)),
                  pl.BlockSpec((None, tk, tn), lambda kk, j: (layer, kk, j)),
                  xo_spec,
                  pl.BlockSpec((bs, 1, tn), lambda kk, j: (0, 0, gt_idx * n_j + j * kk))],
        out_specs=(xo_spec, pl.BlockSpec((tk, tn), lambda kk, j: (kk, j))),
        scratch_shapes=[pltpu.VMEM((n_j, TM, tn), F32)],
        compiler_params=_cparams(("arbitrary", "arbitrary")),
        name="down_proj_cast")(a, w, x, mod3)
    return out, wb


def _mix_ac_kernel(u_ref, v_ref, b_ref, c_ref, hc_ref, wt_ref, bias_ref, wc_ref, buf_ref,
                   ya_ref, yc_ref, bufnew_ref, *rest, seq_len, phases=PHASES, defer_to=None):
    zbuf = rest[-1]
    vn_ref = rest[0] if len(rest) == 2 else None
    bs, tt, w = u_ref.shape
    rows = bs * tt
    t = pl.program_id(T_AXIS)
    pad = 8

    if "init" in phases:
        @pl.when(t == 0)
        def _():
            zbuf[:, pad - 2:pad, :] = buf_ref[...]

    if "body" in phases:
        work = _mix_ac_work(u_ref, v_ref, b_ref, c_ref, hc_ref, wt_ref, bias_ref, wc_ref, ya_ref, yc_ref,
                            vn_ref, zbuf, pad, seq_len)
        if defer_to is None:
            for piece in work:
                piece()
        else:
            defer_to.extend(work)

    if "final" in phases:
        @pl.when(t == pl.num_programs(T_AXIS) - 1)
        def _():
            bufnew_ref[...] = zbuf[:, pad - 2:pad, :]


def _mix_ac_work(u_ref, v_ref, b_ref, c_ref, hc_ref, wt_ref, bias_ref, wc_ref, ya_ref, yc_ref, vn_ref, zbuf,
                 pad, seq_len):
    bs, tt, _ = u_ref.shape
    rows = bs * tt

    def piece(g):
        sl = slice(g * DH_A, (g + 1) * DH_A)
        zc = c_ref[:, :, sl] * hc_ref[:, :, sl]
        zbuf[:, pad:pad + tt, sl] = zc
        y = zbuf[:, pad - 2:pad - 2 + tt, sl] * wc_ref[0:1, sl]
        y = y + zbuf[:, pad - 1:pad - 1 + tt, sl] * wc_ref[1:2, sl]
        y = y + zc * wc_ref[2:3, sl]
        yc_ref[:, sl] = (b_ref[:, :, sl] * y).reshape(rows, DH_A).astype(BF16)
        zbuf[:, pad - 2:pad, sl] = zbuf[:, pad + tt - 2:pad + tt, sl]

        gu = _gelu_tanh(u_ref[:, :, sl].reshape(rows, DH_A))
        vg = _gelu_tanh(v_ref[:, :, sl].reshape(rows, DH_A))
        dv = vg - jnp.mean(vg, axis=-1, keepdims=True)
        vn = dv * lax.rsqrt(jnp.mean(dv * dv, axis=-1, keepdims=True) + EPS)
        ri = lax.broadcasted_iota(jnp.int32, (rows, rows), 0)
        ci = lax.broadcasted_iota(jnp.int32, (rows, rows), 1)
        keep = ci <= ri
        if seq_len < rows:
            keep = keep & ((ri // seq_len) == (ci // seq_len))
        wm = jnp.where(keep, wt_ref[g], 0.0).astype(BF16)
        mixed = jnp.dot(wm, vn.astype(BF16), preferred_element_type=F32) + bias_ref[:, sl]
        ya_ref[:, sl] = (gu * mixed).astype(BF16)
        if vn_ref is not None:
            vn_ref[:, :, sl] = vn.reshape(bs, tt, DH_A)

    return [functools.partial(piece, g) for g in range(G_A)]


def _mix_ac_parts(z_head, z_tail, wt, bias_full, w_conv, layer, buf, emit_v):
    n_seq, t, _ = z_head.shape
    bs, tt = _row_tiling(t, ROW_TILE)
    nt = t // tt
    rows = n_seq * t
    zspec = lambda c: pl.BlockSpec((bs, tt, W_A), lambda i, j, *_: (i, j, c))
    y_spec = pl.BlockSpec((ROW_TILE, W_A), lambda i, j, *_: (i * nt + j, 0))
    out_shape = [jax.ShapeDtypeStruct((rows, W_A), BF16),
                 jax.ShapeDtypeStruct((rows, W_C), BF16),
                 jax.ShapeDtypeStruct((n_seq, CONV_W - 1, W_C), F32)]
    out_specs = [y_spec, y_spec, pl.BlockSpec((bs, CONV_W - 1, W_C), lambda i, j, *_: (i, 0, 0))]
    if emit_v:
        out_shape.append(jax.ShapeDtypeStruct((n_seq, t, W_A), F32))
        out_specs.append(pl.BlockSpec((bs, tt, W_A), lambda i, j, *_: (i, j, 0)))
    in_specs = [zspec(0), zspec(1), zspec(0), zspec(1), zspec(2),
                pl.BlockSpec((G_A, ROW_TILE, ROW_TILE), lambda *_: (0, 0, 0)),
                pl.BlockSpec((ROW_TILE, W_A), lambda *_: (0, 0)),
                pl.BlockSpec((None, CONV_W, W_C), lambda *_: (layer, 0, 0)),
                pl.BlockSpec((None, bs, CONV_W - 1, W_C), lambda i, *_: (layer, i, 0, 0))]
    return dict(args=[z_head, z_head, z_tail, z_tail, z_tail, wt, bias_full, w_conv, buf],
                in_specs=in_specs, out_shape=out_shape, out_specs=out_specs,
                scratch=[pltpu.VMEM((bs, tt + 8, W_C), F32)], grid=(n_seq // bs, nt),
                kernel=functools.partial(_mix_ac_kernel, seq_len=min(t, CHUNK)))


def _mix_ac(*operands):
    m = _mix_ac_parts(*operands)
    return pl.pallas_call(
        m["kernel"], out_shape=m["out_shape"], grid=m["grid"], in_specs=m["in_specs"],
        out_specs=m["out_specs"], scratch_shapes=m["scratch"],
        compiler_params=_cparams(("parallel", "arbitrary")), name="mix_ac")(*m["args"])


def _gla_tables(rows, seq_len):
    i = np.arange(rows)[:, None]
    t = np.arange(rows)[None, :]
    window = min(seq_len, GLA_WINDOW)
    slabs = [((i // window) == (t // window)) & (t <= i)]
    masks = [i == t]
    s = seq_len // 2
    while s >= 1:
        second = (i // s) % 2 == 1
        start2 = (i // s) * s
        end1 = start2 + s - 1
        if s < MIN_REF_LEVEL:
            slabs.append(np.where(second, (t >= start2) & (t <= i), (t > i) & (t <= end1)))
        masks.append(((i // (2 * s)) == (t // (2 * s))) & second & ((t // s) % 2 == 0))
        s //= 2
    return (np.concatenate(slabs, 0).astype(np.float32), np.stack(masks).astype(np.float32))


def _block_ref_exponent(g, block, ref_row, flip_from):
    rows, dk = g.shape
    gr = g.reshape(rows // block, block, dk)
    d = gr - gr[:, ref_row:ref_row + 1, :]
    pos = lax.broadcasted_iota(jnp.int32, gr.shape, 1)
    return jnp.where(pos >= flip_from, d, -d).reshape(rows, dk)


def _double_restart(pre, suf, s):
    rows, dk = pre.shape
    pr = pre.reshape(rows // (2 * s), 2 * s, dk)
    sr = suf.reshape(rows // (2 * s), 2 * s, dk)
    second = lax.broadcasted_iota(jnp.int32, pr.shape, 1) >= s
    x = jnp.where(second, pr, sr)
    pre2 = jnp.where(second, pr + pr[:, s - 1:s, :], pr)
    suf2 = jnp.where(second, sr, sr + pr[:, 2 * s - 1:2 * s, :])
    return x.reshape(rows, dk), pre2.reshape(rows, dk), suf2.reshape(rows, dk)


def _gla_kernel(q_ref, k_ref, v_ref, r_ref, la_ref, mall_ref, mask_ref, g_ref, s0_ref, *rest, phases=PHASES,
                between=()):
    y_ref, snew_ref, s_scr = rest[-3:]
    t = pl.program_id(T_AXIS)

    if "init" in phases:
        @pl.when(t == 0)
        def _():
            s_scr[...] = s0_ref[...]

    if "body" in phases:
        _gla_body(q_ref, k_ref, v_ref, r_ref, la_ref, mall_ref, mask_ref, g_ref, y_ref, s_scr, between)

    if "final" in phases:
        @pl.when(t == pl.num_programs(T_AXIS) - 1)
        def _():
            snew_ref[...] = s_scr[...]


def _gla_body(q_ref, k_ref, v_ref, r_ref, la_ref, mall_ref, mask_ref, g_ref, y_ref, s_scr, between):
    bs, tt, _ = q_ref.shape
    hp = s_scr.shape[1]
    assert len(between) in (0, hp)
    rows = bs * tt
    n_lvl = mask_ref.shape[0] - 1
    ri = lax.broadcasted_iota(jnp.int32, (DK_B, DK_B), 0)
    ci = lax.broadcasted_iota(jnp.int32, (DK_B, DK_B), 1)
    eye = ri == ci
    nt_dims = (((1,), (1,)), ((), ()))
    tn_dims = (((0,), (0,)), ((), ()))

    for hh in range(hp):
        ks = slice(hh * DK_B, (hh + 1) * DK_B)
        vs = slice(hh * DV_B, (hh + 1) * DV_B)
        q = q_ref[:, :, ks].reshape(rows, DK_B) * (DK_B ** -0.5)
        k = k_ref[:, :, ks].reshape(rows, DK_B)
        v = v_ref[:, :, vs].reshape(rows, DV_B)
        vb = v.astype(BF16)
        la = la_ref[:, :, ks].reshape(rows, DK_B) * LOG2_E
        la_hi = la.astype(BF16)
        la_lo = (la - la_hi.astype(F32)).astype(BF16)
        x = jnp.dot(mall_ref[...], jnp.concatenate([la_hi, la_lo], axis=1), preferred_element_type=F32)
        x = x[:, :DK_B] + x[:, DK_B:]
        window = min(tt, GLA_WINDOW)
        pre = x[0:rows]
        suf = _block_ref_exponent(pre, window, window - 1, window)

        a = mask_ref[0] * lax.dot_general(q.astype(BF16), k.astype(BF16), nt_dims,
                                          preferred_element_type=F32)
        for lvl in reversed(range(n_lvl)):
            s = tt >> (lvl + 1)
            if s >= window:
                xl, pre, suf = _double_restart(pre, suf, s)
            elif s >= MIN_REF_LEVEL:
                xl = _block_ref_exponent(pre, 2 * s, s - 1, s)
            else:
                n_slab = 1 + lvl - (n_lvl - N_SMALL_LEVELS)
                xl = x[n_slab * rows:(n_slab + 1) * rows]
            el = jnp.exp2(xl)
            p = lax.dot_general((q * el).astype(BF16), (k * el).astype(BF16), nt_dims,
                                preferred_element_type=F32)
            a = a + mask_ref[lvl + 1] * p
        o_intra = jnp.dot(a.astype(BF16), vb, preferred_element_type=F32)
        e_g = jnp.exp2(pre)
        qg = q * e_g
        kd = k * jnp.exp2(suf)

        o_parts = []
        for b in range(bs):
            rs = slice(b * tt, (b + 1) * tt)
            s = s_scr[b, hh]
            o_parts.append(o_intra[rs] + jnp.dot(qg[rs].astype(BF16), s.astype(BF16),
                                                 preferred_element_type=F32))
            e_last = e_g[(b + 1) * tt - 1:(b + 1) * tt]
            e_col = jnp.sum(jnp.where(eye, jnp.broadcast_to(e_last, (DK_B, DK_B)), 0.0),
                            axis=1, keepdims=True)
            kv = lax.dot_general(kd[rs].astype(BF16), v[rs].astype(BF16), tn_dims,
                                 preferred_element_type=F32)
            s_scr[b, hh] = e_col * s + kv
        o = o_parts[0] if bs == 1 else jnp.concatenate(o_parts, axis=0)
        yn = o * lax.rsqrt(jnp.mean(o * o, axis=-1, keepdims=True) + EPS) * g_ref[...]
        y_ref[:, vs] = (yn * _silu(r_ref[:, :, vs].reshape(rows, DV_B))).astype(BF16)
        if between:
            between[hh]()


def _gla_parts(z_head, la, g_gla, s0, layer, s_out_prev):
    n_seq, t, _ = z_head.shape
    tile = ROW_TILE if t >= ROW_TILE else 2 * ROW_TILE
    bs, tt = _row_tiling(t, tile)
    hp = H_B if bs == 1 else 1
    nt = t // tt
    assert nt == 1 or hp == H_B
    rows = n_seq * t
    mall, masks = _gla_tables(tile, tt)
    mall = jnp.asarray(mall, BF16)
    masks = jnp.asarray(masks, F32)
    zs = lambda w, c0: pl.BlockSpec((bs, tt, hp * w), lambda i, j, h=0: (i, j, c0 // hp + h))
    s_spec = pl.BlockSpec((None, bs, hp, DK_B, DV_B), lambda i, j, h=0: (layer, i, h, 0, 0))
    in_specs = [zs(DK_B, ZQ0), zs(DK_B, ZK0), zs(DV_B, ZV0), zs(DV_B, ZR0),
                pl.BlockSpec((bs, tt, hp * DK_B), lambda i, j, h=0: (i, j, h)),
                pl.BlockSpec(mall.shape, lambda *_: (0, 0)),
                pl.BlockSpec(masks.shape, lambda *_: (0, 0, 0)),
                pl.BlockSpec((None, 1, DV_B), lambda *_: (layer, 0, 0)),
                s_spec]
    args = [z_head, z_head, z_head, z_head, la, mall, masks, g_gla, s0]
    if s_out_prev is not None:
        in_specs.append(pl.BlockSpec(memory_space=pl.ANY))
        args.append(s_out_prev)
    return dict(args=args, in_specs=in_specs, aliased=s_out_prev is not None,
                out_shape=[jax.ShapeDtypeStruct((rows, W_B), BF16),
                           jax.ShapeDtypeStruct((DEPTH, n_seq, H_B, DK_B, DV_B), F32)],
                out_specs=[pl.BlockSpec((tile, hp * DV_B), lambda i, j, h=0: (i * nt + j, h)), s_spec],
                scratch=[pltpu.VMEM((bs, hp, DK_B, DV_B), F32)], grid=(n_seq // bs, nt, H_B // hp))


def _gla(*operands):
    m = _gla_parts(*operands)
    aliases = {len(m["args"]) - 1: 1} if m["aliased"] else {}
    return pl.pallas_call(
        _gla_kernel, out_shape=m["out_shape"], grid=m["grid"], in_specs=m["in_specs"],
        out_specs=m["out_specs"], scratch_shapes=m["scratch"], input_output_aliases=aliases,
        compiler_params=_cparams(("parallel", "arbitrary", "parallel")), name="gla")(*m["args"])


def _mix_gla_kernel(*refs, n_in, n_out, mix_kernel):
    (mi, gi), (mo, go) = n_in, n_out
    o0 = mi + gi
    s0 = o0 + mo + go
    mix_refs = (*refs[:mi], *refs[o0:o0 + mo], refs[s0])
    gla_refs = (*refs[mi:o0], *refs[o0 + mo:s0], refs[s0 + 1])
    for phase in PHASES:
        pieces = []
        mix_kernel(*mix_refs, phases=(phase,), defer_to=pieces)
        _gla_kernel(*gla_refs, phases=(phase,), between=pieces)


def _mix_gla(mix_operands, gla_operands):
    m = _mix_ac_parts(*mix_operands)
    g = _gla_parts(*gla_operands)
    assert m["grid"] == g["grid"][:2] and g["grid"][2] == 1
    n_in = (len(m["args"]), len(g["args"]))
    n_out = (len(m["out_shape"]), len(g["out_shape"]))
    aliases = {sum(n_in) - 1: n_out[0] + 1} if g["aliased"] else {}
    res = pl.pallas_call(
        functools.partial(_mix_gla_kernel, n_in=n_in, n_out=n_out, mix_kernel=m["kernel"]),
        out_shape=m["out_shape"] + g["out_shape"], grid=m["grid"],
        in_specs=m["in_specs"] + g["in_specs"], out_specs=m["out_specs"] + g["out_specs"],
        scratch_shapes=m["scratch"] + g["scratch"], input_output_aliases=aliases,
        compiler_params=_cparams(("parallel", "arbitrary")), name="mix_gla")(*m["args"], *g["args"])
    return res[:n_out[0]], res[n_out[0]:]


def _layer(x, layer, mod3, s_gla, buf_conv, p, wts, s_out_prev, emit_v):
    n_seq, t, d = x.shape
    seq_len = min(t, CHUNK)
    reps = ROW_TILE // seq_len
    h, la = _norm_call(x, p["g_mix"], mod3, 1, 0, gate_w=(p["w_a"], p["w_a2"], p["b_a2"]))
    z_head, wb_head = _in_proj(h, wts["w_head"], layer, A_OFF, n_seq, t)
    if wts["w_tail"] is None:
        z_tail, wb_tail = _in_proj_tail_cast(h, wts["w_head"], layer, n_seq, t)
    else:
        z_tail, wb_tail = _in_proj(h, wts["w_tail"], layer, P_TAIL, n_seq, t)
    wt = jnp.tile(p["w_s"][:, :seq_len, :seq_len], (1, reps, reps))
    bias_full = jnp.repeat(jnp.tile(p["b_s"][:, :seq_len].T, (reps, 1)), DH_A, axis=1)
    mix_ops = (z_head, z_tail, wt, bias_full, p["w_conv"], layer, buf_conv, emit_v)
    gla_ops = (z_head, la, p["g_gla"], s_gla, layer, s_out_prev)
    if t >= ROW_TILE:
        mix, (yb, s_out) = _mix_gla(mix_ops, gla_ops)
    else:
        mix = _mix_ac(*mix_ops)
        yb, s_out = _gla(*gla_ops)
    ya, yc, buf_new = mix[:3]
    vn = mix[3] if emit_v else None
    x, wb_out = _out_proj(ya, yb, yc, wts["w_out"], layer, x, mod3, 2)
    h2 = _norm_call(x, p["g_ffn"], mod3, 4, 3)
    a, wb_gate, wb_up = _swiglu(h2, wts["w_gate"], wts["w_up"], layer)
    x, wb_down = _down_proj(a, wts["w_down"], layer, x, mod3, 5)
    wb = dict(w_head=wb_head, w_tail=wb_tail, w_out=wb_out, w_gate=wb_gate, w_up=wb_up, w_down=wb_down)
    return x, s_out, buf_new, vn, wb


def kernel(x_prompt, x_sample, state_gla, state_conv, c_prompt, c_sample, g_mix, g_ffn, w_mod, b_mod,
           w_in, w_s, b_s, w_a2, b_a2, g_gla, w_conv, w_out, w_gate, w_up, w_down, g_final):
    bp = x_prompt.shape[0]
    bd = x_sample.shape[0]
    n_c = bp + bd
    mc = -(-n_c // 16) * 16
    c_all = jnp.concatenate([c_prompt, c_sample, jnp.zeros((mc - n_c, D_MODEL), F32)], axis=0)
    mod = _modulation(c_all, w_mod, b_mod)
    gf = g_final.reshape(1, D_MODEL)
    g_gla3 = g_gla.reshape(DEPTH, 1, DV_B)
    gla0 = jnp.zeros((DEPTH, bp, H_B, DK_B, DV_B), F32)
    conv0 = jnp.zeros((DEPTH, bp, CONV_W - 1, W_C), F32)

    xs, xp = x_sample, x_prompt
    gla_s = gla_p = None
    conv_s, conv_p, v_s = [], [], []
    wt_in = jnp.swapaxes(w_in, 1, 2)
    for l in range(DEPTH):
        w_a = wt_in[l, A_OFF:A_OFF + GATE_RANK, :].T
        p = dict(
            g_mix=g_mix[l].reshape(1, D_MODEL), g_ffn=g_ffn[l].reshape(1, D_MODEL),
            w_a=jnp.pad(w_a, ((0, 0), (0, LANES - GATE_RANK))),
            w_a2=jnp.pad(w_a2[l], ((0, LANES - GATE_RANK), (0, 0))).astype(BF16),
            b_a2=b_a2[l].reshape(1, KW_B),
            w_s=w_s[l], b_s=b_s[l], g_gla=g_gla3, w_conv=w_conv)
        w_f32 = dict(w_head=wt_in, w_tail=None, w_out=w_out,
                     w_gate=w_gate, w_up=w_up, w_down=w_down)
        mod_s = mod[l, bp:n_c].reshape(bd, 1, N_MOD * D_MODEL)
        mod_p = mod[l, :bp].reshape(bp, 1, N_MOD * D_MODEL)
        xs, gla_s, buf_s, vn_s, w_bf16 = _layer(xs, l, mod_s, state_gla, state_conv, p, w_f32, gla_s, True)
        xp, gla_p, buf_p, _, _ = _layer(xp, l, mod_p, gla0, conv0, p, w_bf16, gla_p, False)
        conv_s.append(buf_s)
        conv_p.append(buf_p)
        v_s.append(vn_s)
    y_s = _final_norm(xs, gf)
    y_p = _final_norm(xp, gf)
    return (y_p, y_s, gla_p, jnp.stack(conv_p), gla_s, jnp.stack(conv_s), jnp.stack(v_s))
```

```python
import functools

import numpy as np
import jax
import jax.numpy as jnp
from jax import lax
from jax.experimental import pallas as pl
from jax.experimental.pallas import tpu as pltpu

F32 = jnp.float32
BF16 = jnp.bfloat16

D_MODEL = 4096
DEPTH = 2
EPS = 1e-6
CHUNK = 128
W_A = D_MODEL // 4
G_A = 8
DH_A = W_A // G_A
W_B = D_MODEL // 2
H_B = 8
DV_B = W_B // H_B
DK_B = DV_B // 2
KW_B = H_B * DK_B
GATE_RANK = 16
GATE_TAU = 16.0
W_C = D_MODEL // 4
CONV_W = 3
D_FF = -(-8 * D_MODEL // (3 * 256)) * 256
N_MOD = 6
A_OFF = 2 * W_A + 2 * KW_B + 2 * W_B
P_TAIL = 3 * W_C

LOG2_E = float(np.log2(np.e))
LANES = 128
ROW_TILE = 128
MIN_REF_LEVEL = 4
N_SMALL_LEVELS = MIN_REF_LEVEL.bit_length() - 1
GLA_WINDOW = 32
TM = 1024
IN_TN = (512, 1024)
OUT_TN = (512, 1024)
FFN_TN = 256
DOWN_TN = 256
MOD_TN = 512
NORM_ROWS = (256, 512)
SWIGLU_PART = 512
T_AXIS = 1
PHASES = ("init", "body", "final")
VMEM_LIMIT = 56 * 1024 * 1024
DOWN_PROJ_VMEM_LIMIT = 62 * 1024 * 1024

ZQ0 = (2 * W_A) // DK_B
ZK0 = (2 * W_A + KW_B) // DK_B
ZV0 = (2 * W_A + 2 * KW_B) // DV_B
ZR0 = (2 * W_A + 2 * KW_B + W_B) // DV_B


def _cparams(sem):
    return pltpu.CompilerParams(dimension_semantics=sem, vmem_limit_bytes=VMEM_LIMIT)


def _silu(x):
    return x / (1.0 + jnp.exp(-x))


def _gelu_tanh(x):
    c = np.float32(np.sqrt(2.0 / np.pi))
    return 0.5 * x * (1.0 + jnp.tanh(c * (x + 0.044715 * (x * x * x))))


def _row_tiling(t, rows):
    if t >= rows:
        return 1, rows
    return rows // t, t


def _w_spec(w, layer, k, tn, col_map):
    if w.ndim == 2:
        return pl.BlockSpec((k, tn), lambda *g: (0, col_map(*g)))
    return pl.BlockSpec((None, k, tn), lambda *g: (layer, 0, col_map(*g)))


def _mod_kernel(c_ref, w_ref, b_ref, o_ref):
    s = _silu(c_ref[...]).astype(BF16)
    o_ref[0] = jnp.dot(s, w_ref[0].astype(BF16), preferred_element_type=F32) + b_ref[0]


def _modulation(c_all, w_mod, b_mod):
    mc = c_all.shape[0]
    n = N_MOD * D_MODEL
    tn = MOD_TN
    return pl.pallas_call(
        _mod_kernel,
        out_shape=jax.ShapeDtypeStruct((DEPTH, mc, n), F32),
        grid=(DEPTH, n // tn),
        in_specs=[
            pl.BlockSpec((mc, D_MODEL), lambda l, j: (0, 0)),
            pl.BlockSpec((1, D_MODEL, tn), lambda l, j: (l, 0, j)),
            pl.BlockSpec((1, 1, tn), lambda l, j: (l, 0, j)),
        ],
        out_specs=pl.BlockSpec((1, mc, tn), lambda l, j: (l, 0, j)),
        compiler_params=_cparams(("parallel", "parallel")),
        name="modulation",
    )(c_all, w_mod, b_mod.reshape(DEPTH, 1, n))


def _norm_mod(x, g, sc, sh):
    y = x * lax.rsqrt(jnp.mean(x * x, axis=-1, keepdims=True) + EPS) * g
    return y * (1.0 + sc) + sh


def _norm_gate_kernel(x_ref, g_ref, sc_ref, sh_ref, wa_ref, wa2_ref, ba2_ref, h_ref, la_ref):
    bs, tt, d = x_ref.shape
    h = _norm_mod(x_ref[...], g_ref[...], sc_ref[...], sh_ref[...]).reshape(bs * tt, d).astype(BF16)
    h_ref[...] = h
    a = jnp.dot(h, wa_ref[...].astype(BF16), preferred_element_type=F32)
    pre = jnp.dot(a.astype(BF16), wa2_ref[...], preferred_element_type=F32) + ba2_ref[...]
    log_sig = jnp.minimum(pre, 0.0) - jnp.log(1.0 + jnp.exp(-jnp.abs(pre)))
    la_ref[...] = (log_sig / GATE_TAU).reshape(bs, tt, KW_B)


def _norm_kernel(x_ref, g_ref, sc_ref, sh_ref, h_ref):
    bs, tt, d = x_ref.shape
    h_ref[...] = _norm_mod(x_ref[...], g_ref[...], sc_ref[...], sh_ref[...]).reshape(bs * tt, d).astype(BF16)


def _norm_call(x, g, mod3, sc_idx, sh_idx, gate_w=None):
    n_seq, t, d = x.shape
    bs, tt = _row_tiling(t, NORM_ROWS[t >= NORM_ROWS[1]])
    rows = bs * tt
    nt = t // tt
    grid = (n_seq // bs, nt)
    x_spec = pl.BlockSpec((bs, tt, d), lambda i, j: (i, j, 0))
    g_spec = pl.BlockSpec((1, d), lambda i, j: (0, 0))
    mod_spec = lambda c: pl.BlockSpec((bs, 1, d), lambda i, j: (i, 0, c))
    h_spec = pl.BlockSpec((rows, d), lambda i, j: (i * nt + j, 0))
    h_shape = jax.ShapeDtypeStruct((n_seq * t, d), BF16)
    if gate_w is None:
        return pl.pallas_call(
            _norm_kernel, out_shape=h_shape, grid=grid,
            in_specs=[x_spec, g_spec, mod_spec(sc_idx), mod_spec(sh_idx)],
            out_specs=h_spec, compiler_params=_cparams(("parallel", "parallel")),
            name="norm_mod")(x, g, mod3, mod3)
    wa, wa2, ba2 = gate_w
    return pl.pallas_call(
        _norm_gate_kernel,
        out_shape=(h_shape, jax.ShapeDtypeStruct((n_seq, t, KW_B), F32)),
        grid=grid,
        in_specs=[x_spec, g_spec, mod_spec(sc_idx), mod_spec(sh_idx),
                  pl.BlockSpec((d, LANES), lambda i, j: (0, 0)),
                  pl.BlockSpec((LANES, KW_B), lambda i, j: (0, 0)),
                  pl.BlockSpec((1, KW_B), lambda i, j: (0, 0))],
        out_specs=(h_spec, pl.BlockSpec((bs, tt, KW_B), lambda i, j: (i, j, 0))),
        compiler_params=_cparams(("parallel", "parallel")),
        name="norm_mod_gate")(x, g, mod3, mod3, wa, wa2, ba2)


def _final_norm_kernel(x_ref, g_ref, o_ref):
    x = x_ref[...]
    o_ref[...] = x * lax.rsqrt(jnp.mean(x * x, axis=-1, keepdims=True) + EPS) * g_ref[...]


def _final_norm(x, g):
    n_seq, t, d = x.shape
    bs, tt = _row_tiling(t, NORM_ROWS[t >= NORM_ROWS[1]])
    return pl.pallas_call(
        _final_norm_kernel, out_shape=jax.ShapeDtypeStruct(x.shape, F32),
        grid=(n_seq // bs, t // tt),
        in_specs=[pl.BlockSpec((bs, tt, d), lambda i, j: (i, j, 0)),
                  pl.BlockSpec((1, d), lambda i, j: (0, 0))],
        out_specs=pl.BlockSpec((bs, tt, d), lambda i, j: (i, j, 0)),
        compiler_params=_cparams(("parallel", "parallel")),
        name="final_norm")(x, g)


def _bf16_tile(w_ref, wb_ref):
    if wb_ref is None:
        return w_ref[...]
    wb = w_ref[...].astype(BF16)
    wb_ref[...] = wb
    return wb


def _in_proj_kernel(h_ref, wt_ref, z_ref, wtb_ref=None):
    acc = lax.dot_general(h_ref[...], _bf16_tile(wt_ref, wtb_ref), (((1,), (1,)), ((), ())),
                          preferred_element_type=F32)
    z_ref[...] = acc.reshape(z_ref.shape)


def _in_proj(h, wt, layer, n_cols, n_seq, t):
    rows, k = h.shape
    emit = wt.dtype != BF16
    bs, tt = _row_tiling(t, TM)
    nt = t // tt
    tn = IN_TN[0] if emit else IN_TN[1]
    if wt.ndim == 2:
        w_spec = pl.BlockSpec((tn, k), lambda i, j: (j, 0))
    else:
        w_spec = pl.BlockSpec((None, tn, k), lambda i, j: (layer, j, 0))
    out_shape = [jax.ShapeDtypeStruct((n_seq, t, n_cols), F32)]
    out_specs = [pl.BlockSpec((bs, tt, tn), lambda i, j: (i // nt, i % nt, j))]
    if emit:
        assert rows == TM
        out_shape.append(jax.ShapeDtypeStruct((n_cols, k), BF16))
        out_specs.append(pl.BlockSpec((tn, k), lambda i, j: (j, 0)))
    res = pl.pallas_call(
        _in_proj_kernel, out_shape=out_shape,
        grid=(rows // TM, n_cols // tn),
        in_specs=[pl.BlockSpec((TM, k), lambda i, j: (i, 0)), w_spec],
        out_specs=out_specs,
        compiler_params=_cparams(("parallel", "parallel")),
        name="in_proj_cast" if emit else "in_proj")(h, wt)
    return (res[0], res[1]) if emit else (res[0], None)


def _in_proj_tail_cast_kernel(h_ref, wa_ref, wb_ref, z_ref, wtb_ref):
    wt = jnp.concatenate([wa_ref[GATE_RANK:, :], wb_ref[...]], axis=0).astype(BF16)
    wtb_ref[...] = wt
    acc = lax.dot_general(h_ref[...], wt, (((1,), (1,)), ((), ())), preferred_element_type=F32)
    z_ref[...] = acc.reshape(z_ref.shape)


def _in_proj_tail_cast(h, wt_all, layer, n_seq, t):
    rows, k = h.shape
    assert rows == TM
    bs, tt = _row_tiling(t, TM)
    tn = IN_TN[0]
    return pl.pallas_call(
        _in_proj_tail_cast_kernel,
        out_shape=(jax.ShapeDtypeStruct((n_seq, t, P_TAIL), F32), jax.ShapeDtypeStruct((P_TAIL, k), BF16)),
        grid=(P_TAIL // tn,),
        in_specs=[pl.BlockSpec((TM, k), lambda j: (0, 0)),
                  pl.BlockSpec((None, tn, k), lambda j: (layer, A_OFF // tn + j, 0)),
                  pl.BlockSpec((None, GATE_RANK, k),
                               lambda j: (layer, (A_OFF + tn * (j + 1)) // GATE_RANK, 0))],
        out_specs=(pl.BlockSpec((bs, tt, tn), lambda j: (0, 0, j)),
                   pl.BlockSpec((tn, k), lambda j: (j, 0))),
        compiler_params=_cparams(("parallel",)),
        name="in_proj_tail_cast")(h, wt_all, wt_all)


def _out_proj_kernel(ya_ref, yb_ref, yc_ref, w_ref, x_ref, gt_ref, o_ref, wb_ref=None):
    if wb_ref is not None:
        wb_ref[...] = w_ref[...].astype(BF16)
        w_ref = wb_ref
    acc = jnp.dot(ya_ref[...], w_ref[0:W_A, :], preferred_element_type=F32)
    acc += jnp.dot(yb_ref[...], w_ref[W_A:W_A + W_B, :], preferred_element_type=F32)
    acc += jnp.dot(yc_ref[...], w_ref[W_A + W_B:, :], preferred_element_type=F32)
    o_ref[...] = x_ref[...] + gt_ref[...] * acc.reshape(o_ref.shape)


def _out_proj(ya, yb, yc, w, layer, x, mod3, gt_idx):
    n_seq, t, d = x.shape
    rows = n_seq * t
    emit = w.dtype != BF16
    bs, tt = _row_tiling(t, TM)
    nt = t // tt
    tn = OUT_TN[0] if emit else OUT_TN[1]
    xo_spec = pl.BlockSpec((bs, tt, tn), lambda i, j: (i // nt, i % nt, j))
    out_shape = [jax.ShapeDtypeStruct(x.shape, F32)]
    out_specs = [xo_spec]
    if emit:
        assert rows == TM
        out_shape.append(jax.ShapeDtypeStruct((d, d), BF16))
        out_specs.append(pl.BlockSpec((d, tn), lambda i, j: (0, j)))
    res = pl.pallas_call(
        _out_proj_kernel, out_shape=out_shape,
        grid=(rows // TM, d // tn),
        in_specs=[pl.BlockSpec((TM, W_A), lambda i, j: (i, 0)),
                  pl.BlockSpec((TM, W_B), lambda i, j: (i, 0)),
                  pl.BlockSpec((TM, W_C), lambda i, j: (i, 0)),
                  _w_spec(w, layer, d, tn, lambda i, j: j),
                  xo_spec,
                  pl.BlockSpec((bs, 1, tn), lambda i, j: (i // nt, 0, gt_idx * (d // tn) + j))],
        out_specs=out_specs,
        compiler_params=_cparams(("parallel", "parallel")),
        name="out_proj_cast" if emit else "out_proj")(ya, yb, yc, w, x, mod3)
    return (res[0], res[1]) if emit else (res[0], None)


def _swiglu_kernel(h_ref, wg_ref, wu_ref, a_ref, wgb_ref=None, wub_ref=None):
    wg = _bf16_tile(wg_ref, wgb_ref)
    wu = _bf16_tile(wu_ref, wub_ref)
    rows = h_ref.shape[0]
    part = SWIGLU_PART
    for r0 in range(0, rows, part):
        h = h_ref[r0:r0 + part, :]
        g = jnp.dot(h, wg, preferred_element_type=F32)
        u = jnp.dot(h, wu, preferred_element_type=F32)
        a_ref[r0:r0 + part, :] = (_silu(g) * u).astype(BF16)


def _swiglu(h, wg, wu, layer):
    rows, k = h.shape
    n = wg.shape[-1]
    emit = wg.dtype != BF16
    tn = FFN_TN
    tm = TM if emit or rows % (2 * TM) else 2 * TM
    out_shape = [jax.ShapeDtypeStruct((rows, n), BF16)]
    out_specs = [pl.BlockSpec((tm, tn), lambda i, j: (i, j))]
    if emit:
        assert rows == TM
        out_shape += [jax.ShapeDtypeStruct((k, n), BF16)] * 2
        out_specs += [pl.BlockSpec((k, tn), lambda i, j: (0, j))] * 2
    res = pl.pallas_call(
        _swiglu_kernel, out_shape=out_shape,
        grid=(rows // tm, n // tn),
        in_specs=[pl.BlockSpec((tm, k), lambda i, j: (i, 0)),
                  _w_spec(wg, layer, k, tn, lambda i, j: j),
                  _w_spec(wu, layer, k, tn, lambda i, j: j)],
        out_specs=out_specs,
        compiler_params=_cparams(("parallel", "parallel")),
        name="swiglu_cast" if emit else "swiglu")(h, wg, wu)
    return (res[0], res[1], res[2]) if emit else (res[0], None, None)


def _down_proj_cast_kernel(a_ref, w_ref, x_ref, gt_ref, o_ref, wb_ref, acc_ref):
    kk = pl.program_id(0)
    j = pl.program_id(1)
    part = jnp.dot(a_ref[...], _bf16_tile(w_ref, wb_ref), preferred_element_type=F32)

    @pl.when(kk == 0)
    def _():
        acc_ref[j] = part

    @pl.when(kk == 1)
    def _():
        o_ref[...] = x_ref[...] + gt_ref[...] * (acc_ref[j] + part).reshape(o_ref.shape)


def _down_proj_kernel(a_ref, w_ref, x_ref, gt_ref, o_ref):
    acc = jnp.dot(a_ref[...], w_ref[...], preferred_element_type=F32)
    o_ref[...] = x_ref[...] + gt_ref[...] * acc.reshape(o_ref.shape)


def _down_proj(a, w, layer, x, mod3, gt_idx):
    n_seq, t, d = x.shape
    rows, k = a.shape
    bs, tt = _row_tiling(t, TM)
    nt = t // tt
    tn = DOWN_TN
    n_j = d // tn
    if w.dtype == BF16:
        xo_spec = pl.BlockSpec((bs, tt, tn), lambda i, j: (i // nt, i % nt, j))
        out = pl.pallas_call(
            _down_proj_kernel, out_shape=jax.ShapeDtypeStruct(x.shape, F32),
            grid=(rows // TM, n_j),
            in_specs=[pl.BlockSpec((TM, k), lambda i, j: (i, 0)),
                      pl.BlockSpec((k, tn), lambda i, j: (0, j)),
                      xo_spec,
                      pl.BlockSpec((bs, 1, tn), lambda i, j: (i // nt, 0, gt_idx * n_j + j))],
            out_specs=xo_spec,
            compiler_params=pltpu.CompilerParams(dimension_semantics=("parallel", "parallel"),
                                                 vmem_limit_bytes=DOWN_PROJ_VMEM_LIMIT),
            name="down_proj")(a, w, x, mod3)
        return out, None
    assert rows == TM
    tk = k // 2
    xo_spec = pl.BlockSpec((bs, tt, tn), lambda kk, j: (0, 0, j * kk))
    out, wb = pl.pallas_call(
        _down_proj_cast_kernel,
        out_shape=(jax.ShapeDtypeStruct(x.shape, F32), jax.ShapeDtypeStruct((k, d), BF16)),
        grid=(2, n_j),
        in_specs=[pl.BlockSpec((TM, tk), lambda kk, j: (0, kk), pipeline_mode=pl.Buffered(1)),
                  pl.BlockSpec((None, tk, tn), lambda kk, j: (layer, kk, j)),
                  xo_spec,
                  pl.BlockSpec((bs, 1, tn), lambda kk, j: (0, 0, gt_idx * n_j + j * kk))],
        out_specs=(xo_spec, pl.BlockSpec((tk, tn), lambda kk, j: (kk, j))),
        scratch_shapes=[pltpu.VMEM((n_j, TM, tn), F32)],
        compiler_params=_cparams(("arbitrary", "arbitrary")),
        name="down_proj_cast")(a, w, x, mod3)
    return out, wb


def _mix_ac_kernel(u_ref, v_ref, b_ref, c_ref, hc_ref, wt_ref, bias_ref, wc_ref, buf_ref,
                   ya_ref, yc_ref, bufnew_ref, *rest, seq_len, phases=PHASES, defer_to=None):
    zbuf = rest[-1]
    vn_ref = rest[0] if len(rest) == 2 else None
    bs, tt, w = u_ref.shape
    rows = bs * tt
    t = pl.program_id(T_AXIS)
    pad = 8

    if "init" in phases:
        @pl.when(t == 0)
        def _():
            zbuf[:, pad - 2:pad, :] = buf_ref[...]

    if "body" in phases:
        work = _mix_ac_work(u_ref, v_ref, b_ref, c_ref, hc_ref, wt_ref, bias_ref, wc_ref, ya_ref, yc_ref,
                            vn_ref, zbuf, pad, seq_len)
        if defer_to is None:
            for piece in work:
                piece()
        else:
            defer_to.extend(work)

    if "final" in phases:
        @pl.when(t == pl.num_programs(T_AXIS) - 1)
        def _():
            bufnew_ref[...] = zbuf[:, pad - 2:pad, :]


def _mix_ac_work(u_ref, v_ref, b_ref, c_ref, hc_ref, wt_ref, bias_ref, wc_ref, ya_ref, yc_ref, vn_ref, zbuf,
                 pad, seq_len):
    bs, tt, _ = u_ref.shape
    rows = bs * tt

    def piece(g):
        sl = slice(g * DH_A, (g + 1) * DH_A)
        zc = c_ref[:, :, sl] * hc_ref[:, :, sl]
        zbuf[:, pad:pad + tt, sl] = zc
        y = zbuf[:, pad - 2:pad - 2 + tt, sl] * wc_ref[0:1, sl]
        y = y + zbuf[:, pad - 1:pad - 1 + tt, sl] * wc_ref[1:2, sl]
        y = y + zc * wc_ref[2:3, sl]
        yc_ref[:, sl] = (b_ref[:, :, sl] * y).reshape(rows, DH_A).astype(BF16)
        zbuf[:, pad - 2:pad, sl] = zbuf[:, pad + tt - 2:pad + tt, sl]

        gu = _gelu_tanh(u_ref[:, :, sl].reshape(rows, DH_A))
        vg = _gelu_tanh(v_ref[:, :, sl].reshape(rows, DH_A))
        dv = vg - jnp.mean(vg, axis=-1, keepdims=True)
        vn = dv * lax.rsqrt(jnp.mean(dv * dv, axis=-1, keepdims=True) + EPS)
        ri = lax.broadcasted_iota(jnp.int32, (rows, rows), 0)
        ci = lax.broadcasted_iota(jnp.int32, (rows, rows), 1)
        keep = ci <= ri
        if seq_len < rows:
            keep = keep & ((ri // seq_len) == (ci // seq_len))
        wm = jnp.where(keep, wt_ref[g], 0.0).astype(BF16)
        mixed = jnp.dot(wm, vn.astype(BF16), preferred_element_type=F32) + bias_ref[:, sl]
        ya_ref[:, sl] = (gu * mixed).astype(BF16)
        if vn_ref is not None:
            vn_ref[:, :, sl] = vn.reshape(bs, tt, DH_A)

    return [functools.partial(piece, g) for g in range(G_A)]


def _mix_ac_parts(z_head, z_tail, wt, bias_full, w_conv, layer, buf, emit_v):
    n_seq, t, _ = z_head.shape
    bs, tt = _row_tiling(t, ROW_TILE)
    nt = t // tt
    rows = n_seq * t
    zspec = lambda c: pl.BlockSpec((bs, tt, W_A), lambda i, j, *_: (i, j, c))
    y_spec = pl.BlockSpec((ROW_TILE, W_A), lambda i, j, *_: (i * nt + j, 0))
    out_shape = [jax.ShapeDtypeStruct((rows, W_A), BF16),
                 jax.ShapeDtypeStruct((rows, W_C), BF16),
                 jax.ShapeDtypeStruct((n_seq, CONV_W - 1, W_C), F32)]
    out_specs = [y_spec, y_spec, pl.BlockSpec((bs, CONV_W - 1, W_C), lambda i, j, *_: (i, 0, 0))]
    if emit_v:
        out_shape.append(jax.ShapeDtypeStruct((n_seq, t, W_A), F32))
        out_specs.append(pl.BlockSpec((bs, tt, W_A), lambda i, j, *_: (i, j, 0)))
    in_specs = [zspec(0), zspec(1), zspec(0), zspec(1), zspec(2),
                pl.BlockSpec((G_A, ROW_TILE, ROW_TILE), lambda *_: (0, 0, 0)),
                pl.BlockSpec((ROW_TILE, W_A), lambda *_: (0, 0)),
                pl.BlockSpec((None, CONV_W, W_C), lambda *_: (layer, 0, 0)),
                pl.BlockSpec((None, bs, CONV_W - 1, W_C), lambda i, *_: (layer, i, 0, 0))]
    return dict(args=[z_head, z_head, z_tail, z_tail, z_tail, wt, bias_full, w_conv, buf],
                in_specs=in_specs, out_shape=out_shape, out_specs=out_specs,
                scratch=[pltpu.VMEM((bs, tt + 8, W_C), F32)], grid=(n_seq // bs, nt),
                kernel=functools.partial(_mix_ac_kernel, seq_len=min(t, CHUNK)))


def _mix_ac(*operands):
    m = _mix_ac_parts(*operands)
    return pl.pallas_call(
        m["kernel"], out_shape=m["out_shape"], grid=m["grid"], in_specs=m["in_specs"],
        out_specs=m["out_specs"], scratch_shapes=m["scratch"],
        compiler_params=_cparams(("parallel", "arbitrary")), name="mix_ac")(*m["args"])


def _gla_tables(rows, seq_len):
    i = np.arange(rows)[:, None]
    t = np.arange(rows)[None, :]
    window = min(seq_len, GLA_WINDOW)
    slabs = [((i // window) == (t // window)) & (t <= i)]
    masks = [i == t]
    s = seq_len // 2
    while s >= 1:
        second = (i // s) % 2 == 1
        start2 = (i // s) * s
        end1 = start2 + s - 1
        if s < MIN_REF_LEVEL:
            slabs.append(np.where(second, (t >= start2) & (t <= i), (t > i) & (t <= end1)))
        masks.append(((i // (2 * s)) == (t // (2 * s))) & second & ((t // s) % 2 == 0))
        s //= 2
    return (np.concatenate(slabs, 0).astype(np.float32), np.stack(masks).astype(np.float32))


def _block_ref_exponent(g, block, ref_row, flip_from):
    rows, dk = g.shape
    gr = g.reshape(rows // block, block, dk)
    d = gr - gr[:, ref_row:ref_row + 1, :]
    pos = lax.broadcasted_iota(jnp.int32, gr.shape, 1)
    return jnp.where(pos >= flip_from, d, -d).reshape(rows, dk)


def _double_restart(pre, suf, s):
    rows, dk = pre.shape
    pr = pre.reshape(rows // (2 * s), 2 * s, dk)
    sr = suf.reshape(rows // (2 * s), 2 * s, dk)
    second = lax.broadcasted_iota(jnp.int32, pr.shape, 1) >= s
    x = jnp.where(second, pr, sr)
    pre2 = jnp.where(second, pr + pr[:, s - 1:s, :], pr)
    suf2 = jnp.where(second, sr, sr + pr[:, 2 * s - 1:2 * s, :])
    return x.reshape(rows, dk), pre2.reshape(rows, dk), suf2.reshape(rows, dk)


def _gla_kernel(q_ref, k_ref, v_ref, r_ref, la_ref, mall_ref, mask_ref, g_ref, s0_ref, *rest, phases=PHASES,
                between=()):
    y_ref, snew_ref, s_scr = rest[-3:]
    t = pl.program_id(T_AXIS)

    if "init" in phases:
        @pl.when(t == 0)
        def _():
            s_scr[...] = s0_ref[...]

    if "body" in phases:
        _gla_body(q_ref, k_ref, v_ref, r_ref, la_ref, mall_ref, mask_ref, g_ref, y_ref, s_scr, between)

    if "final" in phases:
        @pl.when(t == pl.num_programs(T_AXIS) - 1)
        def _():
            snew_ref[...] = s_scr[...]


def _gla_body(q_ref, k_ref, v_ref, r_ref, la_ref, mall_ref, mask_ref, g_ref, y_ref, s_scr, between):
    bs, tt, _ = q_ref.shape
    hp = s_scr.shape[1]
    assert len(between) in (0, hp)
    rows = bs * tt
    n_lvl = mask_ref.shape[0] - 1
    ri = lax.broadcasted_iota(jnp.int32, (DK_B, DK_B), 0)
    ci = lax.broadcasted_iota(jnp.int32, (DK_B, DK_B), 1)
    eye = ri == ci
    nt_dims = (((1,), (1,)), ((), ()))
    tn_dims = (((0,), (0,)), ((), ()))

    for hh in range(hp):
        ks = slice(hh * DK_B, (hh + 1) * DK_B)
        vs = slice(hh * DV_B, (hh + 1) * DV_B)
        q = q_ref[:, :, ks].reshape(rows, DK_B) * (DK_B ** -0.5)
        k = k_ref[:, :, ks].reshape(rows, DK_B)
        v = v_ref[:, :, vs].reshape(rows, DV_B)
        vb = v.astype(BF16)
        la = la_ref[:, :, ks].reshape(rows, DK_B) * LOG2_E
        la_hi = la.astype(BF16)
        la_lo = (la - la_hi.astype(F32)).astype(BF16)
        x = jnp.dot(mall_ref[...], jnp.concatenate([la_hi, la_lo], axis=1), preferred_element_type=F32)
        x = x[:, :DK_B] + x[:, DK_B:]
        window = min(tt, GLA_WINDOW)
        pre = x[0:rows]
        suf = _block_ref_exponent(pre, window, window - 1, window)

        a = mask_ref[0] * lax.dot_general(q.astype(BF16), k.astype(BF16), nt_dims,
                                          preferred_element_type=F32)
        for lvl in reversed(range(n_lvl)):
            s = tt >> (lvl + 1)
            if s >= window:
                xl, pre, suf = _double_restart(pre, suf, s)
            elif s >= MIN_REF_LEVEL:
                xl = _block_ref_exponent(pre, 2 * s, s - 1, s)
            else:
                n_slab = 1 + lvl - (n_lvl - N_SMALL_LEVELS)
                xl = x[n_slab * rows:(n_slab + 1) * rows]
            el = jnp.exp2(xl)
            p = lax.dot_general((q * el).astype(BF16), (k * el).astype(BF16), nt_dims,
                                preferred_element_type=F32)
            a = a + mask_ref[lvl + 1] * p
        o_intra = jnp.dot(a.astype(BF16), vb, preferred_element_type=F32)
        e_g = jnp.exp2(pre)
        qg = q * e_g
        kd = k * jnp.exp2(suf)

        o_parts = []
        for b in range(bs):
            rs = slice(b * tt, (b + 1) * tt)
            s = s_scr[b, hh]
            o_parts.append(o_intra[rs] + jnp.dot(qg[rs].astype(BF16), s.astype(BF16),
                                                 preferred_element_type=F32))
            e_last = e_g[(b + 1) * tt - 1:(b + 1) * tt]
            e_col = jnp.sum(jnp.where(eye, jnp.broadcast_to(e_last, (DK_B, DK_B)), 0.0),
                            axis=1, keepdims=True)
            kv = lax.dot_general(kd[rs].astype(BF16), v[rs].astype(BF16), tn_dims,
                                 preferred_element_type=F32)
            s_scr[b, hh] = e_col * s + kv
        o = o_parts[0] if bs == 1 else jnp.concatenate(o_parts, axis=0)
        yn = o * lax.rsqrt(jnp.mean(o * o, axis=-1, keepdims=True) + EPS) * g_ref[...]
        y_ref[:, vs] = (yn * _silu(r_ref[:, :, vs].reshape(rows, DV_B))).astype(BF16)
        if between:
            between[hh]()


def _gla_parts(z_head, la, g_gla, s0, layer, s_out_prev):
    n_seq, t, _ = z_head.shape
    tile = ROW_TILE if t >= ROW_TILE else 2 * ROW_TILE
    bs, tt = _row_tiling(t, tile)
    hp = H_B if bs == 1 else 1
    nt = t // tt
    assert nt == 1 or hp == H_B
    rows = n_seq * t
    mall, masks = _gla_tables(tile, tt)
    mall = jnp.asarray(mall, BF16)
    masks = jnp.asarray(masks, F32)
    zs = lambda w, c0: pl.BlockSpec((bs, tt, hp * w), lambda i, j, h=0: (i, j, c0 // hp + h))
    s_spec = pl.BlockSpec((None, bs, hp, DK_B, DV_B), lambda i, j, h=0: (layer, i, h, 0, 0))
    in_specs = [zs(DK_B, ZQ0), zs(DK_B, ZK0), zs(DV_B, ZV0), zs(DV_B, ZR0),
                pl.BlockSpec((bs, tt, hp * DK_B), lambda i, j, h=0: (i, j, h)),
                pl.BlockSpec(mall.shape, lambda *_: (0, 0)),
                pl.BlockSpec(masks.shape, lambda *_: (0, 0, 0)),
                pl.BlockSpec((None, 1, DV_B), lambda *_: (layer, 0, 0)),
                s_spec]
    args = [z_head, z_head, z_head, z_head, la, mall, masks, g_gla, s0]
    if s_out_prev is not None:
        in_specs.append(pl.BlockSpec(memory_space=pl.ANY))
        args.append(s_out_prev)
    return dict(args=args, in_specs=in_specs, aliased=s_out_prev is not None,
                out_shape=[jax.ShapeDtypeStruct((rows, W_B), BF16),
                           jax.ShapeDtypeStruct((DEPTH, n_seq, H_B, DK_B, DV_B), F32)],
                out_specs=[pl.BlockSpec((tile, hp * DV_B), lambda i, j, h=0: (i * nt + j, h)), s_spec],
                scratch=[pltpu.VMEM((bs, hp, DK_B, DV_B), F32)], grid=(n_seq // bs, nt, H_B // hp))


def _gla(*operands):
    m = _gla_parts(*operands)
    aliases = {len(m["args"]) - 1: 1} if m["aliased"] else {}
    return pl.pallas_call(
        _gla_kernel, out_shape=m["out_shape"], grid=m["grid"], in_specs=m["in_specs"],
        out_specs=m["out_specs"], scratch_shapes=m["scratch"], input_output_aliases=aliases,
        compiler_params=_cparams(("parallel", "arbitrary", "arbitrary")), name="gla")(*m["args"])


def _mix_gla_kernel(*refs, n_in, n_out, mix_kernel):
    (mi, gi), (mo, go) = n_in, n_out
    o0 = mi + gi
    s0 = o0 + mo + go
    mix_refs = (*refs[:mi], *refs[o0:o0 + mo], refs[s0])
    gla_refs = (*refs[mi:o0], *refs[o0 + mo:s0], refs[s0 + 1])
    for phase in PHASES:
        pieces = []
        mix_kernel(*mix_refs, phases=(phase,), defer_to=pieces)
        _gla_kernel(*gla_refs, phases=(phase,), between=pieces)


def _mix_gla(mix_operands, gla_operands):
    m = _mix_ac_parts(*mix_operands)
    g = _gla_parts(*gla_operands)
    assert m["grid"] == g["grid"][:2] and g["grid"][2] == 1
    n_in = (len(m["args"]), len(g["args"]))
    n_out = (len(m["out_shape"]), len(g["out_shape"]))
    aliases = {sum(n_in) - 1: n_out[0] + 1} if g["aliased"] else {}
    res = pl.pallas_call(
        functools.partial(_mix_gla_kernel, n_in=n_in, n_out=n_out, mix_kernel=m["kernel"]),
        out_shape=m["out_shape"] + g["out_shape"], grid=m["grid"],
        in_specs=m["in_specs"] + g["in_specs"], out_specs=m["out_specs"] + g["out_specs"],
        scratch_shapes=m["scratch"] + g["scratch"], input_output_aliases=aliases,
        compiler_params=_cparams(("parallel", "arbitrary")), name="mix_gla")(*m["args"], *g["args"])
    return res[:n_out[0]], res[n_out[0]:]


def _layer(x, layer, mod3, s_gla, buf_conv, p, wts, s_out_prev, emit_v):
    n_seq, t, d = x.shape
    seq_len = min(t, CHUNK)
    reps = ROW_TILE // seq_len
    h, la = _norm_call(x, p["g_mix"], mod3, 1, 0, gate_w=(p["w_a"], p["w_a2"], p["b_a2"]))
    z_head, wb_head = _in_proj(h, wts["w_head"], layer, A_OFF, n_seq, t)
    if wts["w_tail"] is None:
        z_tail, wb_tail = _in_proj_tail_cast(h, wts["w_head"], layer, n_seq, t)
    else:
        z_tail, wb_tail = _in_proj(h, wts["w_tail"], layer, P_TAIL, n_seq, t)
    wt = jnp.tile(p["w_s"][:, :seq_len, :seq_len], (1, reps, reps))
    bias_full = jnp.repeat(jnp.tile(p["b_s"][:, :seq_len].T, (reps, 1)), DH_A, axis=1)
    mix_ops = (z_head, z_tail, wt, bias_full, p["w_conv"], layer, buf_conv, emit_v)
    gla_ops = (z_head, la, p["g_gla"], s_gla, layer, s_out_prev)
    if t >= ROW_TILE:
        mix, (yb, s_out) = _mix_gla(mix_ops, gla_ops)
    else:
        mix = _mix_ac(*mix_ops)
        yb, s_out = _gla(*gla_ops)
    ya, yc, buf_new = mix[:3]
    vn = mix[3] if emit_v else None
    x, wb_out = _out_proj(ya, yb, yc, wts["w_out"], layer, x, mod3, 2)
    h2 = _norm_call(x, p["g_ffn"], mod3, 4, 3)
    a, wb_gate, wb_up = _swiglu(h2, wts["w_gate"], wts["w_up"], layer)
    x, wb_down = _down_proj(a, wts["w_down"], layer, x, mod3, 5)
    wb = dict(w_head=wb_head, w_tail=wb_tail, w_out=wb_out, w_gate=wb_gate, w_up=wb_up, w_down=wb_down)
    return x, s_out, buf_new, vn, wb


def kernel(x_prompt, x_sample, state_gla, state_conv, c_prompt, c_sample, g_mix, g_ffn, w_mod, b_mod,
           w_in, w_s, b_s, w_a2, b_a2, g_gla, w_conv, w_out, w_gate, w_up, w_down, g_final):
    bp = x_prompt.shape[0]
    bd = x_sample.shape[0]
    n_c = bp + bd
    mc = -(-n_c // 16) * 16
    c_all = jnp.concatenate([c_prompt, c_sample, jnp.zeros((mc - n_c, D_MODEL), F32)], axis=0)
    mod = _modulation(c_all, w_mod, b_mod)
    gf = g_final.reshape(1, D_MODEL)
    g_gla3 = g_gla.reshape(DEPTH, 1, DV_B)
    gla0 = jnp.zeros((DEPTH, bp, H_B, DK_B, DV_B), F32)
    conv0 = jnp.zeros((DEPTH, bp, CONV_W - 1, W_C), F32)

    xs, xp = x_sample, x_prompt
    gla_s = gla_p = None
    conv_s, conv_p, v_s = [], [], []
    wt_in = jnp.swapaxes(w_in, 1, 2)
    for l in range(DEPTH):
        w_a = wt_in[l, A_OFF:A_OFF + GATE_RANK, :].T
        p = dict(
            g_mix=g_mix[l].reshape(1, D_MODEL), g_ffn=g_ffn[l].reshape(1, D_MODEL),
            w_a=jnp.pad(w_a, ((0, 0), (0, LANES - GATE_RANK))),
            w_a2=jnp.pad(w_a2[l], ((0, LANES - GATE_RANK), (0, 0))).astype(BF16),
            b_a2=b_a2[l].reshape(1, KW_B),
            w_s=w_s[l], b_s=b_s[l], g_gla=g_gla3, w_conv=w_conv)
        w_f32 = dict(w_head=wt_in, w_tail=None, w_out=w_out,
                     w_gate=w_gate, w_up=w_up, w_down=w_down)
        mod_s = mod[l, bp:n_c].reshape(bd, 1, N_MOD * D_MODEL)
        mod_p = mod[l, :bp].reshape(bp, 1, N_MOD * D_MODEL)
        xs, gla_s, buf_s, vn_s, w_bf16 = _layer(xs, l, mod_s, state_gla, state_conv, p, w_f32, gla_s, True)
        xp, gla_p, buf_p, _, _ = _layer(xp, l, mod_p, gla0, conv0, p, w_bf16, gla_p, False)
        conv_s.append(buf_s)
        conv_p.append(buf_p)
        v_s.append(vn_s)
    y_s = _final_norm(xs, gf)
    y_p = _final_norm(xp, gf)
    return (y_p, y_s, gla_p, jnp.stack(conv_p), gla_s, jnp.stack(conv_s), jnp.stack(v_s))
```

```python
import functools

import numpy as np
import jax
import jax.numpy as jnp
from jax import lax
from jax.experimental import pallas as pl
from jax.experimental.pallas import tpu as pltpu

F32 = jnp.float32
BF16 = jnp.bfloat16

D_MODEL = 4096
DEPTH = 2
EPS = 1e-6
CHUNK = 128
W_A = D_MODEL // 4
G_A = 8
DH_A = W_A // G_A
W_B = D_MODEL // 2
H_B = 8
DV_B = W_B // H_B
DK_B = DV_B // 2
KW_B = H_B * DK_B
GATE_RANK = 16
GATE_TAU = 16.0
W_C = D_MODEL // 4
CONV_W = 3
D_FF = -(-8 * D_MODEL // (3 * 256)) * 256
N_MOD = 6
A_OFF = 2 * W_A + 2 * KW_B + 2 * W_B
P_TAIL = 3 * W_C

LOG2_E = float(np.log2(np.e))
LANES = 128
ROW_TILE = 128
MIN_REF_LEVEL = 4
N_SMALL_LEVELS = MIN_REF_LEVEL.bit_length() - 1
GLA_WINDOW = 32
TM = 1024
IN_TN = (512, 1024)
OUT_TN = (512, 1024)
FFN_TN = 256
DOWN_TN = 256
MOD_TN = 512
NORM_ROWS = (256, 512)
SWIGLU_PART = 512
T_AXIS = 1
PHASES = ("init", "body", "final")
VMEM_LIMIT = 56 * 1024 * 1024
DOWN_PROJ_VMEM_LIMIT = 62 * 1024 * 1024

ZQ0 = (2 * W_A) // DK_B
ZK0 = (2 * W_A + KW_B) // DK_B
ZV0 = (2 * W_A + 2 * KW_B) // DV_B
ZR0 = (2 * W_A + 2 * KW_B + W_B) // DV_B


def _cparams(sem):
    return pltpu.CompilerParams(dimension_semantics=sem, vmem_limit_bytes=VMEM_LIMIT)


def _silu(x):
    return x / (1.0 + jnp.exp(-x))


def _gelu_tanh(x):
    c = np.float32(np.sqrt(2.0 / np.pi))
    return 0.5 * x * (1.0 + jnp.tanh(c * (x + 0.044715 * (x * x * x))))


def _row_tiling(t, rows):
    if t >= rows:
        return 1, rows
    return rows // t, t


def _w_spec(w, layer, k, tn, col_map):
    if w.ndim == 2:
        return pl.BlockSpec((k, tn), lambda *g: (0, col_map(*g)))
    return pl.BlockSpec((None, k, tn), lambda *g: (layer, 0, col_map(*g)))


def _mod_kernel(c_ref, w_ref, b_ref, o_ref):
    s = _silu(c_ref[...]).astype(BF16)
    o_ref[0] = jnp.dot(s, w_ref[0].astype(BF16), preferred_element_type=F32) + b_ref[0]


def _modulation(c_all, w_mod, b_mod):
    mc = c_all.shape[0]
    n = N_MOD * D_MODEL
    tn = MOD_TN
    return pl.pallas_call(
        _mod_kernel,
        out_shape=jax.ShapeDtypeStruct((DEPTH, mc, n), F32),
        grid=(DEPTH, n // tn),
        in_specs=[
            pl.BlockSpec((mc, D_MODEL), lambda l, j: (0, 0)),
            pl.BlockSpec((1, D_MODEL, tn), lambda l, j: (l, 0, j)),
            pl.BlockSpec((1, 1, tn), lambda l, j: (l, 0, j)),
        ],
        out_specs=pl.BlockSpec((1, mc, tn), lambda l, j: (l, 0, j)),
        compiler_params=_cparams(("parallel", "parallel")),
        name="modulation",
    )(c_all, w_mod, b_mod.reshape(DEPTH, 1, n))


def _norm_mod(x, g, sc, sh):
    y = x * lax.rsqrt(jnp.mean(x * x, axis=-1, keepdims=True) + EPS) * g
    return y * (1.0 + sc) + sh


def _norm_gate_kernel(x_ref, g_ref, sc_ref, sh_ref, wa_ref, wa2_ref, ba2_ref, h_ref, la_ref):
    bs, tt, d = x_ref.shape
    h = _norm_mod(x_ref[...], g_ref[...], sc_ref[...], sh_ref[...]).reshape(bs * tt, d).astype(BF16)
    h_ref[...] = h
    a = jnp.dot(h, wa_ref[...].astype(BF16), preferred_element_type=F32)
    pre = jnp.dot(a.astype(BF16), wa2_ref[...], preferred_element_type=F32) + ba2_ref[...]
    log_sig = jnp.minimum(pre, 0.0) - jnp.log(1.0 + jnp.exp(-jnp.abs(pre)))
    la_ref[...] = (log_sig / GATE_TAU).reshape(bs, tt, KW_B)


def _norm_kernel(x_ref, g_ref, sc_ref, sh_ref, h_ref):
    bs, tt, d = x_ref.shape
    h_ref[...] = _norm_mod(x_ref[...], g_ref[...], sc_ref[...], sh_ref[...]).reshape(bs * tt, d).astype(BF16)


def _streamed_call(body, grid, in_specs, out_spec, out_shape, name, *operands):
    def kernel(*refs):
        pltpu.emit_pipeline(body, grid=grid, in_specs=in_specs, out_specs=[out_spec])(*refs)

    any_spec = pl.BlockSpec(memory_space=pl.ANY)
    return pl.pallas_call(
        kernel, out_shape=out_shape, in_specs=[any_spec] * len(operands), out_specs=any_spec,
        compiler_params=pltpu.CompilerParams(vmem_limit_bytes=VMEM_LIMIT), name=name)(*operands)


def _norm_call(x, g, mod3, sc_idx, sh_idx, gate_w=None):
    n_seq, t, d = x.shape
    bs, tt = _row_tiling(t, NORM_ROWS[t >= NORM_ROWS[1]])
    rows = bs * tt
    nt = t // tt
    grid = (n_seq // bs, nt)
    x_spec = pl.BlockSpec((bs, tt, d), lambda i, j: (i, j, 0))
    g_spec = pl.BlockSpec((1, d), lambda i, j: (0, 0))
    mod_spec = lambda c: pl.BlockSpec((bs, 1, d), lambda i, j: (i, 0, c))
    h_spec = pl.BlockSpec((rows, d), lambda i, j: (i * nt + j, 0))
    h_shape = jax.ShapeDtypeStruct((n_seq * t, d), BF16)
    if gate_w is None:
        x3_spec = pl.BlockSpec((bs, tt, d), lambda i, j: (i, j, 0), pipeline_mode=pl.Buffered(3))
        return _streamed_call(_norm_kernel, grid, [x3_spec, g_spec, mod_spec(sc_idx), mod_spec(sh_idx)],
                              h_spec, h_shape, "norm_mod", x, g, mod3, mod3)
    wa, wa2, ba2 = gate_w
    return pl.pallas_call(
        _norm_gate_kernel,
        out_shape=(h_shape, jax.ShapeDtypeStruct((n_seq, t, KW_B), F32)),
        grid=grid,
        in_specs=[x_spec, g_spec, mod_spec(sc_idx), mod_spec(sh_idx),
                  pl.BlockSpec((d, LANES), lambda i, j: (0, 0)),
                  pl.BlockSpec((LANES, KW_B), lambda i, j: (0, 0)),
                  pl.BlockSpec((1, KW_B), lambda i, j: (0, 0))],
        out_specs=(h_spec, pl.BlockSpec((bs, tt, KW_B), lambda i, j: (i, j, 0))),
        compiler_params=_cparams(("parallel", "parallel")),
        name="norm_mod_gate")(x, g, mod3, mod3, wa, wa2, ba2)


def _final_norm_kernel(x_ref, g_ref, o_ref):
    x = x_ref[...]
    o_ref[...] = x * lax.rsqrt(jnp.mean(x * x, axis=-1, keepdims=True) + EPS) * g_ref[...]


def _final_norm(x, g):
    n_seq, t, d = x.shape
    bs, tt = _row_tiling(t, NORM_ROWS[t >= NORM_ROWS[1]])
    return _streamed_call(
        _final_norm_kernel, (n_seq // bs, t // tt),
        [pl.BlockSpec((bs, tt, d), lambda i, j: (i, j, 0), pipeline_mode=pl.Buffered(3)),
         pl.BlockSpec((1, d), lambda i, j: (0, 0))],
        pl.BlockSpec((bs, tt, d), lambda i, j: (i, j, 0)),
        jax.ShapeDtypeStruct(x.shape, F32), "final_norm", x, g)


def _bf16_tile(w_ref, wb_ref):
    if wb_ref is None:
        return w_ref[...]
    wb = w_ref[...].astype(BF16)
    wb_ref[...] = wb
    return wb


def _in_proj_kernel(h_ref, wt_ref, z_ref, wtb_ref=None):
    acc = lax.dot_general(h_ref[...], _bf16_tile(wt_ref, wtb_ref), (((1,), (1,)), ((), ())),
                          preferred_element_type=F32)
    z_ref[...] = acc.reshape(z_ref.shape)


def _in_proj(h, wt, layer, n_cols, n_seq, t):
    rows, k = h.shape
    emit = wt.dtype != BF16
    bs, tt = _row_tiling(t, TM)
    nt = t // tt
    tn = IN_TN[0] if emit else IN_TN[1]
    if wt.ndim == 2:
        w_spec = pl.BlockSpec((tn, k), lambda i, j: (j, 0))
    else:
        w_spec = pl.BlockSpec((None, tn, k), lambda i, j: (layer, j, 0))
    out_shape = [jax.ShapeDtypeStruct((n_seq, t, n_cols), F32)]
    out_specs = [pl.BlockSpec((bs, tt, tn), lambda i, j: (i // nt, i % nt, j))]
    if emit:
        assert rows == TM
        out_shape.append(jax.ShapeDtypeStruct((n_cols, k), BF16))
        out_specs.append(pl.BlockSpec((tn, k), lambda i, j: (j, 0)))
    res = pl.pallas_call(
        _in_proj_kernel, out_shape=out_shape,
        grid=(rows // TM, n_cols // tn),
        in_specs=[pl.BlockSpec((TM, k), lambda i, j: (i, 0)), w_spec],
        out_specs=out_specs,
        compiler_params=_cparams(("parallel", "parallel")),
        name="in_proj_cast" if emit else "in_proj")(h, wt)
    return (res[0], res[1]) if emit else (res[0], None)


def _in_proj_tail_cast_kernel(h_ref, wa_ref, wb_ref, z_ref, wtb_ref):
    wt = jnp.concatenate([wa_ref[GATE_RANK:, :], wb_ref[...]], axis=0).astype(BF16)
    wtb_ref[...] = wt
    acc = lax.dot_general(h_ref[...], wt, (((1,), (1,)), ((), ())), preferred_element_type=F32)
    z_ref[...] = acc.reshape(z_ref.shape)


def _in_proj_tail_cast(h, wt_all, layer, n_seq, t):
    rows, k = h.shape
    assert rows == TM
    bs, tt = _row_tiling(t, TM)
    tn = IN_TN[0]
    return pl.pallas_call(
        _in_proj_tail_cast_kernel,
        out_shape=(jax.ShapeDtypeStruct((n_seq, t, P_TAIL), F32), jax.ShapeDtypeStruct((P_TAIL, k), BF16)),
        grid=(P_TAIL // tn,),
        in_specs=[pl.BlockSpec((TM, k), lambda j: (0, 0)),
                  pl.BlockSpec((None, tn, k), lambda j: (layer, A_OFF // tn + j, 0)),
                  pl.BlockSpec((None, GATE_RANK, k),
                               lambda j: (layer, (A_OFF + tn * (j + 1)) // GATE_RANK, 0))],
        out_specs=(pl.BlockSpec((bs, tt, tn), lambda j: (0, 0, j)),
                   pl.BlockSpec((tn, k), lambda j: (j, 0))),
        compiler_params=_cparams(("parallel",)),
        name="in_proj_tail_cast")(h, wt_all, wt_all)


def _out_proj_kernel(ya_ref, yb_ref, yc_ref, w_ref, x_ref, gt_ref, o_ref, wb_ref=None):
    if wb_ref is not None:
        wb_ref[...] = w_ref[...].astype(BF16)
        w_ref = wb_ref
    acc = jnp.dot(ya_ref[...], w_ref[0:W_A, :], preferred_element_type=F32)
    acc += jnp.dot(yb_ref[...], w_ref[W_A:W_A + W_B, :], preferred_element_type=F32)
    acc += jnp.dot(yc_ref[...], w_ref[W_A + W_B:, :], preferred_element_type=F32)
    o_ref[...] = x_ref[...] + gt_ref[...] * acc.reshape(o_ref.shape)


def _out_proj(ya, yb, yc, w, layer, x, mod3, gt_idx):
    n_seq, t, d = x.shape
    rows = n_seq * t
    emit = w.dtype != BF16
    bs, tt = _row_tiling(t, TM)
    nt = t // tt
    tn = OUT_TN[0] if emit else OUT_TN[1]
    xo_spec = pl.BlockSpec((bs, tt, tn), lambda i, j: (i // nt, i % nt, j))
    out_shape = [jax.ShapeDtypeStruct(x.shape, F32)]
    out_specs = [xo_spec]
    if emit:
        assert rows == TM
        out_shape.append(jax.ShapeDtypeStruct((d, d), BF16))
        out_specs.append(pl.BlockSpec((d, tn), lambda i, j: (0, j)))
    res = pl.pallas_call(
        _out_proj_kernel, out_shape=out_shape,
        grid=(rows // TM, d // tn),
        in_specs=[pl.BlockSpec((TM, W_A), lambda i, j: (i, 0)),
                  pl.BlockSpec((TM, W_B), lambda i, j: (i, 0)),
                  pl.BlockSpec((TM, W_C), lambda i, j: (i, 0)),
                  _w_spec(w, layer, d, tn, lambda i, j: j),
                  xo_spec,
                  pl.BlockSpec((bs, 1, tn), lambda i, j: (i // nt, 0, gt_idx * (d // tn) + j))],
        out_specs=out_specs,
        compiler_params=_cparams(("parallel", "parallel")),
        name="out_proj_cast" if emit else "out_proj")(ya, yb, yc, w, x, mod3)
    return (res[0], res[1]) if emit else (res[0], None)


def _swiglu_kernel(h_ref, wg_ref, wu_ref, a_ref, wgb_ref=None, wub_ref=None):
    wg = _bf16_tile(wg_ref, wgb_ref)
    wu = _bf16_tile(wu_ref, wub_ref)
    rows = h_ref.shape[0]
    part = SWIGLU_PART
    for r0 in range(0, rows, part):
        h = h_ref[r0:r0 + part, :]
        g = jnp.dot(h, wg, preferred_element_type=F32)
        u = jnp.dot(h, wu, preferred_element_type=F32)
        a_ref[r0:r0 + part, :] = (_silu(g) * u).astype(BF16)


def _swiglu(h, wg, wu, layer):
    rows, k = h.shape
    n = wg.shape[-1]
    emit = wg.dtype != BF16
    tn = FFN_TN
    tm = TM if emit or rows % (2 * TM) else 2 * TM
    out_shape = [jax.ShapeDtypeStruct((rows, n), BF16)]
    out_specs = [pl.BlockSpec((tm, tn), lambda i, j: (i, j))]
    if emit:
        assert rows == TM
        out_shape += [jax.ShapeDtypeStruct((k, n), BF16)] * 2
        out_specs += [pl.BlockSpec((k, tn), lambda i, j: (0, j))] * 2
    res = pl.pallas_call(
        _swiglu_kernel, out_shape=out_shape,
        grid=(rows // tm, n // tn),
        in_specs=[pl.BlockSpec((tm, k), lambda i, j: (i, 0)),
                  _w_spec(wg, layer, k, tn, lambda i, j: j),
                  _w_spec(wu, layer, k, tn, lambda i, j: j)],
        out_specs=out_specs,
        compiler_params=_cparams(("parallel", "parallel")),
        name="swiglu_cast" if emit else "swiglu")(h, wg, wu)
    return (res[0], res[1], res[2]) if emit else (res[0], None, None)


def _down_proj_cast_kernel(a_ref, w_ref, x_ref, gt_ref, o_ref, wb_ref, acc_ref):
    kk = pl.program_id(0)
    j = pl.program_id(1)
    part = jnp.dot(a_ref[...], _bf16_tile(w_ref, wb_ref), preferred_element_type=F32)

    @pl.when(kk == 0)
    def _():
        acc_ref[j] = part

    @pl.when(kk == 1)
    def _():
        o_ref[...] = x_ref[...] + gt_ref[...] * (acc_ref[j] + part).reshape(o_ref.shape)


def _down_proj_kernel(a_ref, w_ref, x_ref, gt_ref, o_ref):
    acc = jnp.dot(a_ref[...], w_ref[...], preferred_element_type=F32)
    o_ref[...] = x_ref[...] + gt_ref[...] * acc.reshape(o_ref.shape)


def _down_proj(a, w, layer, x, mod3, gt_idx):
    n_seq, t, d = x.shape
    rows, k = a.shape
    bs, tt = _row_tiling(t, TM)
    nt = t // tt
    tn = DOWN_TN
    n_j = d // tn
    if w.dtype == BF16:
        xo_spec = pl.BlockSpec((bs, tt, tn), lambda i, j: (i // nt, i % nt, j))
        out = pl.pallas_call(
            _down_proj_kernel, out_shape=jax.ShapeDtypeStruct(x.shape, F32),
            grid=(rows // TM, n_j),
            in_specs=[pl.BlockSpec((TM, k), lambda i, j: (i, 0)),
                      pl.BlockSpec((k, tn), lambda i, j: (0, j)),
                      xo_spec,
                      pl.BlockSpec((bs, 1, tn), lambda i, j: (i // nt, 0, gt_idx * n_j + j))],
            out_specs=xo_spec,
            compiler_params=pltpu.CompilerParams(dimension_semantics=("parallel", "parallel"),
                                                 vmem_limit_bytes=DOWN_PROJ_VMEM_LIMIT),
            name="down_proj")(a, w, x, mod3)
        return out, None
    assert rows == TM
    tk = k // 2
    xo_spec = pl.BlockSpec((bs, tt, tn), lambda kk, j: (0, 0, j * kk))
    out, wb = pl.pallas_call(
        _down_proj_cast_kernel,
        out_shape=(jax.ShapeDtypeStruct(x.shape, F32), jax.ShapeDtypeStruct((k, d), BF16)),
        grid=(2, n_j),
        in_specs=[pl.BlockSpec((TM, tk), lambda kk, j: (0, kk), pipeline_mode=pl.Buffered(1)),
                  pl.BlockSpec((None, tk, tn), lambda kk, j: (layer, kk, j)),
                  xo_spec,
                  pl.BlockSpec((bs, 1, tn), lambda kk, j: (0, 0, gt_idx * n_j + j * kk))],
        out_specs=(xo_spec, pl.BlockSpec((tk, tn), lambda kk, j: (kk, j))),
        scratch_shapes=[pltpu.VMEM((n_j, TM, tn), F32)],
        compiler_params=_cparams(("arbitrary", "arbitrary")),
        name="down_proj_cast")(a, w, x, mod3)
    return out, wb


def _mix_ac_kernel(u_ref, v_ref, b_ref, c_ref, hc_ref, wt_ref, bias_ref, wc_ref, buf_ref,
                   ya_ref, yc_ref, bufnew_ref, *rest, seq_len, phases=PHASES, defer_to=None):
    zbuf = rest[-1]
    vn_ref = rest[0] if len(rest) == 2 else None
    bs, tt, w = u_ref.shape
    rows = bs * tt
    t = pl.program_id(T_AXIS)
    pad = 8

    if "init" in phases:
        @pl.when(t == 0)
        def _():
            zbuf[:, pad - 2:pad, :] = buf_ref[...]

    if "body" in phases:
        work = _mix_ac_work(u_ref, v_ref, b_ref, c_ref, hc_ref, wt_ref, bias_ref, wc_ref, ya_ref, yc_ref,
                            vn_ref, zbuf, pad, seq_len)
        if defer_to is None:
            for piece in work:
                piece()
        else:
            defer_to.extend(work)

    if "final" in phases:
        @pl.when(t == pl.num_programs(T_AXIS) - 1)
        def _():
            bufnew_ref[...] = zbuf[:, pad - 2:pad, :]


def _mix_ac_work(u_ref, v_ref, b_ref, c_ref, hc_ref, wt_ref, bias_ref, wc_ref, ya_ref, yc_ref, vn_ref, zbuf,
                 pad, seq_len):
    bs, tt, _ = u_ref.shape
    rows = bs * tt

    def piece(g):
        sl = slice(g * DH_A, (g + 1) * DH_A)
        zc = c_ref[:, :, sl] * hc_ref[:, :, sl]
        zbuf[:, pad:pad + tt, sl] = zc
        y = zbuf[:, pad - 2:pad - 2 + tt, sl] * wc_ref[0:1, sl]
        y = y + zbuf[:, pad - 1:pad - 1 + tt, sl] * wc_ref[1:2, sl]
        y = y + zc * wc_ref[2:3, sl]
        yc_ref[:, sl] = (b_ref[:, :, sl] * y).reshape(rows, DH_A).astype(BF16)
        zbuf[:, pad - 2:pad, sl] = zbuf[:, pad + tt - 2:pad + tt, sl]

        gu = _gelu_tanh(u_ref[:, :, sl].reshape(rows, DH_A))
        vg = _gelu_tanh(v_ref[:, :, sl].reshape(rows, DH_A))
        dv = vg - jnp.mean(vg, axis=-1, keepdims=True)
        vn = dv * lax.rsqrt(jnp.mean(dv * dv, axis=-1, keepdims=True) + EPS)
        ri = lax.broadcasted_iota(jnp.int32, (rows, rows), 0)
        ci = lax.broadcasted_iota(jnp.int32, (rows, rows), 1)
        keep = ci <= ri
        if seq_len < rows:
            keep = keep & ((ri // seq_len) == (ci // seq_len))
        wm = jnp.where(keep, wt_ref[g], 0.0).astype(BF16)
        mixed = jnp.dot(wm, vn.astype(BF16), preferred_element_type=F32) + bias_ref[:, sl]
        ya_ref[:, sl] = (gu * mixed).astype(BF16)
        if vn_ref is not None:
            vn_ref[:, :, sl] = vn.reshape(bs, tt, DH_A)

    return [functools.partial(piece, g) for g in range(G_A)]


def _mix_ac_parts(z_head, z_tail, wt, bias_full, w_conv, layer, buf, emit_v):
    n_seq, t, _ = z_head.shape
    bs, tt = _row_tiling(t, ROW_TILE)
    nt = t // tt
    rows = n_seq * t
    zspec = lambda c: pl.BlockSpec((bs, tt, W_A), lambda i, j, *_: (i, j, c))
    y_spec = pl.BlockSpec((ROW_TILE, W_A), lambda i, j, *_: (i * nt + j, 0))
    out_shape = [jax.ShapeDtypeStruct((rows, W_A), BF16),
                 jax.ShapeDtypeStruct((rows, W_C), BF16),
                 jax.ShapeDtypeStruct((n_seq, CONV_W - 1, W_C), F32)]
    out_specs = [y_spec, y_spec, pl.BlockSpec((bs, CONV_W - 1, W_C), lambda i, j, *_: (i, 0, 0))]
    if emit_v:
        out_shape.append(jax.ShapeDtypeStruct((n_seq, t, W_A), F32))
        out_specs.append(pl.BlockSpec((bs, tt, W_A), lambda i, j, *_: (i, j, 0)))
    in_specs = [zspec(0), zspec(1), zspec(0), zspec(1), zspec(2),
                pl.BlockSpec((G_A, ROW_TILE, ROW_TILE), lambda *_: (0, 0, 0)),
                pl.BlockSpec((ROW_TILE, W_A), lambda *_: (0, 0)),
                pl.BlockSpec((None, CONV_W, W_C), lambda *_: (layer, 0, 0)),
                pl.BlockSpec((None, bs, CONV_W - 1, W_C), lambda i, *_: (layer, i, 0, 0))]
    return dict(args=[z_head, z_head, z_tail, z_tail, z_tail, wt, bias_full, w_conv, buf],
                in_specs=in_specs, out_shape=out_shape, out_specs=out_specs,
                scratch=[pltpu.VMEM((bs, tt + 8, W_C), F32)], grid=(n_seq // bs, nt),
                kernel=functools.partial(_mix_ac_kernel, seq_len=min(t, CHUNK)))


def _mix_ac(*operands):
    m = _mix_ac_parts(*operands)
    return pl.pallas_call(
        m["kernel"], out_shape=m["out_shape"], grid=m["grid"], in_specs=m["in_specs"],
        out_specs=m["out_specs"], scratch_shapes=m["scratch"],
        compiler_params=_cparams(("parallel", "arbitrary")), name="mix_ac")(*m["args"])


def _gla_tables(rows, seq_len):
    i = np.arange(rows)[:, None]
    t = np.arange(rows)[None, :]
    window = min(seq_len, GLA_WINDOW)
    slabs = [((i // window) == (t // window)) & (t <= i)]
    masks = [i == t]
    s = seq_len // 2
    while s >= 1:
        second = (i // s) % 2 == 1
        start2 = (i // s) * s
        end1 = start2 + s - 1
        if s < MIN_REF_LEVEL:
            slabs.append(np.where(second, (t >= start2) & (t <= i), (t > i) & (t <= end1)))
        masks.append(((i // (2 * s)) == (t // (2 * s))) & second & ((t // s) % 2 == 0))
        s //= 2
    return (np.concatenate(slabs, 0).astype(np.float32), np.stack(masks).astype(np.float32))


def _block_ref_exponent(g, block, ref_row, flip_from):
    rows, dk = g.shape
    gr = g.reshape(rows // block, block, dk)
    d = gr - gr[:, ref_row:ref_row + 1, :]
    pos = lax.broadcasted_iota(jnp.int32, gr.shape, 1)
    return jnp.where(pos >= flip_from, d, -d).reshape(rows, dk)


def _double_restart(pre, suf, s):
    rows, dk = pre.shape
    pr = pre.reshape(rows // (2 * s), 2 * s, dk)
    sr = suf.reshape(rows // (2 * s), 2 * s, dk)
    second = lax.broadcasted_iota(jnp.int32, pr.shape, 1) >= s
    x = jnp.where(second, pr, sr)
    pre2 = jnp.where(second, pr + pr[:, s - 1:s, :], pr)
    suf2 = jnp.where(second, sr, sr + pr[:, 2 * s - 1:2 * s, :])
    return x.reshape(rows, dk), pre2.reshape(rows, dk), suf2.reshape(rows, dk)


def _gla_kernel(q_ref, k_ref, v_ref, r_ref, la_ref, mall_ref, mask_ref, g_ref, s0_ref, *rest, phases=PHASES,
                between=()):
    y_ref, snew_ref, s_scr = rest[-3:]
    t = pl.program_id(T_AXIS)

    if "init" in phases:
        @pl.when(t == 0)
        def _():
            s_scr[...] = s0_ref[...]

    if "body" in phases:
        _gla_body(q_ref, k_ref, v_ref, r_ref, la_ref, mall_ref, mask_ref, g_ref, y_ref, s_scr, between)

    if "final" in phases:
        @pl.when(t == pl.num_programs(T_AXIS) - 1)
        def _():
            snew_ref[...] = s_scr[...]


def _gla_body(q_ref, k_ref, v_ref, r_ref, la_ref, mall_ref, mask_ref, g_ref, y_ref, s_scr, between):
    bs, tt, _ = q_ref.shape
    hp = s_scr.shape[1]
    assert len(between) in (0, hp)
    rows = bs * tt
    n_lvl = mask_ref.shape[0] - 1
    ri = lax.broadcasted_iota(jnp.int32, (DK_B, DK_B), 0)
    ci = lax.broadcasted_iota(jnp.int32, (DK_B, DK_B), 1)
    eye = ri == ci
    nt_dims = (((1,), (1,)), ((), ()))
    tn_dims = (((0,), (0,)), ((), ()))

    for hh in range(hp):
        ks = slice(hh * DK_B, (hh + 1) * DK_B)
        vs = slice(hh * DV_B, (hh + 1) * DV_B)
        q = q_ref[:, :, ks].reshape(rows, DK_B) * (DK_B ** -0.5)
        k = k_ref[:, :, ks].reshape(rows, DK_B)
        v = v_ref[:, :, vs].reshape(rows, DV_B)
        vb = v.astype(BF16)
        la = la_ref[:, :, ks].reshape(rows, DK_B) * LOG2_E
        la_hi = la.astype(BF16)
        la_lo = (la - la_hi.astype(F32)).astype(BF16)
        x = jnp.dot(mall_ref[...], jnp.concatenate([la_hi, la_lo], axis=1), preferred_element_type=F32)
        x = x[:, :DK_B] + x[:, DK_B:]
        window = min(tt, GLA_WINDOW)
        pre = x[0:rows]
        suf = _block_ref_exponent(pre, window, window - 1, window)

        a = mask_ref[0] * lax.dot_general(q.astype(BF16), k.astype(BF16), nt_dims,
                                          preferred_element_type=F32)
        for lvl in reversed(range(n_lvl)):
            s = tt >> (lvl + 1)
            if s >= window:
                xl, pre, suf = _double_restart(pre, suf, s)
            elif s >= MIN_REF_LEVEL:
                xl = _block_ref_exponent(pre, 2 * s, s - 1, s)
            else:
                n_slab = 1 + lvl - (n_lvl - N_SMALL_LEVELS)
                xl = x[n_slab * rows:(n_slab + 1) * rows]
            el = jnp.exp2(xl)
            p = lax.dot_general((q * el).astype(BF16), (k * el).astype(BF16), nt_dims,
                                preferred_element_type=F32)
            a = a + mask_ref[lvl + 1] * p
        o_intra = jnp.dot(a.astype(BF16), vb, preferred_element_type=F32)
        e_g = jnp.exp2(pre)
        qg = q * e_g
        kd = k * jnp.exp2(suf)

        o_parts = []
        for b in range(bs):
            rs = slice(b * tt, (b + 1) * tt)
            s = s_scr[b, hh]
            o_parts.append(o_intra[rs] + jnp.dot(qg[rs].astype(BF16), s.astype(BF16),
                                                 preferred_element_type=F32))
            e_last = e_g[(b + 1) * tt - 1:(b + 1) * tt]
            e_col = jnp.sum(jnp.where(eye, jnp.broadcast_to(e_last, (DK_B, DK_B)), 0.0),
                            axis=1, keepdims=True)
            kv = lax.dot_general(kd[rs].astype(BF16), v[rs].astype(BF16), tn_dims,
                                 preferred_element_type=F32)
            s_scr[b, hh] = e_col * s + kv
        o = o_parts[0] if bs == 1 else jnp.concatenate(o_parts, axis=0)
        yn = o * lax.rsqrt(jnp.mean(o * o, axis=-1, keepdims=True) + EPS) * g_ref[...]
        y_ref[:, vs] = (yn * _silu(r_ref[:, :, vs].reshape(rows, DV_B))).astype(BF16)
        if between:
            between[hh]()


def _gla_parts(z_head, la, g_gla, s0, layer, s_out_prev):
    n_seq, t, _ = z_head.shape
    tile = ROW_TILE if t >= ROW_TILE else 2 * ROW_TILE
    bs, tt = _row_tiling(t, tile)
    hp = H_B if bs == 1 else 1
    nt = t // tt
    assert nt == 1 or hp == H_B
    rows = n_seq * t
    mall, masks = _gla_tables(tile, tt)
    mall = jnp.asarray(mall, BF16)
    masks = jnp.asarray(masks, F32)
    zs = lambda w, c0: pl.BlockSpec((bs, tt, hp * w), lambda i, j, h=0: (i, j, c0 // hp + h))
    s_spec = pl.BlockSpec((None, bs, hp, DK_B, DV_B), lambda i, j, h=0: (layer, i, h, 0, 0))
    in_specs = [zs(DK_B, ZQ0), zs(DK_B, ZK0), zs(DV_B, ZV0), zs(DV_B, ZR0),
                pl.BlockSpec((bs, tt, hp * DK_B), lambda i, j, h=0: (i, j, h)),
                pl.BlockSpec(mall.shape, lambda *_: (0, 0)),
                pl.BlockSpec(masks.shape, lambda *_: (0, 0, 0)),
                pl.BlockSpec((None, 1, DV_B), lambda *_: (layer, 0, 0)),
                s_spec]
    args = [z_head, z_head, z_head, z_head, la, mall, masks, g_gla, s0]
    if s_out_prev is not None:
        in_specs.append(pl.BlockSpec(memory_space=pl.ANY))
        args.append(s_out_prev)
    return dict(args=args, in_specs=in_specs, aliased=s_out_prev is not None,
                out_shape=[jax.ShapeDtypeStruct((rows, W_B), BF16),
                           jax.ShapeDtypeStruct((DEPTH, n_seq, H_B, DK_B, DV_B), F32)],
                out_specs=[pl.BlockSpec((tile, hp * DV_B), lambda i, j, h=0: (i * nt + j, h)), s_spec],
                scratch=[pltpu.VMEM((bs, hp, DK_B, DV_B), F32)], grid=(n_seq // bs, nt, H_B // hp))


def _gla(*operands):
    m = _gla_parts(*operands)
    aliases = {len(m["args"]) - 1: 1} if m["aliased"] else {}
    return pl.pallas_call(
        _gla_kernel, out_shape=m["out_shape"], grid=m["grid"], in_specs=m["in_specs"],
        out_specs=m["out_specs"], scratch_shapes=m["scratch"], input_output_aliases=aliases,
        compiler_params=_cparams(("parallel", "arbitrary", "arbitrary")), name="gla")(*m["args"])


def _mix_gla_kernel(*refs, n_in, n_out, mix_kernel):
    (mi, gi), (mo, go) = n_in, n_out
    o0 = mi + gi
    s0 = o0 + mo + go
    mix_refs = (*refs[:mi], *refs[o0:o0 + mo], refs[s0])
    gla_refs = (*refs[mi:o0], *refs[o0 + mo:s0], refs[s0 + 1])
    for phase in PHASES:
        pieces = []
        mix_kernel(*mix_refs, phases=(phase,), defer_to=pieces)
        _gla_kernel(*gla_refs, phases=(phase,), between=pieces)


def _mix_gla(mix_operands, gla_operands):
    m = _mix_ac_parts(*mix_operands)
    g = _gla_parts(*gla_operands)
    assert m["grid"] == g["grid"][:2] and g["grid"][2] == 1
    n_in = (len(m["args"]), len(g["args"]))
    n_out = (len(m["out_shape"]), len(g["out_shape"]))
    aliases = {sum(n_in) - 1: n_out[0] + 1} if g["aliased"] else {}
    res = pl.pallas_call(
        functools.partial(_mix_gla_kernel, n_in=n_in, n_out=n_out, mix_kernel=m["kernel"]),
        out_shape=m["out_shape"] + g["out_shape"], grid=m["grid"],
        in_specs=m["in_specs"] + g["in_specs"], out_specs=m["out_specs"] + g["out_specs"],
        scratch_shapes=m["scratch"] + g["scratch"], input_output_aliases=aliases,
        compiler_params=_cparams(("parallel", "arbitrary")), name="mix_gla")(*m["args"], *g["args"])
    return res[:n_out[0]], res[n_out[0]:]


def _layer(x, layer, mod3, s_gla, buf_conv, p, wts, s_out_prev, emit_v):
    n_seq, t, d = x.shape
    seq_len = min(t, CHUNK)
    reps = ROW_TILE // seq_len
    h, la = _norm_call(x, p["g_mix"], mod3, 1, 0, gate_w=(p["w_a"], p["w_a2"], p["b_a2"]))
    z_head, wb_head = _in_proj(h, wts["w_head"], layer, A_OFF, n_seq, t)
    if wts["w_tail"] is None:
        z_tail, wb_tail = _in_proj_tail_cast(h, wts["w_head"], layer, n_seq, t)
    else:
        z_tail, wb_tail = _in_proj(h, wts["w_tail"], layer, P_TAIL, n_seq, t)
    wt = jnp.tile(p["w_s"][:, :seq_len, :seq_len], (1, reps, reps))
    bias_full = jnp.repeat(jnp.tile(p["b_s"][:, :seq_len].T, (reps, 1)), DH_A, axis=1)
    mix_ops = (z_head, z_tail, wt, bias_full, p["w_conv"], layer, buf_conv, emit_v)
    gla_ops = (z_head, la, p["g_gla"], s_gla, layer, s_out_prev)
    if t >= ROW_TILE:
        mix, (yb, s_out) = _mix_gla(mix_ops, gla_ops)
    else:
        mix = _mix_ac(*mix_ops)
        yb, s_out = _gla(*gla_ops)
    ya, yc, buf_new = mix[:3]
    vn = mix[3] if emit_v else None
    x, wb_out = _out_proj(ya, yb, yc, wts["w_out"], layer, x, mod3, 2)
    h2 = _norm_call(x, p["g_ffn"], mod3, 4, 3)
    a, wb_gate, wb_up = _swiglu(h2, wts["w_gate"], wts["w_up"], layer)
    x, wb_down = _down_proj(a, wts["w_down"], layer, x, mod3, 5)
    wb = dict(w_head=wb_head, w_tail=wb_tail, w_out=wb_out, w_gate=wb_gate, w_up=wb_up, w_down=wb_down)
    return x, s_out, buf_new, vn, wb


def kernel(x_prompt, x_sample, state_gla, state_conv, c_prompt, c_sample, g_mix, g_ffn, w_mod, b_mod,
           w_in, w_s, b_s, w_a2, b_a2, g_gla, w_conv, w_out, w_gate, w_up, w_down, g_final):
    bp = x_prompt.shape[0]
    bd = x_sample.shape[0]
    n_c = bp + bd
    mc = -(-n_c // 16) * 16
    c_all = jnp.concatenate([c_prompt, c_sample, jnp.zeros((mc - n_c, D_MODEL), F32)], axis=0)
    mod = _modulation(c_all, w_mod, b_mod)
    gf = g_final.reshape(1, D_MODEL)
    g_gla3 = g_gla.reshape(DEPTH, 1, DV_B)
    gla0 = jnp.zeros((DEPTH, bp, H_B, DK_B, DV_B), F32)
    conv0 = jnp.zeros((DEPTH, bp, CONV_W - 1, W_C), F32)

    xs, xp = x_sample, x_prompt
    gla_s = gla_p = None
    conv_s, conv_p, v_s = [], [], []
    wt_in = jnp.swapaxes(w_in, 1, 2)
    for l in range(DEPTH):
        w_a = wt_in[l, A_OFF:A_OFF + GATE_RANK, :].T
        p = dict(
            g_mix=g_mix[l].reshape(1, D_MODEL), g_ffn=g_ffn[l].reshape(1, D_MODEL),
            w_a=jnp.pad(w_a, ((0, 0), (0, LANES - GATE_RANK))),
            w_a2=jnp.pad(w_a2[l], ((0, LANES - GATE_RANK), (0, 0))).astype(BF16),
            b_a2=b_a2[l].reshape(1, KW_B),
            w_s=w_s[l], b_s=b_s[l], g_gla=g_gla3, w_conv=w_conv)
        w_f32 = dict(w_head=wt_in, w_tail=None, w_out=w_out,
                     w_gate=w_gate, w_up=w_up, w_down=w_down)
        mod_s = mod[l, bp:n_c].reshape(bd, 1, N_MOD * D_MODEL)
        mod_p = mod[l, :bp].reshape(bp, 1, N_MOD * D_MODEL)
        xs, gla_s, buf_s, vn_s, w_bf16 = _layer(xs, l, mod_s, state_gla, state_conv, p, w_f32, gla_s, True)
        xp, gla_p, buf_p, _, _ = _layer(xp, l, mod_p, gla0, conv0, p, w_bf16, gla_p, False)
        conv_s.append(buf_s)
        conv_p.append(buf_p)
        v_s.append(vn_s)
    y_s = _final_norm(xs, gf)
    y_p = _final_norm(xp, gf)
    return (y_p, y_s, gla_p, jnp.stack(conv_p), gla_s, jnp.stack(conv_s), jnp.stack(v_s))
```
